```python
import math
import jax, jax.numpy as jnp
from jax import lax
import numpy as np

D_MODEL = 1024
BATCH = 16
SEQ = 4096
DEPTH = 2

CTX_LEN = 256
GRID_W = 64
N_BRANCH = 4
MIX_W = 256
GLA_HEADS = 4
GLA_DK = 32
GLA_DV = 64
GLA_RANK = 16
GLA_TAU = 16.0
HY_WIDTH = 256
HY_ORDER = 2
HY_BANDS = 16
HY_FEAT = 1 + 2 * HY_BANDS
HY_FFN = 64
HY_SHIFT = 0.05
HG_HEADS = 4
HG_DK = 64
HG_DV = 64
DA_HEADS = 4
DA_DK = 32
DA_DV = 64
ROPE_BASE = 10000.0
Q_BLOCK = 128
CHUNK = 64
N_EXPERTS = 16
EXPERT_FF = 1024
CAPACITY_FACTOR = 2
ADA_CHUNKS = 6
EPS = 1e-6
F_TINY = 1e-20

IN_NAMES = ('gla_q', 'gla_k', 'gla_v', 'gla_af', 'gla_ab', 'gla_g', 'hy',
            'hg_q', 'hg_ff', 'hg_fb', 'hg_i', 'hg_g', 'da_q', 'da_k', 'da_v', 'merge')
IN_WIDTHS = (GLA_HEADS * GLA_DK, GLA_HEADS * GLA_DK, GLA_HEADS * GLA_DV, GLA_RANK, GLA_RANK, GLA_HEADS * GLA_DV,
             (1 + HY_ORDER) * HY_WIDTH,
             HG_HEADS * HG_DK, HG_HEADS * HG_DK, HG_HEADS * HG_DK, HG_HEADS * HG_DV, HG_HEADS * HG_DV,
             DA_HEADS * 2 * DA_DK, DA_HEADS * 2 * DA_DK, DA_HEADS * DA_DV,
             N_BRANCH * D_MODEL)
N_IN = sum(IN_WIDTHS)

kernel_name = 'hybrid_flow_backbone'


def rms_norm(x, g):
    xf = x.astype(jnp.float32)
    y = xf * lax.rsqrt(jnp.mean(xf * xf, axis=-1, keepdims=True) + EPS)
    return (y * g.astype(jnp.float32)).astype(x.dtype)


def modulate(x, g, shift, scale):
    return rms_norm(x, g) * (1 + scale) + shift


def heads(t, n):
    B, L, W = t.shape
    return t.reshape(B, L, n, W // n).transpose(0, 2, 1, 3)


def merge_heads(t):
    B, H, L, d = t.shape
    return t.transpose(0, 2, 1, 3).reshape(B, L, H * d)


def short_conv(u, w):
    up = jnp.pad(u, ((0, 0), (1, 1), (0, 0)))
    return up[:, :-2] * w[0] + up[:, 1:-1] * w[1] + up[:, 2:] * w[2]


def chunk_recurrence(q, k, v, log_g, s0):
    B, H, L, dk = q.shape
    dv = v.shape[-1]
    n = L // CHUNK

    def to_chunks(t):
        return t.astype(jnp.float32).reshape(B, H, n, CHUNK, t.shape[-1]).transpose(2, 0, 1, 3, 4)

    mask = jnp.tril(jnp.ones((CHUNK, CHUNK), bool))[:, :, None]

    def step(S, xs):
        qc, kc, vc, gc = xs
        b = jnp.cumsum(gc, axis=-2)
        o_inter = jnp.einsum('bhtd,bhde->bhte', qc * jnp.exp(b), S)
        diff = b[:, :, :, None, :] - b[:, :, None, :, :]
        decay = jnp.where(mask, jnp.exp(jnp.where(mask, diff, 0.0)), 0.0)
        scores = jnp.einsum('bhtd,bhsd,bhtsd->bhts', qc, kc, decay)
        o_intra = jnp.einsum('bhts,bhse->bhte', scores, vc)
        b_end = b[:, :, -1:, :]
        S_new = jnp.exp(b_end[:, :, 0, :, None]) * S + jnp.einsum('bhsd,bhse->bhde', kc * jnp.exp(b_end - b), vc)
        return S_new, o_inter + o_intra

    S_fin, o = lax.scan(step, s0, (to_chunks(q), to_chunks(k), to_chunks(v), to_chunks(log_g)))
    o = o.transpose(1, 2, 0, 3, 4).reshape(B, H, L, dv)
    return o.astype(v.dtype), S_fin


def bidir_recurrence(ctx_in, lat_in):
    qc, kfc, kbc, vc, gfc, gbc = ctx_in
    ql, kfl, kbl, vl, gfl, gbl = lat_in
    B, H, _, dk = qc.shape
    s0 = jnp.zeros((B, H, dk, vc.shape[-1]), jnp.float32)
    flip = lambda t: jnp.flip(t, axis=2)
    oc_f, sc_f = chunk_recurrence(qc, kfc, vc, gfc, s0)
    ol_f, _ = chunk_recurrence(ql, kfl, vl, gfl, sc_f)
    oc_b, sc_b = chunk_recurrence(flip(qc), flip(kbc), flip(vc), flip(gbc), s0)
    ol_b, _ = chunk_recurrence(flip(ql), flip(kbl), flip(vl), flip(gbl), sc_b)
    return oc_f + flip(oc_b), ol_f + flip(ol_b)


def gla_inputs(parts, wa2, ba):
    q = heads(parts['gla_q'], GLA_HEADS) * (GLA_DK ** -0.5)
    k = heads(parts['gla_k'], GLA_HEADS)
    v = heads(parts['gla_v'], GLA_HEADS)
    gf = heads(jax.nn.log_sigmoid((parts['gla_af'] @ wa2[0] + ba[0]).astype(jnp.float32)) / GLA_TAU, GLA_HEADS)
    gb = heads(jax.nn.log_sigmoid((parts['gla_ab'] @ wa2[1] + ba[1]).astype(jnp.float32)) / GLA_TAU, GLA_HEADS)
    return (q, k, k, v, gf, gb)


def hgrn_inputs(parts, lb):
    q = heads(jax.nn.silu(parts['hg_q']), HG_HEADS)
    v = heads(parts['hg_i'], HG_HEADS)
    res = []
    for d, name in enumerate(('hg_ff', 'hg_fb')):
        z = parts[name].astype(jnp.float32)
        f = lb[d] + (1 - lb[d]) * jax.nn.sigmoid(z)
        log_f = jnp.log(jnp.maximum(f, F_TINY))
        k = (1 - lb[d]) * jax.nn.sigmoid(-z)
        res.append((heads(k, HG_HEADS), heads(log_f, HG_HEADS)))
    (kf, gf), (kb, gb) = res
    return (q, kf, kb, v, gf, gb)


def gated_branch(o, g, gate):
    return merge_heads(rms_norm(o, g)) * jax.nn.silu(gate)


def hyena_filters(L, p):
    j = jnp.arange(L, dtype=jnp.float32)
    t = j / max(L - 1, 1)
    w = 2 * math.pi * j / L
    f = jnp.linspace(1e-4, HY_BANDS - 1, HY_BANDS, dtype=jnp.float32)
    feats = jnp.concatenate([t[:, None], jnp.cos(w[:, None] * f), -jnp.sin(w[:, None] * f)], axis=-1)
    h = jnp.sin(p['hy_freq'][0] * (feats @ p['hy_w1'] + p['hy_b1']))
    h = jnp.sin(p['hy_freq'][1] * (h @ p['hy_w2'] + p['hy_b2']))
    h = (h @ p['hy_w3']).astype(jnp.float32)
    dist = jnp.abs(j - L // 2) / (L // 2)
    h = h * (jnp.exp(-dist[:, None] * jnp.abs(p['hy_decay'].astype(jnp.float32))) + HY_SHIFT)
    h = h / jnp.sum(jnp.abs(h), axis=0, keepdims=True)
    return h.reshape(L, HY_ORDER, HY_WIDTH)


def fft_conv(u, h):
    L = u.shape[1]
    n = 2 * L
    U = jnp.fft.rfft(u.astype(jnp.float32), n=n, axis=1)
    Hf = jnp.fft.rfft(h, n=n, axis=0)
    y = jnp.fft.irfft(U * Hf[None], n=n, axis=1)[:, L // 2: L // 2 + L]
    return y.astype(u.dtype)


def hyena_branch(u3, p):
    L = u3.shape[1]
    v, x1, x2 = jnp.split(short_conv(u3, p['hy_conv_w']), 1 + HY_ORDER, axis=-1)
    h = hyena_filters(L, p)
    z = v
    for o, gate in enumerate((x1, x2)):
        z = gate * (fft_conv(z, h[:, o]) + p['hy_bias'][o] * z)
    return z


def rope_axis(t, pos):
    half = t.shape[-1] // 2
    freqs = ROPE_BASE ** (-jnp.arange(half, dtype=jnp.float32) / half)
    ang = pos.astype(jnp.float32)[:, None] * freqs
    cos, sin = jnp.cos(ang).astype(t.dtype), jnp.sin(ang).astype(t.dtype)
    t1, t2 = t[..., :half], t[..., half:]
    return jnp.concatenate([t1 * cos - t2 * sin, t1 * sin + t2 * cos], axis=-1)


def rope2d(t, row, col):
    a = t.shape[-1] // 2
    return jnp.concatenate([rope_axis(t[..., :a], row), rope_axis(t[..., a:], col)], axis=-1)


def da_qk(t, g):
    B, L, _ = t.shape
    t = rms_norm(t.reshape(B, L, DA_HEADS, 2, DA_DK), g).transpose(0, 2, 1, 3, 4)
    return t[..., 0, :], t[..., 1, :]


def diff_attend(q1, q2, k1, k2, v, lam):
    B, H, Lq, d = q1.shape
    nb = Lq // Q_BLOCK
    scale = d ** -0.5

    def blocks(t):
        return t.reshape(B, H, nb, Q_BLOCK, d).transpose(2, 0, 1, 3, 4)

    def one(qs):
        qa, qb = qs
        p1 = jax.nn.softmax(jnp.einsum('bhqd,bhkd->bhqk', qa, k1).astype(jnp.float32) * scale, axis=-1)
        p2 = jax.nn.softmax(jnp.einsum('bhqd,bhkd->bhqk', qb, k2).astype(jnp.float32) * scale, axis=-1)
        return jnp.einsum('bhqk,bhkd->bhqd', (p1 - lam * p2).astype(v.dtype), v)

    o = lax.map(one, (blocks(q1), blocks(q2)))
    return o.transpose(1, 2, 0, 3, 4).reshape(B, H, Lq, v.shape[-1])


def merge_branches(outs, gate_cols, w_branch, w_out):
    B, L, _ = gate_cols.shape
    gates = jax.nn.sigmoid(gate_cols.reshape(B, L, N_BRANCH, D_MODEL))
    merged = gates[:, :, 0] * (outs[0] @ w_branch[0])
    for i in range(1, N_BRANCH):
        merged = merged + gates[:, :, i] * (outs[i] @ w_branch[i])
    return merged @ w_out


def token_mixer(hc, hx, p, lb, row, col, lam_init, last):
    split_idx = np.cumsum(IN_WIDTHS)[:-1].tolist()
    pc = dict(zip(IN_NAMES, jnp.split(hc @ p['w_in'], split_idx, axis=-1)))
    px = dict(zip(IN_NAMES, jnp.split(hx @ p['w_in'], split_idx, axis=-1)))
    gla_c, gla_x = bidir_recurrence(gla_inputs(pc, p['gla_wa2'], p['gla_ba']), gla_inputs(px, p['gla_wa2'], p['gla_ba']))
    hg_c, hg_x = bidir_recurrence(hgrn_inputs(pc, lb), hgrn_inputs(px, lb))
    q1c, q2c = da_qk(pc['da_q'], p['da_qnorm_g'])
    k1c, k2c = da_qk(pc['da_k'], p['da_knorm_g'])
    vc = heads(pc['da_v'], DA_HEADS)
    q1x, q2x = [rope2d(t, row, col) for t in da_qk(px['da_q'], p['da_qnorm_g'])]
    k1x, k2x = [rope2d(t, row, col) for t in da_qk(px['da_k'], p['da_knorm_g'])]
    vx = heads(px['da_v'], DA_HEADS)
    lp = p['da_lam'].astype(jnp.float32)
    lam = jnp.exp(jnp.sum(lp[0] * lp[1])) - jnp.exp(jnp.sum(lp[2] * lp[3])) + lam_init
    cat = lambda a, b: jnp.concatenate([a, b], axis=2)
    da_x = diff_attend(q1x, q2x, cat(k1c, k1x), cat(k2c, k2x), cat(vc, vx), lam)

    def finish(parts, gla_o, hg_o, da_o):
        outs = (gated_branch(gla_o, p['gla_norm_g'], parts['gla_g']),
                hyena_branch(parts['hy'], p),
                gated_branch(hg_o, p['hg_norm_g'], parts['hg_g']),
                merge_heads(rms_norm(da_o, p['da_norm_g']) * (1 - lam_init)))
        return merge_branches(outs, parts['merge'], p['w_branch'], p['w_out'])

    mx = finish(px, gla_x, hg_x, da_x)
    if last:
        return None, mx
    da_c = diff_attend(q1c, q2c, k1c, k2c, vc, lam)
    return finish(pc, gla_c, hg_c, da_c), mx


def expert_choice_moe(h, router, w1, w3, w2):
    B, L, D = h.shape
    cap = CAPACITY_FACTOR * L // N_EXPERTS
    aff = jax.nn.softmax((h @ router).astype(jnp.float32), axis=-1)
    g, idx = lax.top_k(aff.transpose(0, 2, 1), cap)
    xg = jax.vmap(lambda hb, ib: hb[ib])(h, idx)
    a = jnp.einsum('becd,edf->becf', xg, w1)
    b = jnp.einsum('becd,edf->becf', xg, w3)
    y = jnp.einsum('becf,efd->becd', jax.nn.silu(a) * b, w2) * g[..., None].astype(h.dtype)
    return jax.vmap(lambda yb, ib: jnp.zeros((L, D), h.dtype).at[ib.reshape(-1)].add(yb.reshape(-1, D)))(y, idx)


def setup_inputs(seed: int = 0) -> dict:
    key = jax.random.key(seed)
    ks = jax.random.split(key, 34)
    D = D_MODEL
    nrm = lambda k, shape, s: jax.random.normal(k, shape, jnp.float32) * s
    return {
        'x': nrm(ks[0], (BATCH, SEQ, D), 1.0),
        'c': nrm(ks[1], (BATCH, D), 1.0),
        'ctx': nrm(ks[2], (BATCH, CTX_LEN, D), 1.0),
        'c_ctx': nrm(ks[3], (D,), 1.0),
        'ada_w': nrm(ks[4], (DEPTH, D, ADA_CHUNKS * D), 0.5 * D ** -0.5),
        'ada_b': nrm(ks[5], (DEPTH, ADA_CHUNKS * D), 0.02),
        'norm1_g': 1 + nrm(ks[6], (DEPTH, D), 0.02),
        'norm2_g': 1 + nrm(ks[7], (DEPTH, D), 0.02),
        'w_in': nrm(ks[8], (DEPTH, D, N_IN), D ** -0.5),
        'gla_wa2': nrm(ks[9], (DEPTH, 2, GLA_RANK, GLA_HEADS * GLA_DK), GLA_RANK ** -0.5),
        'gla_ba': nrm(ks[10], (DEPTH, 2, GLA_HEADS * GLA_DK), 0.1),
        'gla_norm_g': 1 + nrm(ks[11], (DEPTH, GLA_DV), 0.02),
        'hy_conv_w': nrm(ks[12], (DEPTH, 3, (1 + HY_ORDER) * HY_WIDTH), 3 ** -0.5),
        'hy_w1': nrm(ks[13], (DEPTH, HY_FEAT, HY_FFN), HY_FEAT ** -0.5),
        'hy_b1': nrm(ks[14], (DEPTH, HY_FFN), 0.1),
        'hy_w2': nrm(ks[15], (DEPTH, HY_FFN, HY_FFN), HY_FFN ** -0.5),
        'hy_b2': nrm(ks[16], (DEPTH, HY_FFN), 0.1),
        'hy_w3': nrm(ks[17], (DEPTH, HY_FFN, HY_ORDER * HY_WIDTH), HY_FFN ** -0.5),
        'hy_freq': 1 + nrm(ks[18], (DEPTH, 2, HY_FFN), 0.02),
        'hy_decay': jnp.linspace(3.0, 15.0, HY_ORDER * HY_WIDTH, dtype=jnp.float32)[None] + nrm(ks[19], (DEPTH, HY_ORDER * HY_WIDTH), 0.1),
        'hy_bias': nrm(ks[20], (DEPTH, HY_ORDER, HY_WIDTH), 0.5),
        'hg_lower': nrm(ks[21], (DEPTH, 2, HG_HEADS * HG_DK), 0.1),
        'hg_norm_g': 1 + nrm(ks[22], (DEPTH, HG_DV), 0.02),
        'da_qnorm_g': 1 + nrm(ks[23], (DEPTH, DA_DK), 0.02),
        'da_knorm_g': 1 + nrm(ks[24], (DEPTH, DA_DK), 0.02),
        'da_lam': nrm(ks[25], (DEPTH, 4, DA_DK), 0.1),
        'da_norm_g': 1 + nrm(ks[26], (DEPTH, DA_DV), 0.02),
        'w_branch': nrm(ks[27], (DEPTH, N_BRANCH, MIX_W, D), MIX_W ** -0.5),
        'w_out': nrm(ks[28], (DEPTH, D, D), D ** -0.5),
        'moe_router': nrm(ks[29], (DEPTH, D, N_EXPERTS), D ** -0.5),
        'moe_w1': nrm(ks[30], (DEPTH, N_EXPERTS, D, EXPERT_FF), D ** -0.5),
        'moe_w3': nrm(ks[31], (DEPTH, N_EXPERTS, D, EXPERT_FF), D ** -0.5),
        'moe_w2': nrm(ks[32], (DEPTH, N_EXPERTS, EXPERT_FF, D), EXPERT_FF ** -0.5),
    }


def reference(x, c, ctx, c_ctx, ada_w, ada_b, norm1_g, norm2_g, w_in, gla_wa2, gla_ba, gla_norm_g,
              hy_conv_w, hy_w1, hy_b1, hy_w2, hy_b2, hy_w3, hy_freq, hy_decay, hy_bias,
              hg_lower, hg_norm_g, da_qnorm_g, da_knorm_g, da_lam, da_norm_g, w_branch, w_out,
              moe_router, moe_w1, moe_w3, moe_w2):
    L = x.shape[1]
    ROWS = L // GRID_W
    row = jnp.repeat(jnp.arange(ROWS), GRID_W)
    col = jnp.tile(jnp.arange(GRID_W), ROWS)
    P = jax.nn.softmax(hg_lower.astype(jnp.float32), axis=0)
    lower = jnp.cumsum(P, axis=0) - P[0]
    sc = jax.nn.silu(c)
    scc = jax.nn.silu(c_ctx)
    xc, xx = ctx, x
    for l in range(DEPTH):
        last = l == DEPTH - 1
        lam_init = 0.8 - 0.6 * math.exp(-0.3 * l)
        p = {'w_in': w_in[l], 'gla_wa2': gla_wa2[l], 'gla_ba': gla_ba[l], 'gla_norm_g': gla_norm_g[l],
             'hy_conv_w': hy_conv_w[l], 'hy_w1': hy_w1[l], 'hy_b1': hy_b1[l], 'hy_w2': hy_w2[l], 'hy_b2': hy_b2[l],
             'hy_w3': hy_w3[l], 'hy_freq': hy_freq[l], 'hy_decay': hy_decay[l], 'hy_bias': hy_bias[l],
             'hg_norm_g': hg_norm_g[l], 'da_qnorm_g': da_qnorm_g[l], 'da_knorm_g': da_knorm_g[l],
             'da_lam': da_lam[l], 'da_norm_g': da_norm_g[l], 'w_branch': w_branch[l], 'w_out': w_out[l]}
        mod_x = jnp.split((sc @ ada_w[l] + ada_b[l])[:, None, :], ADA_CHUNKS, axis=-1)
        mod_c = jnp.split((scc @ ada_w[l] + ada_b[l])[None, None, :], ADA_CHUNKS, axis=-1)
        hx = modulate(xx, norm1_g[l], mod_x[0], mod_x[1])
        hc = modulate(xc, norm1_g[l], mod_c[0], mod_c[1])
        mc, mx = token_mixer(hc, hx, p, lower[l], row, col, lam_init, last)
        xx = xx + mod_x[2] * mx
        hx = modulate(xx, norm2_g[l], mod_x[3], mod_x[4])
        xx = xx + mod_x[5] * expert_choice_moe(hx, moe_router[l], moe_w1[l], moe_w3[l], moe_w2[l])
        if not last:
            xc = xc + mod_c[2] * mc
            hc = modulate(xc, norm2_g[l], mod_c[3], mod_c[4])
            xc = xc + mod_c[5] * expert_choice_moe(hc, moe_router[l], moe_w1[l], moe_w3[l], moe_w2[l])
    return xx
```

```python
import functools
import math

import jax
import jax.numpy as jnp
import numpy as np
from jax import lax
from jax.experimental import pallas as pl
from jax.experimental.pallas import tpu as pltpu

D_MODEL = 1024
DEPTH = 2
GRID_W = 64
N_BRANCH = 4
MIX_W = 256
GLA_HEADS = 4
GLA_DK = 32
GLA_DV = 64
GLA_RANK = 16
GLA_TAU = 16.0
HY_WIDTH = 256
HY_ORDER = 2
HY_BANDS = 16
HY_SHIFT = 0.05
HG_HEADS = 4
HG_DK = 64
HG_DV = 64
DA_HEADS = 4
DA_DK = 32
DA_DV = 64
ROPE_BASE = 10000.0
Q_BLOCK = 128
CHUNK = 64
N_EXPERTS = 16
EXPERT_FF = 1024
CAPACITY_FACTOR = 2
ADA_CHUNKS = 6
EPS = 1e-6
F_TINY = 1e-20

IN_NAMES = ('gla_q', 'gla_k', 'gla_v', 'gla_af', 'gla_ab', 'gla_g', 'hy',
            'hg_q', 'hg_ff', 'hg_fb', 'hg_i', 'hg_g', 'da_q', 'da_k', 'da_v', 'merge')
IN_WIDTHS = (GLA_HEADS * GLA_DK, GLA_HEADS * GLA_DK, GLA_HEADS * GLA_DV, GLA_RANK, GLA_RANK, GLA_HEADS * GLA_DV,
             (1 + HY_ORDER) * HY_WIDTH,
             HG_HEADS * HG_DK, HG_HEADS * HG_DK, HG_HEADS * HG_DK, HG_HEADS * HG_DV, HG_HEADS * HG_DV,
             DA_HEADS * 2 * DA_DK, DA_HEADS * 2 * DA_DK, DA_HEADS * DA_DV,
             N_BRANCH * D_MODEL)
IN_OFFSETS = tuple(int(v) for v in np.cumsum((0,) + IN_WIDTHS)[:-1])

LANES = 128
PROJ_GROUPS = (
    (('gla_q', 'gla_k', 'gla_v', 'gla_g', 'gla_af', 'gla_ab'), 896),
    (('hy',), 768),
    (('hg_q', 'hg_ff', 'hg_fb', 'hg_i', 'hg_g'), 1280),
    (('da_q', 'da_k', 'da_v'), 768),
    (('merge',), 4096),
)
PROJ_WIDTHS = tuple(w for _, w in PROJ_GROUPS)
VMEM_LIMIT = 56 * 1024 * 1024

BF16 = jnp.bfloat16
F32 = jnp.float32


def _in_proj_kernel(x_ref, g_ref, sh_ref, sc_ref, w_ref, *out_refs):
    x = x_ref[0]
    ms = jnp.mean(x * x, axis=-1, keepdims=True)
    h = x * lax.rsqrt(ms + EPS) * g_ref[...]
    h = (h * (1.0 + sc_ref[0]) + sh_ref[0]).astype(BF16)
    off = 0
    for o_ref, w in zip(out_refs, PROJ_WIDTHS):
        o_ref[0] = jnp.dot(h, w_ref[:, off:off + w], preferred_element_type=F32)
        off += w


def _in_proj(x, g, shift, scale, w_groups, tm):
    B, L, D = x.shape
    nw = w_groups.shape[1]
    return pl.pallas_call(
        _in_proj_kernel,
        grid=(B, L // tm),
        in_specs=[
            pl.BlockSpec((1, tm, D), lambda b, i: (b, i, 0)),
            pl.BlockSpec((1, D), lambda b, i: (0, 0)),
            pl.BlockSpec((1, 1, D), lambda b, i: (b, 0, 0)),
            pl.BlockSpec((1, 1, D), lambda b, i: (b, 0, 0)),
            pl.BlockSpec((D, nw), lambda b, i: (0, 0), pipeline_mode=pl.Buffered(1)),
        ],
        out_specs=[pl.BlockSpec((1, tm, w), lambda b, i: (b, i, 0)) for w in PROJ_WIDTHS],
        out_shape=[jax.ShapeDtypeStruct((B, L, w), F32) for w in PROJ_WIDTHS],
        compiler_params=pltpu.CompilerParams(
            dimension_semantics=("parallel", "parallel"), vmem_limit_bytes=VMEM_LIMIT),
        name="in_proj",
    )(x, g.reshape(1, D), shift, scale, w_groups)


def _merge_kernel(o0_ref, o1_ref, o2_ref, o3_ref, gate_ref, wb_ref, wo_ref, x_ref, m_ref, out_ref):
    D = D_MODEL
    acc = None
    for i, o_ref in enumerate((o0_ref, o1_ref, o2_ref, o3_ref)):
        t = jnp.dot(o_ref[0].astype(BF16), wb_ref[i], preferred_element_type=F32)
        t = jax.nn.sigmoid(gate_ref[0, :, i * D:(i + 1) * D]) * t
        acc = t if acc is None else acc + t
    mx = jnp.dot(acc.astype(BF16), wo_ref[...], preferred_element_type=F32)
    out_ref[0] = x_ref[0] + m_ref[0] * mx


def _merge_out(outs, gate_cols, w_branch, w_out, x, m, tm):
    B, L, D = x.shape
    return pl.pallas_call(
        _merge_kernel,
        grid=(B, L // tm),
        in_specs=[pl.BlockSpec((1, tm, MIX_W), lambda b, i: (b, i, 0)) for _ in range(N_BRANCH)] + [
            pl.BlockSpec((1, tm, N_BRANCH * D), lambda b, i: (b, i, 0)),
            pl.BlockSpec((N_BRANCH, MIX_W, D), lambda b, i: (0, 0, 0)),
            pl.BlockSpec((D, D), lambda b, i: (0, 0)),
            pl.BlockSpec((1, tm, D), lambda b, i: (b, i, 0)),
            pl.BlockSpec((1, 1, D), lambda b, i: (b, 0, 0)),
        ],
        out_specs=pl.BlockSpec((1, tm, D), lambda b, i: (b, i, 0)),
        out_shape=jax.ShapeDtypeStruct((B, L, D), F32),
        compiler_params=pltpu.CompilerParams(
            dimension_semantics=("parallel", "parallel"), vmem_limit_bytes=VMEM_LIMIT),
        name="merge_out",
    )(*outs, gate_cols, w_branch, w_out, x, m)


def _expert_ffn_kernel(x_ref, g_ref, w1_ref, w3_ref, w2_ref, out_ref):
    x = x_ref[0, 0].astype(BF16)
    a = jnp.dot(x, w1_ref[0], preferred_element_type=F32)
    b = jnp.dot(x, w3_ref[0], preferred_element_type=F32)
    h = (a * jax.nn.sigmoid(a) * b).astype(BF16)
    y = jnp.dot(h, w2_ref[0], preferred_element_type=F32)
    out_ref[0, 0] = y * g_ref[0, 0]


def _expert_ffn(xg, g, w1, w3, w2):
    B, E, cap, D = xg.shape
    F = w1.shape[-1]
    return pl.pallas_call(
        _expert_ffn_kernel,
        grid=(E, B),
        in_specs=[
            pl.BlockSpec((1, 1, cap, D), lambda e, b: (b, e, 0, 0)),
            pl.BlockSpec((1, 1, cap, 1), lambda e, b: (b, e, 0, 0)),
            pl.BlockSpec((1, D, F), lambda e, b: (e, 0, 0)),
            pl.BlockSpec((1, D, F), lambda e, b: (e, 0, 0)),
            pl.BlockSpec((1, F, D), lambda e, b: (e, 0, 0)),
        ],
        out_specs=pl.BlockSpec((1, 1, cap, D), lambda e, b: (b, e, 0, 0)),
        out_shape=jax.ShapeDtypeStruct((B, E, cap, D), F32),
        compiler_params=pltpu.CompilerParams(
            dimension_semantics=("parallel", "parallel"), vmem_limit_bytes=VMEM_LIMIT),
        name="expert_ffn",
    )(xg, g, w1, w3, w2)


def _rms_norm(x, g):
    xf = x.astype(F32)
    y = xf * lax.rsqrt(jnp.mean(xf * xf, axis=-1, keepdims=True) + EPS)
    return (y * g.astype(F32)).astype(x.dtype)


def _modulate(x, g, shift, scale):
    return _rms_norm(x, g) * (1 + scale) + shift


def _heads(t, n):
    B, L, W = t.shape
    return t.reshape(B, L, n, W // n).transpose(0, 2, 1, 3)


def _merge_heads(t):
    B, H, L, d = t.shape
    return t.transpose(0, 2, 1, 3).reshape(B, L, H * d)


def _short_conv(u, w):
    up = jnp.pad(u, ((0, 0), (1, 1), (0, 0)))
    return up[:, :-2] * w[0] + up[:, 1:-1] * w[1] + up[:, 2:] * w[2]


def _chunk_recurrence(q, k, v, log_g, s0):
    B, H, L, dk = q.shape
    dv = v.shape[-1]
    n = L // CHUNK

    def to_chunks(t):
        return t.astype(F32).reshape(B, H, n, CHUNK, t.shape[-1]).transpose(2, 0, 1, 3, 4)

    mask = jnp.tril(jnp.ones((CHUNK, CHUNK), bool))[:, :, None]

    def step(S, xs):
        qc, kc, vc, gc = xs
        b = jnp.cumsum(gc, axis=-2)
        o_inter = jnp.einsum('bhtd,bhde->bhte', qc * jnp.exp(b), S)
        diff = b[:, :, :, None, :] - b[:, :, None, :, :]
        decay = jnp.where(mask, jnp.exp(jnp.where(mask, diff, 0.0)), 0.0)
        scores = jnp.einsum('bhtd,bhsd,bhtsd->bhts', qc, kc, decay)
        o_intra = jnp.einsum('bhts,bhse->bhte', scores, vc)
        b_end = b[:, :, -1:, :]
        S_new = jnp.exp(b_end[:, :, 0, :, None]) * S + jnp.einsum('bhsd,bhse->bhde', kc * jnp.exp(b_end - b), vc)
        return S_new, o_inter + o_intra

    S_fin, o = lax.scan(step, s0, (to_chunks(q), to_chunks(k), to_chunks(v), to_chunks(log_g)))
    o = o.transpose(1, 2, 0, 3, 4).reshape(B, H, L, dv)
    return o.astype(v.dtype), S_fin


def _bidir_recurrence(ctx_in, lat_in):
    qc, kfc, kbc, vc, gfc, gbc = ctx_in
    ql, kfl, kbl, vl, gfl, gbl = lat_in
    B, H, _, dk = qc.shape
    s0 = jnp.zeros((B, H, dk, vc.shape[-1]), F32)
    flip = lambda t: jnp.flip(t, axis=2)
    oc_f, sc_f = _chunk_recurrence(qc, kfc, vc, gfc, s0)
    ol_f, _ = _chunk_recurrence(ql, kfl, vl, gfl, sc_f)
    oc_b, sc_b = _chunk_recurrence(flip(qc), flip(kbc), flip(vc), flip(gbc), s0)
    ol_b, _ = _chunk_recurrence(flip(ql), flip(kbl), flip(vl), flip(gbl), sc_b)
    return oc_f + flip(oc_b), ol_f + flip(ol_b)


def _gla_inputs(parts, wa2, ba):
    q = _heads(parts['gla_q'], GLA_HEADS) * (GLA_DK ** -0.5)
    k = _heads(parts['gla_k'], GLA_HEADS)
    v = _heads(parts['gla_v'], GLA_HEADS)
    gf = _heads(jax.nn.log_sigmoid((parts['gla_af'] @ wa2[0] + ba[0]).astype(F32)) / GLA_TAU, GLA_HEADS)
    gb = _heads(jax.nn.log_sigmoid((parts['gla_ab'] @ wa2[1] + ba[1]).astype(F32)) / GLA_TAU, GLA_HEADS)
    return (q, k, k, v, gf, gb)


def _hgrn_inputs(parts, lb):
    q = _heads(jax.nn.silu(parts['hg_q']), HG_HEADS)
    v = _heads(parts['hg_i'], HG_HEADS)
    res = []
    for d, name in enumerate(('hg_ff', 'hg_fb')):
        z = parts[name].astype(F32)
        f = lb[d] + (1 - lb[d]) * jax.nn.sigmoid(z)
        log_f = jnp.log(jnp.maximum(f, F_TINY))
        k = (1 - lb[d]) * jax.nn.sigmoid(-z)
        res.append((_heads(k, HG_HEADS), _heads(log_f, HG_HEADS)))
    (kf, gf), (kb, gb) = res
    return (q, kf, kb, v, gf, gb)


def _gated_branch(o, g, gate):
    return _merge_heads(_rms_norm(o, g)) * jax.nn.silu(gate)


def _hyena_filters(L, p):
    j = jnp.arange(L, dtype=F32)
    t = j / max(L - 1, 1)
    w = 2 * math.pi * j / L
    f = jnp.linspace(1e-4, HY_BANDS - 1, HY_BANDS, dtype=F32)
    feats = jnp.concatenate([t[:, None], jnp.cos(w[:, None] * f), -jnp.sin(w[:, None] * f)], axis=-1)
    h = jnp.sin(p['hy_freq'][0] * (feats @ p['hy_w1'] + p['hy_b1']))
    h = jnp.sin(p['hy_freq'][1] * (h @ p['hy_w2'] + p['hy_b2']))
    h = (h @ p['hy_w3']).astype(F32)
    dist = jnp.abs(j - L // 2) / (L // 2)
    h = h * (jnp.exp(-dist[:, None] * jnp.abs(p['hy_decay'].astype(F32))) + HY_SHIFT)
    h = h / jnp.sum(jnp.abs(h), axis=0, keepdims=True)
    return h.reshape(L, HY_ORDER, HY_WIDTH)


def _fft_conv(u, h):
    L = u.shape[1]
    n = 2 * L
    U = jnp.fft.rfft(u.astype(F32), n=n, axis=1)
    Hf = jnp.fft.rfft(h, n=n, axis=0)
    y = jnp.fft.irfft(U * Hf[None], n=n, axis=1)[:, L // 2: L // 2 + L]
    return y.astype(u.dtype)


def _hyena_branch(u3, p):
    L = u3.shape[1]
    v, x1, x2 = jnp.split(_short_conv(u3, p['hy_conv_w']), 1 + HY_ORDER, axis=-1)
    h = _hyena_filters(L, p)
    z = v
    for o, gate in enumerate((x1, x2)):
        z = gate * (_fft_conv(z, h[:, o]) + p['hy_bias'][o] * z)
    return z


def _rope_axis(t, pos):
    half = t.shape[-1] // 2
    freqs = ROPE_BASE ** (-jnp.arange(half, dtype=F32) / half)
    ang = pos.astype(F32)[:, None] * freqs
    cos, sin = jnp.cos(ang).astype(t.dtype), jnp.sin(ang).astype(t.dtype)
    t1, t2 = t[..., :half], t[..., half:]
    return jnp.concatenate([t1 * cos - t2 * sin, t1 * sin + t2 * cos], axis=-1)


def _rope2d(t, row, col):
    a = t.shape[-1] // 2
    return jnp.concatenate([_rope_axis(t[..., :a], row), _rope_axis(t[..., a:], col)], axis=-1)


def _da_qk(t, g):
    B, L, _ = t.shape
    t = _rms_norm(t.reshape(B, L, DA_HEADS, 2, DA_DK), g).transpose(0, 2, 1, 3, 4)
    return t[..., 0, :], t[..., 1, :]


def _diff_attend(q1, q2, k1, k2, v, lam):
    B, H, Lq, d = q1.shape
    nb = Lq // Q_BLOCK
    scale = d ** -0.5

    def blocks(t):
        return t.reshape(B, H, nb, Q_BLOCK, d).transpose(2, 0, 1, 3, 4)

    def one(qs):
        qa, qb = qs
        p1 = jax.nn.softmax(jnp.einsum('bhqd,bhkd->bhqk', qa, k1).astype(F32) * scale, axis=-1)
        p2 = jax.nn.softmax(jnp.einsum('bhqd,bhkd->bhqk', qb, k2).astype(F32) * scale, axis=-1)
        return jnp.einsum('bhqk,bhkd->bhqd', (p1 - lam * p2).astype(v.dtype), v)

    o = lax.map(one, (blocks(q1), blocks(q2)))
    return o.transpose(1, 2, 0, 3, 4).reshape(B, H, Lq, v.shape[-1])


def _split_parts(groups):
    parts = {}
    for (names, _), arr in zip(PROJ_GROUPS, groups):
        off = 0
        for nm in names:
            w = IN_WIDTHS[IN_NAMES.index(nm)]
            parts[nm] = arr[..., off:off + w]
            off += w
    return parts


def _group_weights(w_in):
    cols = []
    for names, width in PROJ_GROUPS:
        used = 0
        for nm in names:
            i = IN_NAMES.index(nm)
            cols.append(w_in[:, IN_OFFSETS[i]:IN_OFFSETS[i] + IN_WIDTHS[i]])
            used += IN_WIDTHS[i]
        if width > used:
            cols.append(jnp.zeros((w_in.shape[0], width - used), w_in.dtype))
    return jnp.concatenate(cols, axis=1).astype(BF16)


def _token_mixer(pc, px, p, lb, row, col, lam_init):
    gla_c, gla_x = _bidir_recurrence(_gla_inputs(pc, p['gla_wa2'], p['gla_ba']),
                                     _gla_inputs(px, p['gla_wa2'], p['gla_ba']))
    hg_c, hg_x = _bidir_recurrence(_hgrn_inputs(pc, lb), _hgrn_inputs(px, lb))
    q1c, q2c = _da_qk(pc['da_q'], p['da_qnorm_g'])
    k1c, k2c = _da_qk(pc['da_k'], p['da_knorm_g'])
    vc = _heads(pc['da_v'], DA_HEADS)
    q1x, q2x = [_rope2d(t, row, col) for t in _da_qk(px['da_q'], p['da_qnorm_g'])]
    k1x, k2x = [_rope2d(t, row, col) for t in _da_qk(px['da_k'], p['da_knorm_g'])]
    vx = _heads(px['da_v'], DA_HEADS)
    lp = p['da_lam'].astype(F32)
    lam = jnp.exp(jnp.sum(lp[0] * lp[1])) - jnp.exp(jnp.sum(lp[2] * lp[3])) + lam_init
    cat = lambda a, b: jnp.concatenate([a, b], axis=2)
    da_x = _diff_attend(q1x, q2x, cat(k1c, k1x), cat(k2c, k2x), cat(vc, vx), lam)
    da_c = _diff_attend(q1c, q2c, k1c, k2c, vc, lam)

    def finish(parts, gla_o, hg_o, da_o):
        return (_gated_branch(gla_o, p['gla_norm_g'], parts['gla_g']),
                _hyena_branch(parts['hy'], p),
                _gated_branch(hg_o, p['hg_norm_g'], parts['hg_g']),
                _merge_heads(_rms_norm(da_o, p['da_norm_g']) * (1 - lam_init)))

    return finish(pc, gla_c, hg_c, da_c), finish(px, gla_x, hg_x, da_x)


def _expert_choice_moe(h, router, w1, w3, w2):
    B, L, D = h.shape
    cap = CAPACITY_FACTOR * L // N_EXPERTS
    aff = jax.nn.softmax((h @ router).astype(F32), axis=-1)
    g, idx = lax.top_k(aff.transpose(0, 2, 1), cap)
    xg = jax.vmap(lambda hb, ib: hb[ib])(h, idx)
    y = _expert_ffn(xg, g[..., None], w1, w3, w2)
    return jax.vmap(lambda yb, ib: jnp.zeros((L, D), h.dtype).at[ib.reshape(-1)].add(yb.reshape(-1, D)))(y, idx)


def kernel(x, c, ctx, c_ctx, ada_w, ada_b, norm1_g, norm2_g, w_in, gla_wa2, gla_ba, gla_norm_g,
           hy_conv_w, hy_w1, hy_b1, hy_w2, hy_b2, hy_w3, hy_freq, hy_decay, hy_bias,
           hg_lower, hg_norm_g, da_qnorm_g, da_knorm_g, da_lam, da_norm_g, w_branch, w_out,
           moe_router, moe_w1, moe_w3, moe_w2):
    B, L, D = x.shape
    Lc = ctx.shape[1]
    rows = L // GRID_W
    row = jnp.repeat(jnp.arange(rows), GRID_W)
    col = jnp.tile(jnp.arange(GRID_W), rows)
    P = jax.nn.softmax(hg_lower.astype(F32), axis=0)
    lower = jnp.cumsum(P, axis=0) - P[0]
    sc = jax.nn.silu(c)
    scc = jax.nn.silu(c_ctx)
    xc, xx = ctx, x
    for l in range(DEPTH):
        last = l == DEPTH - 1
        lam_init = 0.8 - 0.6 * math.exp(-0.3 * l)
        p = {'gla_wa2': gla_wa2[l], 'gla_ba': gla_ba[l], 'gla_norm_g': gla_norm_g[l],
             'hy_conv_w': hy_conv_w[l], 'hy_w1': hy_w1[l], 'hy_b1': hy_b1[l], 'hy_w2': hy_w2[l], 'hy_b2': hy_b2[l],
             'hy_w3': hy_w3[l], 'hy_freq': hy_freq[l], 'hy_decay': hy_decay[l], 'hy_bias': hy_bias[l],
             'hg_norm_g': hg_norm_g[l], 'da_qnorm_g': da_qnorm_g[l], 'da_knorm_g': da_knorm_g[l],
             'da_lam': da_lam[l], 'da_norm_g': da_norm_g[l]}
        mod_x = jnp.split((sc @ ada_w[l] + ada_b[l])[:, None, :], ADA_CHUNKS, axis=-1)
        mod_c1 = jnp.split((scc @ ada_w[l] + ada_b[l])[None, None, :], ADA_CHUNKS, axis=-1)
        mod_c = [jnp.broadcast_to(m, (B, 1, D)) for m in mod_c1]
        wg = _group_weights(w_in[l])
        wb = w_branch[l].astype(BF16)
        wo = w_out[l].astype(BF16)
        w1, w3, w2 = moe_w1[l].astype(BF16), moe_w3[l].astype(BF16), moe_w2[l].astype(BF16)
        gx = _in_proj(xx, norm1_g[l], mod_x[0], mod_x[1], wg, tm=256)
        gc = _in_proj(xc, norm1_g[l], mod_c[0], mod_c[1], wg, tm=256)
        px, pc = _split_parts(gx), _split_parts(gc)
        outs_c, outs_x = _token_mixer(pc, px, p, lower[l], row, col, lam_init)
        xx = _merge_out(outs_x, gx[4], wb, wo, xx, mod_x[2], tm=256)
        hx = _modulate(xx, norm2_g[l], mod_x[3], mod_x[4])
        xx = xx + mod_x[5] * _expert_choice_moe(hx, moe_router[l], w1, w3, w2)
        if not last:
            xc = _merge_out(outs_c, gc[4], wb, wo, xc, mod_c[2], tm=256)
            hc = _modulate(xc, norm2_g[l], mod_c[3], mod_c[4])
            xc = xc + mod_c[5] * _expert_choice_moe(hc, moe_router[l], w1, w3, w2)
    return xx
```

```python
import functools
import math

import jax
import jax.numpy as jnp
import numpy as np
from jax import lax
from jax.experimental import pallas as pl
from jax.experimental.pallas import tpu as pltpu

D_MODEL = 1024
DEPTH = 2
GRID_W = 64
N_BRANCH = 4
MIX_W = 256
GLA_HEADS = 4
GLA_DK = 32
GLA_DV = 64
GLA_RANK = 16
GLA_TAU = 16.0
HY_WIDTH = 256
HY_ORDER = 2
HY_BANDS = 16
HY_SHIFT = 0.05
HG_HEADS = 4
HG_DK = 64
HG_DV = 64
DA_HEADS = 4
DA_DK = 32
DA_DV = 64
ROPE_BASE = 10000.0
N_EXPERTS = 16
EXPERT_FF = 1024
CAPACITY_FACTOR = 2
ADA_CHUNKS = 6
EPS = 1e-6
F_TINY = 1e-20

IN_NAMES = ('gla_q', 'gla_k', 'gla_v', 'gla_af', 'gla_ab', 'gla_g', 'hy',
            'hg_q', 'hg_ff', 'hg_fb', 'hg_i', 'hg_g', 'da_q', 'da_k', 'da_v', 'merge')
IN_WIDTHS = (GLA_HEADS * GLA_DK, GLA_HEADS * GLA_DK, GLA_HEADS * GLA_DV, GLA_RANK, GLA_RANK, GLA_HEADS * GLA_DV,
             (1 + HY_ORDER) * HY_WIDTH,
             HG_HEADS * HG_DK, HG_HEADS * HG_DK, HG_HEADS * HG_DK, HG_HEADS * HG_DV, HG_HEADS * HG_DV,
             DA_HEADS * 2 * DA_DK, DA_HEADS * 2 * DA_DK, DA_HEADS * DA_DV,
             N_BRANCH * D_MODEL)
IN_OFFSETS = tuple(int(v) for v in np.cumsum((0,) + IN_WIDTHS)[:-1])

PROJ_GROUPS = (
    (('gla_q', 'gla_k', 'gla_v', 'gla_g', 'gla_af', 'gla_ab'), 896),
    (('hy',), 768),
    (('hg_q', 'hg_ff', 'hg_fb', 'hg_i', 'hg_g'), 1280),
    (('da_q', 'da_k', 'da_v'), 768),
    (('merge',), 4096),
)
PROJ_WIDTHS = tuple(w for _, w in PROJ_GROUPS)
VMEM_LIMIT = 56 * 1024 * 1024

BF16 = jnp.bfloat16
F32 = jnp.float32
N_HEADS = 4
RC = 64
N_QK = 2 * DA_HEADS


def _in_proj_kernel(x_ref, g_ref, sh_ref, sc_ref, w_ref, *out_refs):
    x = x_ref[0]
    ms = jnp.mean(x * x, axis=-1, keepdims=True)
    h = x * lax.rsqrt(ms + EPS) * g_ref[...]
    h = (h * (1.0 + sc_ref[0]) + sh_ref[0]).astype(BF16)
    off = 0
    for o_ref, w in zip(out_refs, PROJ_WIDTHS):
        o_ref[0] = jnp.dot(h, w_ref[:, off:off + w], preferred_element_type=F32)
        off += w


def _in_proj(x, g, shift, scale, w_groups, tm):
    B, L, D = x.shape
    nw = w_groups.shape[1]
    return pl.pallas_call(
        _in_proj_kernel,
        grid=(B, L // tm),
        in_specs=[
            pl.BlockSpec((1, tm, D), lambda b, i: (b, i, 0)),
            pl.BlockSpec((1, D), lambda b, i: (0, 0)),
            pl.BlockSpec((1, 1, D), lambda b, i: (b, 0, 0)),
            pl.BlockSpec((1, 1, D), lambda b, i: (b, 0, 0)),
            pl.BlockSpec((D, nw), lambda b, i: (0, 0), pipeline_mode=pl.Buffered(1)),
        ],
        out_specs=[pl.BlockSpec((1, tm, w), lambda b, i: (b, i, 0)) for w in PROJ_WIDTHS],
        out_shape=[jax.ShapeDtypeStruct((B, L, w), F32) for w in PROJ_WIDTHS],
        compiler_params=pltpu.CompilerParams(
            dimension_semantics=("parallel", "parallel"), vmem_limit_bytes=VMEM_LIMIT),
        name="in_proj",
    )(x, g.reshape(1, D), shift, scale, w_groups)


def _merge_kernel(o0_ref, o1_ref, o2_ref, o3_ref, gate_ref, wb_ref, wo_ref, x_ref, m_ref, out_ref):
    D = D_MODEL
    acc = None
    for i, o_ref in enumerate((o0_ref, o1_ref, o2_ref, o3_ref)):
        t = jnp.dot(o_ref[0].astype(BF16), wb_ref[i], preferred_element_type=F32)
        t = jax.nn.sigmoid(gate_ref[0, :, i * D:(i + 1) * D]) * t
        acc = t if acc is None else acc + t
    mx = jnp.dot(acc.astype(BF16), wo_ref[...], preferred_element_type=F32)
    out_ref[0] = x_ref[0] + m_ref[0] * mx


def _merge_out(outs, gate_cols, w_branch, w_out, x, m, tm):
    B, L, D = x.shape
    return pl.pallas_call(
        _merge_kernel,
        grid=(B, L // tm),
        in_specs=[pl.BlockSpec((1, tm, MIX_W), lambda b, i: (b, i, 0)) for _ in range(N_BRANCH)] + [
            pl.BlockSpec((1, tm, N_BRANCH * D), lambda b, i: (b, i, 0)),
            pl.BlockSpec((N_BRANCH, MIX_W, D), lambda b, i: (0, 0, 0)),
            pl.BlockSpec((D, D), lambda b, i: (0, 0)),
            pl.BlockSpec((1, tm, D), lambda b, i: (b, i, 0)),
            pl.BlockSpec((1, 1, D), lambda b, i: (b, 0, 0)),
        ],
        out_specs=pl.BlockSpec((1, tm, D), lambda b, i: (b, i, 0)),
        out_shape=jax.ShapeDtypeStruct((B, L, D), F32),
        compiler_params=pltpu.CompilerParams(
            dimension_semantics=("parallel", "parallel"), vmem_limit_bytes=VMEM_LIMIT),
        name="merge_out",
    )(*outs, gate_cols, w_branch, w_out, x, m)


def _expert_ffn_kernel(x_ref, g_ref, w1_ref, w3_ref, w2_ref, out_ref):
    x = x_ref[0, 0].astype(BF16)
    a = jnp.dot(x, w1_ref[0], preferred_element_type=F32)
    b = jnp.dot(x, w3_ref[0], preferred_element_type=F32)
    h = (a * jax.nn.sigmoid(a) * b).astype(BF16)
    y = jnp.dot(h, w2_ref[0], preferred_element_type=F32)
    out_ref[0, 0] = y * g_ref[0, 0]


def _expert_ffn(xg, g, w1, w3, w2):
    B, E, cap, D = xg.shape
    F = w1.shape[-1]
    return pl.pallas_call(
        _expert_ffn_kernel,
        grid=(E, B),
        in_specs=[
            pl.BlockSpec((1, 1, cap, D), lambda e, b: (b, e, 0, 0)),
            pl.BlockSpec((1, 1, cap, 1), lambda e, b: (b, e, 0, 0)),
            pl.BlockSpec((1, D, F), lambda e, b: (e, 0, 0)),
            pl.BlockSpec((1, D, F), lambda e, b: (e, 0, 0)),
            pl.BlockSpec((1, F, D), lambda e, b: (e, 0, 0)),
        ],
        out_specs=pl.BlockSpec((1, 1, cap, D), lambda e, b: (b, e, 0, 0)),
        out_shape=jax.ShapeDtypeStruct((B, E, cap, D), F32),
        compiler_params=pltpu.CompilerParams(
            dimension_semantics=("parallel", "parallel"), vmem_limit_bytes=VMEM_LIMIT),
        name="expert_ffn",
    )(xg, g, w1, w3, w2)


def _recur_masks(C):
    t = np.arange(C)[:, None]
    s = np.tile(np.arange(C), N_HEADS)[None, :]
    ms = []
    n = 2
    while n < C:
        ms.append((t // n == s // n).astype(np.float32))
        n *= 2
    ms.append((t == s).astype(np.float32))
    return jnp.asarray(np.stack(ms))


def _recur_kernel(*refs, rev, C, nsub, dk, dv, has_prev):
    if has_prev:
        q_ref, k_ref, v_ref, g_ref, s0_ref, m_ref, prev_ref, o_ref, sfin_ref, s_scr = refs
    else:
        q_ref, k_ref, v_ref, g_ref, s0_ref, m_ref, o_ref, sfin_ref, s_scr = refs
        prev_ref = None
    HK = N_HEADS * dk
    HV = N_HEADS * dv
    i = pl.program_id(1)

    @pl.when(i == 0)
    def _():
        s_scr[...] = s0_ref[0]

    row = lax.broadcasted_iota(jnp.int32, (C, HK), 0)
    pos = (C - 1 - row) if rev else row
    lane_head_k = lax.broadcasted_iota(jnp.int32, (C, HK), 1) // dk
    lane_head_v = lax.broadcasted_iota(jnp.int32, (C, HV), 1) // dv
    bd = (lax.broadcasted_iota(jnp.int32, (HV, HK), 0) // dv
          == lax.broadcasted_iota(jnp.int32, (HV, HK), 1) // dk)

    def ahead(x, s):
        return pltpu.roll(x, (C - s) if rev else s, axis=0)

    def behind(x, s):
        return pltpu.roll(x, s if rev else (C - s), axis=0)

    def stack_heads(x, lane_head):
        return jnp.concatenate([jnp.where(lane_head == h, x, 0.0) for h in range(N_HEADS)], axis=0).astype(BF16)

    def nt(a, b):
        return lax.dot_general(a, b, (((1,), (1,)), ((), ())), preferred_element_type=F32)

    def chunk(j, carry):
        c = (nsub - 1 - j) if rev else j
        sl = pl.ds(pl.multiple_of(c * C, C), C)
        q = q_ref[0, sl, :]
        k = k_ref[0, sl, :]
        v = v_ref[0, sl, :]
        g = g_ref[0, sl, :]
        b = g
        s = 1
        while s < C:
            b = b + jnp.where(pos >= s, ahead(b, s), 0.0)
            s *= 2
        a = nt(q.astype(BF16), stack_heads(k, lane_head_k)) * m_ref[m_ref.shape[0] - 1]
        m, lvl = 1, 0
        while m < C:
            n = 2 * m
            off = pos % n
            rq = jnp.where(off == m, ahead(b, 1), 0.0)
            rk = jnp.where(off == m - 1, b, 0.0)
            s = 1
            while s < m:
                rq = rq + ahead(rq, s)
                rk = rk + behind(rk, s)
                s *= 2
            late = off >= m
            qt = jnp.where(late, q * jnp.exp(jnp.where(late, b - rq, 0.0)), 0.0)
            kt = jnp.where(late, 0.0, k * jnp.exp(jnp.where(late, 0.0, rk - b)))
            al = nt(qt.astype(BF16), stack_heads(kt, lane_head_k))
            a = a + (al if n == C else al * m_ref[lvl])
            m, lvl = n, lvl + 1
        st = s_scr[...]
        o = jnp.dot(a.astype(BF16), stack_heads(v, lane_head_v), preferred_element_type=F32)
        o = o + nt((q * jnp.exp(b)).astype(BF16), st.astype(BF16))
        if prev_ref is not None:
            o = o + prev_ref[0, sl, :]
        o_ref[0, sl, :] = o
        b_end = b[0:1, :] if rev else b[C - 1:C, :]
        kend = (k * jnp.exp(b_end - b)).astype(BF16)
        upd = lax.dot_general(v.astype(BF16), kend, (((0,), (0,)), ((), ())), preferred_element_type=F32)
        s_scr[...] = st * jnp.exp(b_end) + jnp.where(bd, upd, 0.0)
        return carry

    lax.fori_loop(0, nsub, chunk, 0)

    @pl.when(i == pl.num_programs(1) - 1)
    def _():
        sfin_ref[0] = s_scr[...]


def _recurrence(q, k, v, g, s0, prev, *, rev, dk, dv, tb):
    B, L, HK = q.shape
    HV = v.shape[-1]
    C = RC
    nblk = L // tb
    masks = _recur_masks(C)
    tok = (lambda b, i: (b, nblk - 1 - i, 0)) if rev else (lambda b, i: (b, i, 0))
    in_specs = [
        pl.BlockSpec((1, tb, HK), tok), pl.BlockSpec((1, tb, HK), tok),
        pl.BlockSpec((1, tb, HV), tok), pl.BlockSpec((1, tb, HK), tok),
        pl.BlockSpec((1, HV, HK), lambda b, i: (b, 0, 0)),
        pl.BlockSpec(masks.shape, lambda b, i: (0, 0, 0)),
    ]
    args = [q, k, v, g, s0, masks]
    if prev is not None:
        in_specs.append(pl.BlockSpec((1, tb, HV), tok))
        args.append(prev)
    kern = functools.partial(_recur_kernel, rev=rev, C=C, nsub=tb // C, dk=dk, dv=dv, has_prev=prev is not None)
    return pl.pallas_call(
        kern,
        grid=(B, nblk),
        in_specs=in_specs,
        out_specs=[pl.BlockSpec((1, tb, HV), tok), pl.BlockSpec((1, HV, HK), lambda b, i: (b, 0, 0))],
        out_shape=[jax.ShapeDtypeStruct((B, L, HV), F32), jax.ShapeDtypeStruct((B, HV, HK), F32)],
        scratch_shapes=[pltpu.VMEM((HV, HK), F32)],
        compiler_params=pltpu.CompilerParams(
            dimension_semantics=("parallel", "arbitrary"), vmem_limit_bytes=VMEM_LIMIT),
        name="recur_rev" if rev else "recur_fwd",
    )(*args)


def _bidir(ctx_in, lat_in, *, dk, dv):
    qc, kfc, kbc, vc, gfc, gbc = ctx_in
    ql, kfl, kbl, vl, gfl, gbl = lat_in
    B = qc.shape[0]
    s0 = jnp.zeros((B, vc.shape[-1], qc.shape[-1]), F32)
    kw = dict(dk=dk, dv=dv)
    oc_f, sc_f = _recurrence(qc, kfc, vc, gfc, s0, None, rev=False, tb=qc.shape[1], **kw)
    ol_f, _ = _recurrence(ql, kfl, vl, gfl, sc_f, None, rev=False, tb=256, **kw)
    oc, sc_b = _recurrence(qc, kbc, vc, gbc, s0, oc_f, rev=True, tb=qc.shape[1], **kw)
    ol, _ = _recurrence(ql, kbl, vl, gbl, sc_b, ol_f, rev=True, tb=256, **kw)
    return oc, ol


def _attn_kernel(lam_ref, q_ref, k_ref, vt_ref, g_ref, o_ref, qm_scr, m_scr, l_scr, acc_scr, *, tk):
    tq = q_ref.shape[1]
    W = q_ref.shape[2]
    nkb = k_ref.shape[1] // tk
    q = q_ref[0]
    lane_pair = lax.broadcasted_iota(jnp.int32, (tq, W), 1) // DA_DK
    for j in range(N_QK):
        qm_scr[j] = jnp.where(lane_pair == j, q, 0.0).astype(BF16)
    m_scr[...] = jnp.full(m_scr.shape, -jnp.inf, F32)
    l_scr[...] = jnp.zeros(l_scr.shape, F32)
    acc_scr[...] = jnp.zeros(acc_scr.shape, F32)

    def body(kb, carry):
        ks = pl.ds(pl.multiple_of(kb * tk, tk), tk)
        kblk = k_ref[0, ks, :]
        for h in range(DA_HEADS):
            vth = vt_ref[0, h * DA_DV:(h + 1) * DA_DV, ks]
            for w in range(2):
                j = 2 * h + w
                s = lax.dot_general(kblk, qm_scr[j], (((1,), (1,)), ((), ())), preferred_element_type=F32)
                m_old = m_scr[j]
                m_new = jnp.maximum(m_old, jnp.max(s, axis=0, keepdims=True))
                alpha = jnp.exp(m_old - m_new)
                p = jnp.exp(s - m_new)
                l_scr[j] = alpha * l_scr[j] + jnp.sum(p, axis=0, keepdims=True)
                acc_scr[j] = alpha * acc_scr[j] + jnp.dot(vth, p.astype(BF16), preferred_element_type=F32)
                m_scr[j] = m_new
        return carry

    lax.fori_loop(0, nkb, body, 0)
    lam = lam_ref[0]
    outs = []
    for h in range(DA_HEADS):
        o = acc_scr[2 * h] / l_scr[2 * h] - lam * (acc_scr[2 * h + 1] / l_scr[2 * h + 1])
        ms = jnp.mean(o * o, axis=0, keepdims=True)
        outs.append(o * lax.rsqrt(ms + EPS) * g_ref[...])
    o_ref[0] = jnp.concatenate(outs, axis=0).T


def _diff_attention(q, k, vt, lam, gcol, *, tq, tk):
    B, Lq, W = q.shape
    Lk = k.shape[1]
    HV = vt.shape[1]
    return pl.pallas_call(
        functools.partial(_attn_kernel, tk=tk),
        grid=(B, Lq // tq),
        in_specs=[
            pl.BlockSpec(memory_space=pltpu.SMEM),
            pl.BlockSpec((1, tq, W), lambda b, i: (b, i, 0)),
            pl.BlockSpec((1, Lk, W), lambda b, i: (b, 0, 0)),
            pl.BlockSpec((1, HV, Lk), lambda b, i: (b, 0, 0)),
            pl.BlockSpec((DA_DV, 1), lambda b, i: (0, 0)),
        ],
        out_specs=pl.BlockSpec((1, tq, HV), lambda b, i: (b, i, 0)),
        out_shape=jax.ShapeDtypeStruct((B, Lq, HV), F32),
        scratch_shapes=[
            pltpu.VMEM((N_QK, tq, W), BF16),
            pltpu.VMEM((N_QK, 1, tq), F32),
            pltpu.VMEM((N_QK, 1, tq), F32),
            pltpu.VMEM((N_QK, DA_DV, tq), F32),
        ],
        compiler_params=pltpu.CompilerParams(
            dimension_semantics=("parallel", "parallel"), vmem_limit_bytes=VMEM_LIMIT),
        name="diff_attn",
    )(lam, q, k, vt, gcol)


def _rms_norm(x, g):
    xf = x.astype(F32)
    y = xf * lax.rsqrt(jnp.mean(xf * xf, axis=-1, keepdims=True) + EPS)
    return (y * g.astype(F32)).astype(x.dtype)


def _modulate(x, g, shift, scale):
    return _rms_norm(x, g) * (1 + scale) + shift


def _short_conv(u, w):
    up = jnp.pad(u, ((0, 0), (1, 1), (0, 0)))
    return up[:, :-2] * w[0] + up[:, 1:-1] * w[1] + up[:, 2:] * w[2]


def _gla_inputs(parts, wa2, ba):
    q = parts['gla_q'] * (GLA_DK ** -0.5)
    k = parts['gla_k']
    gf = jax.nn.log_sigmoid((parts['gla_af'] @ wa2[0] + ba[0]).astype(F32)) / GLA_TAU
    gb = jax.nn.log_sigmoid((parts['gla_ab'] @ wa2[1] + ba[1]).astype(F32)) / GLA_TAU
    return (q, k, k, parts['gla_v'], gf, gb)


def _hgrn_inputs(parts, lb):
    q = jax.nn.silu(parts['hg_q'])
    res = []
    for d, name in enumerate(('hg_ff', 'hg_fb')):
        z = parts[name].astype(F32)
        f = lb[d] + (1 - lb[d]) * jax.nn.sigmoid(z)
        res.append(((1 - lb[d]) * jax.nn.sigmoid(-z), jnp.log(jnp.maximum(f, F_TINY))))
    (kf, gf), (kb, gb) = res
    return (q, kf, kb, parts['hg_i'], gf, gb)


def _gated_branch(o, g, gate):
    B, L, W = o.shape
    y = _rms_norm(o.reshape(B, L, N_HEADS, W // N_HEADS), g).reshape(B, L, W)
    return y * jax.nn.silu(gate)


def _hyena_filters(L, p):
    j = jnp.arange(L, dtype=F32)
    t = j / max(L - 1, 1)
    w = 2 * math.pi * j / L
    f = jnp.linspace(1e-4, HY_BANDS - 1, HY_BANDS, dtype=F32)
    feats = jnp.concatenate([t[:, None], jnp.cos(w[:, None] * f), -jnp.sin(w[:, None] * f)], axis=-1)
    h = jnp.sin(p['hy_freq'][0] * (feats @ p['hy_w1'] + p['hy_b1']))
    h = jnp.sin(p['hy_freq'][1] * (h @ p['hy_w2'] + p['hy_b2']))
    h = (h @ p['hy_w3']).astype(F32)
    dist = jnp.abs(j - L // 2) / (L // 2)
    h = h * (jnp.exp(-dist[:, None] * jnp.abs(p['hy_decay'].astype(F32))) + HY_SHIFT)
    h = h / jnp.sum(jnp.abs(h), axis=0, keepdims=True)
    return h.reshape(L, HY_ORDER, HY_WIDTH)


def _fft_conv(u, h):
    L = u.shape[1]
    n = 2 * L
    U = jnp.fft.rfft(u.astype(F32), n=n, axis=1)
    Hf = jnp.fft.rfft(h, n=n, axis=0)
    y = jnp.fft.irfft(U * Hf[None], n=n, axis=1)[:, L // 2: L // 2 + L]
    return y.astype(u.dtype)


def _hyena_branch(u3, p):
    L = u3.shape[1]
    v, x1, x2 = jnp.split(_short_conv(u3, p['hy_conv_w']), 1 + HY_ORDER, axis=-1)
    h = _hyena_filters(L, p)
    z = v
    for o, gate in enumerate((x1, x2)):
        z = gate * (_fft_conv(z, h[:, o]) + p['hy_bias'][o] * z)
    return z


def _rope_tables(L):
    quarter = DA_DK // 4
    freqs = ROPE_BASE ** (-jnp.arange(quarter, dtype=F32) / quarter)
    row = (jnp.arange(L) // GRID_W).astype(F32)[:, None] * freqs
    col = (jnp.arange(L) % GRID_W).astype(F32)[:, None] * freqs
    return jnp.cos(row), jnp.sin(row), jnp.cos(col), jnp.sin(col)


def _da_prep(t, g, tables):
    B, L, W = t.shape
    t = _rms_norm(t.reshape(B, L, N_QK, DA_DK), g)
    if tables is not None:
        cr, sr, cc, sc = [a[None, :, None, :] for a in tables]
        e = DA_DK // 4
        a1, a2, b1, b2 = t[..., :e], t[..., e:2 * e], t[..., 2 * e:3 * e], t[..., 3 * e:]
        t = jnp.concatenate([a1 * cr - a2 * sr, a1 * sr + a2 * cr, b1 * cc - b2 * sc, b1 * sc + b2 * cc], axis=-1)
    return t.reshape(B, L, W)


def _split_parts(groups):
    parts = {}
    for (names, _), arr in zip(PROJ_GROUPS, groups):
        off = 0
        for nm in names:
            w = IN_WIDTHS[IN_NAMES.index(nm)]
            parts[nm] = arr[..., off:off + w]
            off += w
    return parts


def _group_weights(w_in):
    cols = []
    for names, width in PROJ_GROUPS:
        used = 0
        for nm in names:
            i = IN_NAMES.index(nm)
            cols.append(w_in[:, IN_OFFSETS[i]:IN_OFFSETS[i] + IN_WIDTHS[i]])
            used += IN_WIDTHS[i]
        if width > used:
            cols.append(jnp.zeros((w_in.shape[0], width - used), w_in.dtype))
    return jnp.concatenate(cols, axis=1).astype(BF16)


def _token_mixer(pc, px, p, lb, lam_init, last):
    gla_c, gla_x = _bidir(_gla_inputs(pc, p['gla_wa2'], p['gla_ba']),
                          _gla_inputs(px, p['gla_wa2'], p['gla_ba']), dk=GLA_DK, dv=GLA_DV)
    hg_c, hg_x = _bidir(_hgrn_inputs(pc, lb), _hgrn_inputs(px, lb), dk=HG_DK, dv=HG_DV)
    tables = _rope_tables(px['da_q'].shape[1])
    scale = DA_DK ** -0.5
    qx = _da_prep(px['da_q'], p['da_qnorm_g'], tables) * scale
    kx = _da_prep(px['da_k'], p['da_knorm_g'], tables)
    kc = _da_prep(pc['da_k'], p['da_knorm_g'], None)
    k_all = jnp.concatenate([kc, kx], axis=1).astype(BF16)
    vt_all = jnp.concatenate([pc['da_v'], px['da_v']], axis=1).transpose(0, 2, 1).astype(BF16)
    lp = p['da_lam'].astype(F32)
    lam = (jnp.exp(jnp.sum(lp[0] * lp[1])) - jnp.exp(jnp.sum(lp[2] * lp[3])) + lam_init).reshape(1)
    gcol = (p['da_norm_g'].astype(F32) * (1 - lam_init)).reshape(DA_DV, 1)
    da_x = _diff_attention(qx, k_all, vt_all, lam, gcol, tq=256, tk=256)

    def finish(parts, gla_o, hg_o, da_o):
        return (_gated_branch(gla_o, p['gla_norm_g'], parts['gla_g']),
                _hyena_branch(parts['hy'], p),
                _gated_branch(hg_o, p['hg_norm_g'], parts['hg_g']),
                da_o)

    outs_x = finish(px, gla_x, hg_x, da_x)
    if last:
        return None, outs_x
    Lc = kc.shape[1]
    qc = _da_prep(pc['da_q'], p['da_qnorm_g'], None) * scale
    da_c = _diff_attention(qc, k_all[:, :Lc], vt_all[:, :, :Lc], lam, gcol, tq=Lc, tk=Lc)
    return finish(pc, gla_c, hg_c, da_c), outs_x


def _expert_choice_moe(h, router, w1, w3, w2):
    B, L, D = h.shape
    cap = CAPACITY_FACTOR * L // N_EXPERTS
    aff = jax.nn.softmax((h @ router).astype(F32), axis=-1)
    g, idx = lax.top_k(aff.transpose(0, 2, 1), cap)
    xg = jax.vmap(lambda hb, ib: hb[ib])(h, idx)
    y = _expert_ffn(xg, g[..., None], w1, w3, w2)
    return jax.vmap(lambda yb, ib: jnp.zeros((L, D), h.dtype).at[ib.reshape(-1)].add(yb.reshape(-1, D)))(y, idx)


def kernel(x, c, ctx, c_ctx, ada_w, ada_b, norm1_g, norm2_g, w_in, gla_wa2, gla_ba, gla_norm_g,
           hy_conv_w, hy_w1, hy_b1, hy_w2, hy_b2, hy_w3, hy_freq, hy_decay, hy_bias,
           hg_lower, hg_norm_g, da_qnorm_g, da_knorm_g, da_lam, da_norm_g, w_branch, w_out,
           moe_router, moe_w1, moe_w3, moe_w2):
    B, L, D = x.shape
    P = jax.nn.softmax(hg_lower.astype(F32), axis=0)
    lower = jnp.cumsum(P, axis=0) - P[0]
    sc = jax.nn.silu(c)
    scc = jax.nn.silu(c_ctx)
    xc, xx = ctx, x
    for l in range(DEPTH):
        last = l == DEPTH - 1
        lam_init = 0.8 - 0.6 * math.exp(-0.3 * l)
        p = {'gla_wa2': gla_wa2[l], 'gla_ba': gla_ba[l], 'gla_norm_g': gla_norm_g[l],
             'hy_conv_w': hy_conv_w[l], 'hy_w1': hy_w1[l], 'hy_b1': hy_b1[l], 'hy_w2': hy_w2[l], 'hy_b2': hy_b2[l],
             'hy_w3': hy_w3[l], 'hy_freq': hy_freq[l], 'hy_decay': hy_decay[l], 'hy_bias': hy_bias[l],
             'hg_norm_g': hg_norm_g[l], 'da_qnorm_g': da_qnorm_g[l], 'da_knorm_g': da_knorm_g[l],
             'da_lam': da_lam[l], 'da_norm_g': da_norm_g[l]}
        mod_x = jnp.split((sc @ ada_w[l] + ada_b[l])[:, None, :], ADA_CHUNKS, axis=-1)
        mod_c1 = jnp.split((scc @ ada_w[l] + ada_b[l])[None, None, :], ADA_CHUNKS, axis=-1)
        mod_c = [jnp.broadcast_to(m, (B, 1, D)) for m in mod_c1]
        wg = _group_weights(w_in[l])
        wb = w_branch[l].astype(BF16)
        wo = w_out[l].astype(BF16)
        w1, w3, w2 = moe_w1[l].astype(BF16), moe_w3[l].astype(BF16), moe_w2[l].astype(BF16)
        gx = _in_proj(xx, norm1_g[l], mod_x[0], mod_x[1], wg, tm=256)
        gc = _in_proj(xc, norm1_g[l], mod_c[0], mod_c[1], wg, tm=256)
        px, pc = _split_parts(gx), _split_parts(gc)
        outs_c, outs_x = _token_mixer(pc, px, p, lower[l], lam_init, last)
        xx = _merge_out(outs_x, gx[4], wb, wo, xx, mod_x[2], tm=256)
        hx = _modulate(xx, norm2_g[l], mod_x[3], mod_x[4])
        xx = xx + mod_x[5] * _expert_choice_moe(hx, moe_router[l], w1, w3, w2)
        if not last:
            xc = _merge_out(outs_c, gc[4], wb, wo, xc, mod_c[2], tm=256)
            hc = _modulate(xc, norm2_g[l], mod_c[3], mod_c[4])
            xc = xc + mod_c[5] * _expert_choice_moe(hc, moe_router[l], w1, w3, w2)
    return xx
```

```python
import functools
import math

import jax
import jax.numpy as jnp
import numpy as np
from jax import lax
from jax.experimental import pallas as pl
from jax.experimental.pallas import tpu as pltpu

D_MODEL = 1024
DEPTH = 2
GRID_W = 64
N_BRANCH = 4
MIX_W = 256
GLA_HEADS = 4
GLA_DK = 32
GLA_DV = 64
GLA_RANK = 16
GLA_TAU = 16.0
HY_WIDTH = 256
HY_ORDER = 2
HY_BANDS = 16
HY_SHIFT = 0.05
HG_HEADS = 4
HG_DK = 64
HG_DV = 64
DA_HEADS = 4
DA_DK = 32
DA_DV = 64
ROPE_BASE = 10000.0
N_EXPERTS = 16
EXPERT_FF = 1024
CAPACITY_FACTOR = 2
ADA_CHUNKS = 6
EPS = 1e-6
F_TINY = 1e-20

IN_NAMES = ('gla_q', 'gla_k', 'gla_v', 'gla_af', 'gla_ab', 'gla_g', 'hy',
            'hg_q', 'hg_ff', 'hg_fb', 'hg_i', 'hg_g', 'da_q', 'da_k', 'da_v', 'merge')
IN_WIDTHS = (GLA_HEADS * GLA_DK, GLA_HEADS * GLA_DK, GLA_HEADS * GLA_DV, GLA_RANK, GLA_RANK, GLA_HEADS * GLA_DV,
             (1 + HY_ORDER) * HY_WIDTH,
             HG_HEADS * HG_DK, HG_HEADS * HG_DK, HG_HEADS * HG_DK, HG_HEADS * HG_DV, HG_HEADS * HG_DV,
             DA_HEADS * 2 * DA_DK, DA_HEADS * 2 * DA_DK, DA_HEADS * DA_DV,
             N_BRANCH * D_MODEL)
IN_OFFSETS = tuple(int(v) for v in np.cumsum((0,) + IN_WIDTHS)[:-1])

PROJ_GROUPS = (
    (('gla_q', 'gla_k', 'gla_v', 'gla_g', 'gla_af', 'gla_ab'), 896),
    (('hy',), 768),
    (('hg_q', 'hg_ff', 'hg_fb', 'hg_i', 'hg_g'), 1280),
    (('da_q', 'da_k', 'da_v'), 768),
    (('merge',), 4096),
)
PROJ_WIDTHS = tuple(w for _, w in PROJ_GROUPS)
VMEM_LIMIT = 56 * 1024 * 1024

BF16 = jnp.bfloat16
F32 = jnp.float32
N_HEADS = 4
RC = 64
N_QK = 2 * DA_HEADS
HB = 256


def _in_proj_kernel(x_ref, g_ref, sh_ref, sc_ref, w_ref, *out_refs):
    x = x_ref[0]
    ms = jnp.mean(x * x, axis=-1, keepdims=True)
    h = x * lax.rsqrt(ms + EPS) * g_ref[...]
    h = (h * (1.0 + sc_ref[0]) + sh_ref[0]).astype(BF16)
    off = 0
    for o_ref, w in zip(out_refs, PROJ_WIDTHS):
        o_ref[0] = jnp.dot(h, w_ref[:, off:off + w], preferred_element_type=F32)
        off += w


def _in_proj(x, g, shift, scale, w_groups, tm):
    B, L, D = x.shape
    nw = w_groups.shape[1]
    return pl.pallas_call(
        _in_proj_kernel,
        grid=(B, L // tm),
        in_specs=[
            pl.BlockSpec((1, tm, D), lambda b, i: (b, i, 0)),
            pl.BlockSpec((1, D), lambda b, i: (0, 0)),
            pl.BlockSpec((1, 1, D), lambda b, i: (b, 0, 0)),
            pl.BlockSpec((1, 1, D), lambda b, i: (b, 0, 0)),
            pl.BlockSpec((D, nw), lambda b, i: (0, 0), pipeline_mode=pl.Buffered(1)),
        ],
        out_specs=[pl.BlockSpec((1, tm, w), lambda b, i: (b, i, 0)) for w in PROJ_WIDTHS],
        out_shape=[jax.ShapeDtypeStruct((B, L, w), F32) for w in PROJ_WIDTHS],
        compiler_params=pltpu.CompilerParams(
            dimension_semantics=("parallel", "parallel"), vmem_limit_bytes=VMEM_LIMIT),
        name="in_proj",
    )(x, g.reshape(1, D), shift, scale, w_groups)


def _merge_kernel(o0_ref, o1_ref, o2_ref, o3_ref, gate_ref, wb_ref, wo_ref, x_ref, m_ref, out_ref):
    D = D_MODEL
    acc = None
    for i, o_ref in enumerate((o0_ref, o1_ref, o2_ref, o3_ref)):
        t = jnp.dot(o_ref[0].astype(BF16), wb_ref[i], preferred_element_type=F32)
        t = jax.nn.sigmoid(gate_ref[0, :, i * D:(i + 1) * D]) * t
        acc = t if acc is None else acc + t
    mx = jnp.dot(acc.astype(BF16), wo_ref[...], preferred_element_type=F32)
    out_ref[0] = x_ref[0] + m_ref[0] * mx


def _merge_out(outs, gate_cols, w_branch, w_out, x, m, tm):
    B, L, D = x.shape
    return pl.pallas_call(
        _merge_kernel,
        grid=(B, L // tm),
        in_specs=[pl.BlockSpec((1, tm, MIX_W), lambda b, i: (b, i, 0)) for _ in range(N_BRANCH)] + [
            pl.BlockSpec((1, tm, N_BRANCH * D), lambda b, i: (b, i, 0)),
            pl.BlockSpec((N_BRANCH, MIX_W, D), lambda b, i: (0, 0, 0)),
            pl.BlockSpec((D, D), lambda b, i: (0, 0)),
            pl.BlockSpec((1, tm, D), lambda b, i: (b, i, 0)),
            pl.BlockSpec((1, 1, D), lambda b, i: (b, 0, 0)),
        ],
        out_specs=pl.BlockSpec((1, tm, D), lambda b, i: (b, i, 0)),
        out_shape=jax.ShapeDtypeStruct((B, L, D), F32),
        compiler_params=pltpu.CompilerParams(
            dimension_semantics=("parallel", "parallel"), vmem_limit_bytes=VMEM_LIMIT),
        name="merge_out",
    )(*outs, gate_cols, w_branch, w_out, x, m)


def _expert_ffn_kernel(x_ref, g_ref, w1_ref, w3_ref, w2_ref, out_ref):
    x = x_ref[0, 0].astype(BF16)
    a = jnp.dot(x, w1_ref[0], preferred_element_type=F32)
    b = jnp.dot(x, w3_ref[0], preferred_element_type=F32)
    h = (a * jax.nn.sigmoid(a) * b).astype(BF16)
    y = jnp.dot(h, w2_ref[0], preferred_element_type=F32)
    out_ref[0, 0] = y * g_ref[0, 0]


def _expert_ffn(xg, g, w1, w3, w2):
    B, E, cap, D = xg.shape
    F = w1.shape[-1]
    return pl.pallas_call(
        _expert_ffn_kernel,
        grid=(E, B),
        in_specs=[
            pl.BlockSpec((1, 1, cap, D), lambda e, b: (b, e, 0, 0)),
            pl.BlockSpec((1, 1, cap, 1), lambda e, b: (b, e, 0, 0)),
            pl.BlockSpec((1, D, F), lambda e, b: (e, 0, 0)),
            pl.BlockSpec((1, D, F), lambda e, b: (e, 0, 0)),
            pl.BlockSpec((1, F, D), lambda e, b: (e, 0, 0)),
        ],
        out_specs=pl.BlockSpec((1, 1, cap, D), lambda e, b: (b, e, 0, 0)),
        out_shape=jax.ShapeDtypeStruct((B, E, cap, D), F32),
        compiler_params=pltpu.CompilerParams(
            dimension_semantics=("parallel", "parallel"), vmem_limit_bytes=VMEM_LIMIT),
        name="expert_ffn",
    )(xg, g, w1, w3, w2)


def _recur_masks(C):
    t = np.arange(C)[:, None]
    s = np.tile(np.arange(C), N_HEADS)[None, :]
    ms = []
    n = 2
    while n < C:
        ms.append((t // n == s // n).astype(np.float32))
        n *= 2
    ms.append((t == s).astype(np.float32))
    return jnp.asarray(np.stack(ms))


def _recur_kernel(*refs, rev, C, nsub, dk, dv, has_prev):
    if has_prev:
        q_ref, k_ref, v_ref, g_ref, s0_ref, m_ref, prev_ref, o_ref, sfin_ref, s_scr = refs
    else:
        q_ref, k_ref, v_ref, g_ref, s0_ref, m_ref, o_ref, sfin_ref, s_scr = refs
        prev_ref = None
    HK = N_HEADS * dk
    HV = N_HEADS * dv
    i = pl.program_id(1)

    @pl.when(i == 0)
    def _():
        s_scr[...] = s0_ref[0]

    row = lax.broadcasted_iota(jnp.int32, (C, HK), 0)
    pos = (C - 1 - row) if rev else row
    lane_head_k = lax.broadcasted_iota(jnp.int32, (C, HK), 1) // dk
    lane_head_v = lax.broadcasted_iota(jnp.int32, (C, HV), 1) // dv
    bd = (lax.broadcasted_iota(jnp.int32, (HV, HK), 0) // dv
          == lax.broadcasted_iota(jnp.int32, (HV, HK), 1) // dk)

    def ahead(x, s):
        return pltpu.roll(x, (C - s) if rev else s, axis=0)

    def behind(x, s):
        return pltpu.roll(x, s if rev else (C - s), axis=0)

    def stack_heads(x, lane_head):
        return jnp.concatenate([jnp.where(lane_head == h, x, 0.0) for h in range(N_HEADS)], axis=0).astype(BF16)

    def nt(a, b):
        return lax.dot_general(a, b, (((1,), (1,)), ((), ())), preferred_element_type=F32)

    def chunk(j, carry):
        c = (nsub - 1 - j) if rev else j
        sl = pl.ds(pl.multiple_of(c * C, C), C)
        q = q_ref[0, sl, :]
        k = k_ref[0, sl, :]
        v = v_ref[0, sl, :]
        g = g_ref[0, sl, :]
        b = g
        s = 1
        while s < C:
            b = b + jnp.where(pos >= s, ahead(b, s), 0.0)
            s *= 2
        a = nt(q.astype(BF16), stack_heads(k, lane_head_k)) * m_ref[m_ref.shape[0] - 1]
        m, lvl = 1, 0
        while m < C:
            n = 2 * m
            off = pos % n
            rq = jnp.where(off == m, ahead(b, 1), 0.0)
            rk = jnp.where(off == m - 1, b, 0.0)
            s = 1
            while s < m:
                rq = rq + ahead(rq, s)
                rk = rk + behind(rk, s)
                s *= 2
            late = off >= m
            qt = jnp.where(late, q * jnp.exp(jnp.where(late, b - rq, 0.0)), 0.0)
            kt = jnp.where(late, 0.0, k * jnp.exp(jnp.where(late, 0.0, rk - b)))
            al = nt(qt.astype(BF16), stack_heads(kt, lane_head_k))
            a = a + (al if n == C else al * m_ref[lvl])
            m, lvl = n, lvl + 1
        st = s_scr[...]
        o = jnp.dot(a.astype(BF16), stack_heads(v, lane_head_v), preferred_element_type=F32)
        o = o + nt((q * jnp.exp(b)).astype(BF16), st.astype(BF16))
        if prev_ref is not None:
            o = o + prev_ref[0, sl, :]
        o_ref[0, sl, :] = o
        b_end = b[0:1, :] if rev else b[C - 1:C, :]
        kend = (k * jnp.exp(b_end - b)).astype(BF16)
        upd = lax.dot_general(v.astype(BF16), kend, (((0,), (0,)), ((), ())), preferred_element_type=F32)
        s_scr[...] = st * jnp.exp(b_end) + jnp.where(bd, upd, 0.0)
        return carry

    lax.fori_loop(0, nsub, chunk, 0)

    @pl.when(i == pl.num_programs(1) - 1)
    def _():
        sfin_ref[0] = s_scr[...]


def _recurrence(q, k, v, g, s0, prev, *, rev, dk, dv, tb):
    B, L, HK = q.shape
    HV = v.shape[-1]
    C = RC
    nblk = L // tb
    masks = _recur_masks(C)
    tok = (lambda b, i: (b, nblk - 1 - i, 0)) if rev else (lambda b, i: (b, i, 0))
    in_specs = [
        pl.BlockSpec((1, tb, HK), tok), pl.BlockSpec((1, tb, HK), tok),
        pl.BlockSpec((1, tb, HV), tok), pl.BlockSpec((1, tb, HK), tok),
        pl.BlockSpec((1, HV, HK), lambda b, i: (b, 0, 0)),
        pl.BlockSpec(masks.shape, lambda b, i: (0, 0, 0)),
    ]
    args = [q, k, v, g, s0, masks]
    if prev is not None:
        in_specs.append(pl.BlockSpec((1, tb, HV), tok))
        args.append(prev)
    kern = functools.partial(_recur_kernel, rev=rev, C=C, nsub=tb // C, dk=dk, dv=dv, has_prev=prev is not None)
    return pl.pallas_call(
        kern,
        grid=(B, nblk),
        in_specs=in_specs,
        out_specs=[pl.BlockSpec((1, tb, HV), tok), pl.BlockSpec((1, HV, HK), lambda b, i: (b, 0, 0))],
        out_shape=[jax.ShapeDtypeStruct((B, L, HV), F32), jax.ShapeDtypeStruct((B, HV, HK), F32)],
        scratch_shapes=[pltpu.VMEM((HV, HK), F32)],
        compiler_params=pltpu.CompilerParams(
            dimension_semantics=("parallel", "arbitrary"), vmem_limit_bytes=VMEM_LIMIT),
        name="recur_rev" if rev else "recur_fwd",
    )(*args)


def _bidir(ctx_in, lat_in, *, dk, dv):
    qc, kfc, kbc, vc, gfc, gbc = ctx_in
    ql, kfl, kbl, vl, gfl, gbl = lat_in
    B = qc.shape[0]
    s0 = jnp.zeros((B, vc.shape[-1], qc.shape[-1]), F32)
    kw = dict(dk=dk, dv=dv)
    oc_f, sc_f = _recurrence(qc, kfc, vc, gfc, s0, None, rev=False, tb=qc.shape[1], **kw)
    ol_f, _ = _recurrence(ql, kfl, vl, gfl, sc_f, None, rev=False, tb=256, **kw)
    oc, sc_b = _recurrence(qc, kbc, vc, gbc, s0, oc_f, rev=True, tb=qc.shape[1], **kw)
    ol, _ = _recurrence(ql, kbl, vl, gbl, sc_b, ol_f, rev=True, tb=256, **kw)
    return oc, ol


def _attn_kernel(lam_ref, q_ref, k_ref, vt_ref, g_ref, o_ref, qm_scr, m_scr, l_scr, acc_scr, *, tk):
    tq = q_ref.shape[1]
    W = q_ref.shape[2]
    nkb = k_ref.shape[1] // tk
    q = q_ref[0]
    lane_pair = lax.broadcasted_iota(jnp.int32, (tq, W), 1) // DA_DK
    for j in range(N_QK):
        qm_scr[j] = jnp.where(lane_pair == j, q, 0.0).astype(BF16)
    m_scr[...] = jnp.full(m_scr.shape, -jnp.inf, F32)
    l_scr[...] = jnp.zeros(l_scr.shape, F32)
    acc_scr[...] = jnp.zeros(acc_scr.shape, F32)

    def body(kb, carry):
        ks = pl.ds(pl.multiple_of(kb * tk, tk), tk)
        kblk = k_ref[0, ks, :]
        for h in range(DA_HEADS):
            vth = vt_ref[0, h * DA_DV:(h + 1) * DA_DV, ks]
            for w in range(2):
                j = 2 * h + w
                s = lax.dot_general(kblk, qm_scr[j], (((1,), (1,)), ((), ())), preferred_element_type=F32)
                m_old = m_scr[j]
                m_new = jnp.maximum(m_old, jnp.max(s, axis=0, keepdims=True))
                alpha = jnp.exp(m_old - m_new)
                p = jnp.exp(s - m_new)
                l_scr[j] = alpha * l_scr[j] + jnp.sum(p, axis=0, keepdims=True)
                acc_scr[j] = alpha * acc_scr[j] + jnp.dot(vth, p.astype(BF16), preferred_element_type=F32)
                m_scr[j] = m_new
        return carry

    lax.fori_loop(0, nkb, body, 0)
    lam = lam_ref[0]
    outs = []
    for h in range(DA_HEADS):
        o = acc_scr[2 * h] / l_scr[2 * h] - lam * (acc_scr[2 * h + 1] / l_scr[2 * h + 1])
        ms = jnp.mean(o * o, axis=0, keepdims=True)
        outs.append(o * lax.rsqrt(ms + EPS) * g_ref[...])
    o_ref[0] = jnp.concatenate(outs, axis=0).T


def _diff_attention(q, k, vt, lam, gcol, *, tq, tk):
    B, Lq, W = q.shape
    Lk = k.shape[1]
    HV = vt.shape[1]
    return pl.pallas_call(
        functools.partial(_attn_kernel, tk=tk),
        grid=(B, Lq // tq),
        in_specs=[
            pl.BlockSpec(memory_space=pltpu.SMEM),
            pl.BlockSpec((1, tq, W), lambda b, i: (b, i, 0)),
            pl.BlockSpec((1, Lk, W), lambda b, i: (b, 0, 0)),
            pl.BlockSpec((1, HV, Lk), lambda b, i: (b, 0, 0)),
            pl.BlockSpec((DA_DV, 1), lambda b, i: (0, 0)),
        ],
        out_specs=pl.BlockSpec((1, tq, HV), lambda b, i: (b, i, 0)),
        out_shape=jax.ShapeDtypeStruct((B, Lq, HV), F32),
        scratch_shapes=[
            pltpu.VMEM((N_QK, tq, W), BF16),
            pltpu.VMEM((N_QK, 1, tq), F32),
            pltpu.VMEM((N_QK, 1, tq), F32),
            pltpu.VMEM((N_QK, DA_DV, tq), F32),
        ],
        compiler_params=pltpu.CompilerParams(
            dimension_semantics=("parallel", "parallel"), vmem_limit_bytes=VMEM_LIMIT),
        name="diff_attn",
    )(lam, q, k, vt, gcol)


def _hyena_kernel(bias_ref, z_ref, gate_ref, hp_ref, o_ref, acc_scr, *, nb, nbatch):
    c = pl.program_id(0)
    half = nb // 2
    rows = nb * nbatch
    z = z_ref[0].reshape(rows, HB)
    zb = z.astype(BF16)
    hrow = hp_ref[0]
    acc_scr[...] = jnp.zeros(acc_scr.shape, F32)
    for off in range(-half, half + 1):
        lo = (off + half) * HB
        r = jnp.broadcast_to(hrow[:, lo:lo + 2 * HB], (HB, 2 * HB))
        w = pltpu.roll(r, 1, 1, stride=1, stride_axis=0)[:, HB:].astype(BF16)
        s0, s1 = max(0, -off), nb - max(0, off)
        if s1 <= s0:
            continue
        src = slice(s0 * nbatch, s1 * nbatch)
        dst = slice((s0 + off) * nbatch, (s1 + off) * nbatch)
        acc_scr[dst, :] += jnp.dot(zb[src], w, preferred_element_type=F32)
    y = acc_scr[...] + bias_ref[c] * z
    o_ref[0] = (gate_ref[0].reshape(rows, HB) * y).reshape(nb, nbatch, HB)


def _hyena_order(z, gate, hp, bias):
    C, nb, B, _ = z.shape
    blk = pl.BlockSpec((1, nb, B, HB), lambda c: (c, 0, 0, 0))
    return pl.pallas_call(
        functools.partial(_hyena_kernel, nb=nb, nbatch=B),
        grid=(C,),
        in_specs=[
            pl.BlockSpec(memory_space=pltpu.SMEM),
            blk, blk,
            pl.BlockSpec((1, 1, hp.shape[-1]), lambda c: (c, 0, 0)),
        ],
        out_specs=blk,
        out_shape=jax.ShapeDtypeStruct(z.shape, F32),
        scratch_shapes=[pltpu.VMEM((nb * B, HB), F32)],
        compiler_params=pltpu.CompilerParams(
            dimension_semantics=("parallel",), vmem_limit_bytes=VMEM_LIMIT),
        name="hyena_conv",
    )(bias, z, gate, hp)


def _rms_norm(x, g):
    xf = x.astype(F32)
    y = xf * lax.rsqrt(jnp.mean(xf * xf, axis=-1, keepdims=True) + EPS)
    return (y * g.astype(F32)).astype(x.dtype)


def _modulate(x, g, shift, scale):
    return _rms_norm(x, g) * (1 + scale) + shift


def _short_conv(u, w):
    up = jnp.pad(u, ((0, 0), (1, 1), (0, 0)))
    return up[:, :-2] * w[0] + up[:, 1:-1] * w[1] + up[:, 2:] * w[2]


def _gla_inputs(parts, wa2, ba):
    q = parts['gla_q'] * (GLA_DK ** -0.5)
    k = parts['gla_k']
    gf = jax.nn.log_sigmoid((parts['gla_af'] @ wa2[0] + ba[0]).astype(F32)) / GLA_TAU
    gb = jax.nn.log_sigmoid((parts['gla_ab'] @ wa2[1] + ba[1]).astype(F32)) / GLA_TAU
    return (q, k, k, parts['gla_v'], gf, gb)


def _hgrn_inputs(parts, lb):
    q = jax.nn.silu(parts['hg_q'])
    res = []
    for d, name in enumerate(('hg_ff', 'hg_fb')):
        z = parts[name].astype(F32)
        f = lb[d] + (1 - lb[d]) * jax.nn.sigmoid(z)
        res.append(((1 - lb[d]) * jax.nn.sigmoid(-z), jnp.log(jnp.maximum(f, F_TINY))))
    (kf, gf), (kb, gb) = res
    return (q, kf, kb, parts['hg_i'], gf, gb)


def _gated_branch(o, g, gate):
    B, L, W = o.shape
    y = _rms_norm(o.reshape(B, L, N_HEADS, W // N_HEADS), g).reshape(B, L, W)
    return y * jax.nn.silu(gate)


def _hyena_filters(L, p):
    j = jnp.arange(L, dtype=F32)
    t = j / max(L - 1, 1)
    w = 2 * math.pi * j / L
    f = jnp.linspace(1e-4, HY_BANDS - 1, HY_BANDS, dtype=F32)
    feats = jnp.concatenate([t[:, None], jnp.cos(w[:, None] * f), -jnp.sin(w[:, None] * f)], axis=-1)
    h = jnp.sin(p['hy_freq'][0] * (feats @ p['hy_w1'] + p['hy_b1']))
    h = jnp.sin(p['hy_freq'][1] * (h @ p['hy_w2'] + p['hy_b2']))
    h = (h @ p['hy_w3']).astype(F32)
    dist = jnp.abs(j - L // 2) / (L // 2)
    h = h * (jnp.exp(-dist[:, None] * jnp.abs(p['hy_decay'].astype(F32))) + HY_SHIFT)
    h = h / jnp.sum(jnp.abs(h), axis=0, keepdims=True)
    return h.reshape(L, HY_ORDER, HY_WIDTH)


def _to_blocks(a):
    B, L, C = a.shape
    return a.reshape(B, L // HB, HB, C).transpose(3, 1, 0, 2)


def _from_blocks(a):
    C, nb, B, _ = a.shape
    return a.transpose(2, 1, 3, 0).reshape(B, nb * HB, C)


def _pad_filter(h):
    L = h.shape[0]
    nb = L // HB
    left = (HB - 1) - (L // 2 - (nb // 2) * HB)
    return jnp.pad(h.T, ((0, 0), (left, (nb + 2) * HB - L - left)))[:, None, :]


def _hyena_branch(u3, p):
    L = u3.shape[1]
    v, x1, x2 = jnp.split(_short_conv(u3, p['hy_conv_w']), 1 + HY_ORDER, axis=-1)
    h = _hyena_filters(L, p)
    z = _to_blocks(v)
    for o, gate in enumerate((x1, x2)):
        z = _hyena_order(z, _to_blocks(gate), _pad_filter(h[:, o]), p['hy_bias'][o])
    return _from_blocks(z)


def _rope_tables(L):
    quarter = DA_DK // 4
    freqs = ROPE_BASE ** (-jnp.arange(quarter, dtype=F32) / quarter)
    row = (jnp.arange(L) // GRID_W).astype(F32)[:, None] * freqs
    col = (jnp.arange(L) % GRID_W).astype(F32)[:, None] * freqs
    return jnp.cos(row), jnp.sin(row), jnp.cos(col), jnp.sin(col)


def _da_prep(t, g, tables):
    B, L, W = t.shape
    t = _rms_norm(t.reshape(B, L, N_QK, DA_DK), g)
    if tables is not None:
        cr, sr, cc, sc = [a[None, :, None, :] for a in tables]
        e = DA_DK // 4
        a1, a2, b1, b2 = t[..., :e], t[..., e:2 * e], t[..., 2 * e:3 * e], t[..., 3 * e:]
        t = jnp.concatenate([a1 * cr - a2 * sr, a1 * sr + a2 * cr, b1 * cc - b2 * sc, b1 * sc + b2 * cc], axis=-1)
    return t.reshape(B, L, W)


def _split_parts(groups):
    parts = {}
    for (names, _), arr in zip(PROJ_GROUPS, groups):
        off = 0
        for nm in names:
            w = IN_WIDTHS[IN_NAMES.index(nm)]
            parts[nm] = arr[..., off:off + w]
            off += w
    return parts


def _group_weights(w_in):
    cols = []
    for names, width in PROJ_GROUPS:
        used = 0
        for nm in names:
            i = IN_NAMES.index(nm)
            cols.append(w_in[:, IN_OFFSETS[i]:IN_OFFSETS[i] + IN_WIDTHS[i]])
            used += IN_WIDTHS[i]
        if width > used:
            cols.append(jnp.zeros((w_in.shape[0], width - used), w_in.dtype))
    return jnp.concatenate(cols, axis=1).astype(BF16)


def _token_mixer(pc, px, p, lb, lam_init, last):
    gla_c, gla_x = _bidir(_gla_inputs(pc, p['gla_wa2'], p['gla_ba']),
                          _gla_inputs(px, p['gla_wa2'], p['gla_ba']), dk=GLA_DK, dv=GLA_DV)
    hg_c, hg_x = _bidir(_hgrn_inputs(pc, lb), _hgrn_inputs(px, lb), dk=HG_DK, dv=HG_DV)
    tables = _rope_tables(px['da_q'].shape[1])
    scale = DA_DK ** -0.5
    qx = _da_prep(px['da_q'], p['da_qnorm_g'], tables) * scale
    kx = _da_prep(px['da_k'], p['da_knorm_g'], tables)
    kc = _da_prep(pc['da_k'], p['da_knorm_g'], None)
    k_all = jnp.concatenate([kc, kx], axis=1).astype(BF16)
    vt_all = jnp.concatenate([pc['da_v'], px['da_v']], axis=1).transpose(0, 2, 1).astype(BF16)
    lp = p['da_lam'].astype(F32)
    lam = (jnp.exp(jnp.sum(lp[0] * lp[1])) - jnp.exp(jnp.sum(lp[2] * lp[3])) + lam_init).reshape(1)
    gcol = (p['da_norm_g'].astype(F32) * (1 - lam_init)).reshape(DA_DV, 1)
    da_x = _diff_attention(qx, k_all, vt_all, lam, gcol, tq=256, tk=256)

    def finish(parts, gla_o, hg_o, da_o):
        return (_gated_branch(gla_o, p['gla_norm_g'], parts['gla_g']),
                _hyena_branch(parts['hy'], p),
                _gated_branch(hg_o, p['hg_norm_g'], parts['hg_g']),
                da_o)

    outs_x = finish(px, gla_x, hg_x, da_x)
    if last:
        return None, outs_x
    Lc = kc.shape[1]
    qc = _da_prep(pc['da_q'], p['da_qnorm_g'], None) * scale
    da_c = _diff_attention(qc, k_all[:, :Lc], vt_all[:, :, :Lc], lam, gcol, tq=Lc, tk=Lc)
    return finish(pc, gla_c, hg_c, da_c), outs_x


def _expert_choice_moe(h, router, w1, w3, w2):
    B, L, D = h.shape
    cap = CAPACITY_FACTOR * L // N_EXPERTS
    aff = jax.nn.softmax((h @ router).astype(F32), axis=-1)
    g, idx = lax.top_k(aff.transpose(0, 2, 1), cap)
    xg = jax.vmap(lambda hb, ib: hb[ib])(h, idx)
    y = _expert_ffn(xg, g[..., None], w1, w3, w2)
    return jax.vmap(lambda yb, ib: jnp.zeros((L, D), h.dtype).at[ib.reshape(-1)].add(yb.reshape(-1, D)))(y, idx)


def kernel(x, c, ctx, c_ctx, ada_w, ada_b, norm1_g, norm2_g, w_in, gla_wa2, gla_ba, gla_norm_g,
           hy_conv_w, hy_w1, hy_b1, hy_w2, hy_b2, hy_w3, hy_freq, hy_decay, hy_bias,
           hg_lower, hg_norm_g, da_qnorm_g, da_knorm_g, da_lam, da_norm_g, w_branch, w_out,
           moe_router, moe_w1, moe_w3, moe_w2):
    B, L, D = x.shape
    P = jax.nn.softmax(hg_lower.astype(F32), axis=0)
    lower = jnp.cumsum(P, axis=0) - P[0]
    sc = jax.nn.silu(c)
    scc = jax.nn.silu(c_ctx)
    xc, xx = ctx, x
    for l in range(DEPTH):
        last = l == DEPTH - 1
        lam_init = 0.8 - 0.6 * math.exp(-0.3 * l)
        p = {'gla_wa2': gla_wa2[l], 'gla_ba': gla_ba[l], 'gla_norm_g': gla_norm_g[l],
             'hy_conv_w': hy_conv_w[l], 'hy_w1': hy_w1[l], 'hy_b1': hy_b1[l], 'hy_w2': hy_w2[l], 'hy_b2': hy_b2[l],
             'hy_w3': hy_w3[l], 'hy_freq': hy_freq[l], 'hy_decay': hy_decay[l], 'hy_bias': hy_bias[l],
             'hg_norm_g': hg_norm_g[l], 'da_qnorm_g': da_qnorm_g[l], 'da_knorm_g': da_knorm_g[l],
             'da_lam': da_lam[l], 'da_norm_g': da_norm_g[l]}
        mod_x = jnp.split((sc @ ada_w[l] + ada_b[l])[:, None, :], ADA_CHUNKS, axis=-1)
        mod_c1 = jnp.split((scc @ ada_w[l] + ada_b[l])[None, None, :], ADA_CHUNKS, axis=-1)
        mod_c = [jnp.broadcast_to(m, (B, 1, D)) for m in mod_c1]
        wg = _group_weights(w_in[l])
        wb = w_branch[l].astype(BF16)
        wo = w_out[l].astype(BF16)
        w1, w3, w2 = moe_w1[l].astype(BF16), moe_w3[l].astype(BF16), moe_w2[l].astype(BF16)
        gx = _in_proj(xx, norm1_g[l], mod_x[0], mod_x[1], wg, tm=256)
        gc = _in_proj(xc, norm1_g[l], mod_c[0], mod_c[1], wg, tm=256)
        px, pc = _split_parts(gx), _split_parts(gc)
        outs_c, outs_x = _token_mixer(pc, px, p, lower[l], lam_init, last)
        xx = _merge_out(outs_x, gx[4], wb, wo, xx, mod_x[2], tm=256)
        hx = _modulate(xx, norm2_g[l], mod_x[3], mod_x[4])
        xx = xx + mod_x[5] * _expert_choice_moe(hx, moe_router[l], w1, w3, w2)
        if not last:
            xc = _merge_out(outs_c, gc[4], wb, wo, xc, mod_c[2], tm=256)
            hc = _modulate(xc, norm2_g[l], mod_c[3], mod_c[4])
            xc = xc + mod_c[5] * _expert_choice_moe(hc, moe_router[l], w1, w3, w2)
    return xx
```

```python
import functools
import math

import jax
import jax.numpy as jnp
import numpy as np
from jax import lax
from jax.experimental import pallas as pl
from jax.experimental.pallas import tpu as pltpu

D_MODEL = 1024
DEPTH = 2
GRID_W = 64
N_BRANCH = 4
MIX_W = 256
GLA_HEADS = 4
GLA_DK = 32
GLA_DV = 64
GLA_RANK = 16
GLA_TAU = 16.0
HY_WIDTH = 256
HY_ORDER = 2
HY_BANDS = 16
HY_SHIFT = 0.05
HG_HEADS = 4
HG_DK = 64
HG_DV = 64
DA_HEADS = 4
DA_DK = 32
DA_DV = 64
ROPE_BASE = 10000.0
N_EXPERTS = 16
EXPERT_FF = 1024
CAPACITY_FACTOR = 2
ADA_CHUNKS = 6
EPS = 1e-6
F_TINY = 1e-20

IN_NAMES = ('gla_q', 'gla_k', 'gla_v', 'gla_af', 'gla_ab', 'gla_g', 'hy',
            'hg_q', 'hg_ff', 'hg_fb', 'hg_i', 'hg_g', 'da_q', 'da_k', 'da_v', 'merge')
IN_WIDTHS = (GLA_HEADS * GLA_DK, GLA_HEADS * GLA_DK, GLA_HEADS * GLA_DV, GLA_RANK, GLA_RANK, GLA_HEADS * GLA_DV,
             (1 + HY_ORDER) * HY_WIDTH,
             HG_HEADS * HG_DK, HG_HEADS * HG_DK, HG_HEADS * HG_DK, HG_HEADS * HG_DV, HG_HEADS * HG_DV,
             DA_HEADS * 2 * DA_DK, DA_HEADS * 2 * DA_DK, DA_HEADS * DA_DV,
             N_BRANCH * D_MODEL)
IN_OFFSETS = tuple(int(v) for v in np.cumsum((0,) + IN_WIDTHS)[:-1])

PROJ_GROUPS = (
    (('gla_q', 'gla_k', 'gla_v', 'gla_g', 'gla_af', 'gla_ab'), 896),
    (('hy',), 768),
    (('hg_q', 'hg_ff', 'hg_fb', 'hg_i', 'hg_g'), 1280),
    (('da_q', 'da_k', 'da_v'), 768),
    (('merge',), 4096),
)
PROJ_WIDTHS = tuple(w for _, w in PROJ_GROUPS)
VMEM_LIMIT = 56 * 1024 * 1024

BF16 = jnp.bfloat16
F32 = jnp.float32
N_HEADS = 4
RC = 64
N_QK = 2 * DA_HEADS
VROWS = DA_DV + 8
LOG2E = 1.4426950408889634
HB = 256


def _in_proj_kernel(x_ref, g_ref, sh_ref, sc_ref, w_ref, *out_refs):
    x = x_ref[0]
    ms = jnp.mean(x * x, axis=-1, keepdims=True)
    h = x * lax.rsqrt(ms + EPS) * g_ref[...]
    h = (h * (1.0 + sc_ref[0]) + sh_ref[0]).astype(BF16)
    off = 0
    for o_ref, w in zip(out_refs, PROJ_WIDTHS):
        o_ref[0] = jnp.dot(h, w_ref[:, off:off + w], preferred_element_type=F32)
        off += w


def _in_proj(x, g, shift, scale, w_groups, tm):
    B, L, D = x.shape
    nw = w_groups.shape[1]
    return pl.pallas_call(
        _in_proj_kernel,
        grid=(B, L // tm),
        in_specs=[
            pl.BlockSpec((1, tm, D), lambda b, i: (b, i, 0)),
            pl.BlockSpec((1, D), lambda b, i: (0, 0)),
            pl.BlockSpec((1, 1, D), lambda b, i: (b, 0, 0)),
            pl.BlockSpec((1, 1, D), lambda b, i: (b, 0, 0)),
            pl.BlockSpec((D, nw), lambda b, i: (0, 0), pipeline_mode=pl.Buffered(1)),
        ],
        out_specs=[pl.BlockSpec((1, tm, w), lambda b, i: (b, i, 0)) for w in PROJ_WIDTHS],
        out_shape=[jax.ShapeDtypeStruct((B, L, w), F32) for w in PROJ_WIDTHS],
        compiler_params=pltpu.CompilerParams(
            dimension_semantics=("parallel", "parallel"), vmem_limit_bytes=VMEM_LIMIT),
        name="in_proj",
    )(x, g.reshape(1, D), shift, scale, w_groups)


def _merge_kernel(o0_ref, o1_ref, o2_ref, o3_ref, gate_ref, wb_ref, wo_ref, x_ref, m_ref, out_ref):
    D = D_MODEL
    acc = None
    for i, o_ref in enumerate((o0_ref, o1_ref, o2_ref, o3_ref)):
        t = jnp.dot(o_ref[0].astype(BF16), wb_ref[i], preferred_element_type=F32)
        t = jax.nn.sigmoid(gate_ref[0, :, i * D:(i + 1) * D]) * t
        acc = t if acc is None else acc + t
    mx = jnp.dot(acc.astype(BF16), wo_ref[...], preferred_element_type=F32)
    out_ref[0] = x_ref[0] + m_ref[0] * mx


def _merge_out(outs, gate_cols, w_branch, w_out, x, m, tm):
    B, L, D = x.shape
    return pl.pallas_call(
        _merge_kernel,
        grid=(B, L // tm),
        in_specs=[pl.BlockSpec((1, tm, MIX_W), lambda b, i: (b, i, 0)) for _ in range(N_BRANCH)] + [
            pl.BlockSpec((1, tm, N_BRANCH * D), lambda b, i: (b, i, 0)),
            pl.BlockSpec((N_BRANCH, MIX_W, D), lambda b, i: (0, 0, 0)),
            pl.BlockSpec((D, D), lambda b, i: (0, 0)),
            pl.BlockSpec((1, tm, D), lambda b, i: (b, i, 0)),
            pl.BlockSpec((1, 1, D), lambda b, i: (b, 0, 0)),
        ],
        out_specs=pl.BlockSpec((1, tm, D), lambda b, i: (b, i, 0)),
        out_shape=jax.ShapeDtypeStruct((B, L, D), F32),
        compiler_params=pltpu.CompilerParams(
            dimension_semantics=("parallel", "parallel"), vmem_limit_bytes=VMEM_LIMIT),
        name="merge_out",
    )(*outs, gate_cols, w_branch, w_out, x, m)


def _expert_ffn_kernel(x_ref, g_ref, w1_ref, w3_ref, w2_ref, out_ref):
    x = x_ref[0, 0].astype(BF16)
    a = jnp.dot(x, w1_ref[0], preferred_element_type=F32)
    b = jnp.dot(x, w3_ref[0], preferred_element_type=F32)
    h = (a * jax.nn.sigmoid(a) * b).astype(BF16)
    y = jnp.dot(h, w2_ref[0], preferred_element_type=F32)
    out_ref[0, 0] = y * g_ref[0, 0]


def _expert_ffn(xg, g, w1, w3, w2):
    B, E, cap, D = xg.shape
    F = w1.shape[-1]
    return pl.pallas_call(
        _expert_ffn_kernel,
        grid=(E, B),
        in_specs=[
            pl.BlockSpec((1, 1, cap, D), lambda e, b: (b, e, 0, 0)),
            pl.BlockSpec((1, 1, cap, 1), lambda e, b: (b, e, 0, 0)),
            pl.BlockSpec((1, D, F), lambda e, b: (e, 0, 0)),
            pl.BlockSpec((1, D, F), lambda e, b: (e, 0, 0)),
            pl.BlockSpec((1, F, D), lambda e, b: (e, 0, 0)),
        ],
        out_specs=pl.BlockSpec((1, 1, cap, D), lambda e, b: (b, e, 0, 0)),
        out_shape=jax.ShapeDtypeStruct((B, E, cap, D), F32),
        compiler_params=pltpu.CompilerParams(
            dimension_semantics=("parallel", "parallel"), vmem_limit_bytes=VMEM_LIMIT),
        name="expert_ffn",
    )(xg, g, w1, w3, w2)


def _recur_masks(C):
    t = np.arange(C)[:, None]
    s = np.tile(np.arange(C), N_HEADS)[None, :]
    ms = []
    n = 2
    while n < C:
        ms.append((t // n == s // n).astype(np.float32))
        n *= 2
    ms.append((t == s).astype(np.float32))
    return jnp.asarray(np.stack(ms))


def _recur_kernel(*refs, rev, C, nsub, dk, dv, has_prev):
    if has_prev:
        q_ref, k_ref, v_ref, g_ref, s0_ref, m_ref, prev_ref, o_ref, sfin_ref, s_scr = refs
    else:
        q_ref, k_ref, v_ref, g_ref, s0_ref, m_ref, o_ref, sfin_ref, s_scr = refs
        prev_ref = None
    HK = N_HEADS * dk
    HV = N_HEADS * dv
    i = pl.program_id(1)

    @pl.when(i == 0)
    def _():
        s_scr[...] = s0_ref[0]

    row = lax.broadcasted_iota(jnp.int32, (C, HK), 0)
    pos = (C - 1 - row) if rev else row
    lane_head_k = lax.broadcasted_iota(jnp.int32, (C, HK), 1) // dk
    lane_head_v = lax.broadcasted_iota(jnp.int32, (C, HV), 1) // dv
    bd = (lax.broadcasted_iota(jnp.int32, (HV, HK), 0) // dv
          == lax.broadcasted_iota(jnp.int32, (HV, HK), 1) // dk)

    def ahead(x, s):
        return pltpu.roll(x, (C - s) if rev else s, axis=0)

    def behind(x, s):
        return pltpu.roll(x, s if rev else (C - s), axis=0)

    def stack_heads(x, lane_head):
        return jnp.concatenate([jnp.where(lane_head == h, x, 0.0) for h in range(N_HEADS)], axis=0).astype(BF16)

    def nt(a, b):
        return lax.dot_general(a, b, (((1,), (1,)), ((), ())), preferred_element_type=F32)

    def chunk(j, carry):
        c = (nsub - 1 - j) if rev else j
        sl = pl.ds(pl.multiple_of(c * C, C), C)
        q = q_ref[0, sl, :]
        k = k_ref[0, sl, :]
        v = v_ref[0, sl, :]
        g = g_ref[0, sl, :]
        b = g
        s = 1
        while s < C:
            b = b + jnp.where(pos >= s, ahead(b, s), 0.0)
            s *= 2
        a = nt(q.astype(BF16), stack_heads(k, lane_head_k)) * m_ref[m_ref.shape[0] - 1]
        m, lvl = 1, 0
        while m < C:
            n = 2 * m
            off = pos % n
            rq = jnp.where(off == m, ahead(b, 1), 0.0)
            rk = jnp.where(off == m - 1, b, 0.0)
            s = 1
            while s < m:
                rq = rq + ahead(rq, s)
                rk = rk + behind(rk, s)
                s *= 2
            late = off >= m
            qt = jnp.where(late, q * jnp.exp(jnp.where(late, b - rq, 0.0)), 0.0)
            kt = jnp.where(late, 0.0, k * jnp.exp(jnp.where(late, 0.0, rk - b)))
            al = nt(qt.astype(BF16), stack_heads(kt, lane_head_k))
            a = a + (al if n == C else al * m_ref[lvl])
            m, lvl = n, lvl + 1
        st = s_scr[...]
        o = jnp.dot(a.astype(BF16), stack_heads(v, lane_head_v), preferred_element_type=F32)
        o = o + nt((q * jnp.exp(b)).astype(BF16), st.astype(BF16))
        if prev_ref is not None:
            o = o + prev_ref[0, sl, :]
        o_ref[0, sl, :] = o
        b_end = b[0:1, :] if rev else b[C - 1:C, :]
        kend = (k * jnp.exp(b_end - b)).astype(BF16)
        upd = lax.dot_general(v.astype(BF16), kend, (((0,), (0,)), ((), ())), preferred_element_type=F32)
        s_scr[...] = st * jnp.exp(b_end) + jnp.where(bd, upd, 0.0)
        return carry

    lax.fori_loop(0, nsub, chunk, 0)

    @pl.when(i == pl.num_programs(1) - 1)
    def _():
        sfin_ref[0] = s_scr[...]


def _recurrence(q, k, v, g, s0, prev, *, rev, dk, dv, tb):
    B, L, HK = q.shape
    HV = v.shape[-1]
    C = RC
    nblk = L // tb
    masks = _recur_masks(C)
    tok = (lambda b, i: (b, nblk - 1 - i, 0)) if rev else (lambda b, i: (b, i, 0))
    in_specs = [
        pl.BlockSpec((1, tb, HK), tok), pl.BlockSpec((1, tb, HK), tok),
        pl.BlockSpec((1, tb, HV), tok), pl.BlockSpec((1, tb, HK), tok),
        pl.BlockSpec((1, HV, HK), lambda b, i: (b, 0, 0)),
        pl.BlockSpec(masks.shape, lambda b, i: (0, 0, 0)),
    ]
    args = [q, k, v, g, s0, masks]
    if prev is not None:
        in_specs.append(pl.BlockSpec((1, tb, HV), tok))
        args.append(prev)
    kern = functools.partial(_recur_kernel, rev=rev, C=C, nsub=tb // C, dk=dk, dv=dv, has_prev=prev is not None)
    return pl.pallas_call(
        kern,
        grid=(B, nblk),
        in_specs=in_specs,
        out_specs=[pl.BlockSpec((1, tb, HV), tok), pl.BlockSpec((1, HV, HK), lambda b, i: (b, 0, 0))],
        out_shape=[jax.ShapeDtypeStruct((B, L, HV), F32), jax.ShapeDtypeStruct((B, HV, HK), F32)],
        scratch_shapes=[pltpu.VMEM((HV, HK), F32)],
        compiler_params=pltpu.CompilerParams(
            dimension_semantics=("parallel", "arbitrary"), vmem_limit_bytes=VMEM_LIMIT),
        name="recur_rev" if rev else "recur_fwd",
    )(*args)


def _bidir(ctx_in, lat_in, *, dk, dv):
    qc, kfc, kbc, vc, gfc, gbc = ctx_in
    ql, kfl, kbl, vl, gfl, gbl = lat_in
    B = qc.shape[0]
    s0 = jnp.zeros((B, vc.shape[-1], qc.shape[-1]), F32)
    kw = dict(dk=dk, dv=dv)
    oc_f, sc_f = _recurrence(qc, kfc, vc, gfc, s0, None, rev=False, tb=qc.shape[1], **kw)
    ol_f, _ = _recurrence(ql, kfl, vl, gfl, sc_f, None, rev=False, tb=256, **kw)
    oc, sc_b = _recurrence(qc, kbc, vc, gbc, s0, oc_f, rev=True, tb=qc.shape[1], **kw)
    ol, _ = _recurrence(ql, kbl, vl, gbl, sc_b, ol_f, rev=True, tb=256, **kw)
    return oc, ol


def _attn_kernel(lam_ref, qt_ref, k_ref, vt_ref, g_ref, o_ref, qm_scr, s_scr, p_scr, m_scr, a_scr, acc_scr, *, tk):
    W = qt_ref.shape[1]
    tq = qt_ref.shape[2]
    nkb = k_ref.shape[1] // tk
    qt = qt_ref[0]
    row_pair = lax.broadcasted_iota(jnp.int32, (W, tq), 0) // DA_DK
    for j in range(N_QK):
        qm_scr[j] = jnp.where(row_pair == j, qt, 0.0).astype(BF16)
    m_scr[...] = jnp.full(m_scr.shape, -jnp.inf, F32)
    acc_scr[...] = jnp.zeros(acc_scr.shape, F32)

    def body(kb, carry):
        ks = pl.ds(pl.multiple_of(kb * tk, tk), tk)
        kblk = k_ref[0, ks, :]
        for j in range(N_QK):
            s_scr[j] = jnp.dot(kblk, qm_scr[j], preferred_element_type=F32)
        for j in range(N_QK):
            s = s_scr[j]
            m_old = m_scr[j]
            m_new = jnp.maximum(m_old, jnp.max(s, axis=0, keepdims=True))
            a_scr[j] = jnp.exp2(m_old - m_new)
            p_scr[j] = jnp.exp2(s - m_new).astype(BF16)
            m_scr[j] = m_new
        for j in range(N_QK):
            pv = jnp.dot(vt_ref[0, j // 2, :, ks], p_scr[j], preferred_element_type=F32)
            acc_scr[j] = a_scr[j] * acc_scr[j] + pv
        return carry

    lax.fori_loop(0, nkb, body, 0)
    lam = lam_ref[0]
    outs = []
    for h in range(DA_HEADS):
        a1, a2 = acc_scr[2 * h], acc_scr[2 * h + 1]
        o = a1[:DA_DV] / a1[DA_DV:DA_DV + 1] - lam * (a2[:DA_DV] / a2[DA_DV:DA_DV + 1])
        ms = jnp.mean(o * o, axis=0, keepdims=True)
        outs.append(o * lax.rsqrt(ms + EPS) * g_ref[...])
    o_ref[0] = jnp.concatenate(outs, axis=0).T


def _diff_attention(qt, k, vt, lam, gcol, *, tq, tk):
    B, W, Lq = qt.shape
    Lk = k.shape[1]
    HV = DA_HEADS * DA_DV
    return pl.pallas_call(
        functools.partial(_attn_kernel, tk=tk),
        grid=(B, Lq // tq),
        in_specs=[
            pl.BlockSpec(memory_space=pltpu.SMEM),
            pl.BlockSpec((1, W, tq), lambda b, i: (b, 0, i)),
            pl.BlockSpec((1, Lk, W), lambda b, i: (b, 0, 0)),
            pl.BlockSpec((1, DA_HEADS, VROWS, Lk), lambda b, i: (b, 0, 0, 0)),
            pl.BlockSpec((DA_DV, 1), lambda b, i: (0, 0)),
        ],
        out_specs=pl.BlockSpec((1, tq, HV), lambda b, i: (b, i, 0)),
        out_shape=jax.ShapeDtypeStruct((B, Lq, HV), F32),
        scratch_shapes=[
            pltpu.VMEM((N_QK, W, tq), BF16),
            pltpu.VMEM((N_QK, tk, tq), F32),
            pltpu.VMEM((N_QK, tk, tq), BF16),
            pltpu.VMEM((N_QK, 1, tq), F32),
            pltpu.VMEM((N_QK, 1, tq), F32),
            pltpu.VMEM((N_QK, VROWS, tq), F32),
        ],
        compiler_params=pltpu.CompilerParams(
            dimension_semantics=("parallel", "parallel"), vmem_limit_bytes=VMEM_LIMIT),
        name="diff_attn",
    )(lam, qt, k, vt, gcol)


def _hyena_kernel(bias_ref, z_ref, gate_ref, hp_ref, o_ref, acc_scr, *, nb, nbatch):
    c = pl.program_id(0)
    half = nb // 2
    rows = nb * nbatch
    z = z_ref[0].reshape(rows, HB)
    zb = z.astype(BF16)
    hrow = hp_ref[0]
    acc_scr[...] = jnp.zeros(acc_scr.shape, F32)
    for off in range(-half, half + 1):
        lo = (off + half) * HB
        r = jnp.broadcast_to(hrow[:, lo:lo + 2 * HB], (HB, 2 * HB))
        w = pltpu.roll(r, 1, 1, stride=1, stride_axis=0)[:, HB:].astype(BF16)
        s0, s1 = max(0, -off), nb - max(0, off)
        if s1 <= s0:
            continue
        src = slice(s0 * nbatch, s1 * nbatch)
        dst = slice((s0 + off) * nbatch, (s1 + off) * nbatch)
        acc_scr[dst, :] += jnp.dot(zb[src], w, preferred_element_type=F32)
    y = acc_scr[...] + bias_ref[c] * z
    o_ref[0] = (gate_ref[0].reshape(rows, HB) * y).reshape(nb, nbatch, HB)


def _hyena_order(z, gate, hp, bias):
    C, nb, B, _ = z.shape
    blk = pl.BlockSpec((1, nb, B, HB), lambda c: (c, 0, 0, 0))
    return pl.pallas_call(
        functools.partial(_hyena_kernel, nb=nb, nbatch=B),
        grid=(C,),
        in_specs=[
            pl.BlockSpec(memory_space=pltpu.SMEM),
            blk, blk,
            pl.BlockSpec((1, 1, hp.shape[-1]), lambda c: (c, 0, 0)),
        ],
        out_specs=blk,
        out_shape=jax.ShapeDtypeStruct(z.shape, F32),
        scratch_shapes=[pltpu.VMEM((nb * B, HB), F32)],
        compiler_params=pltpu.CompilerParams(
            dimension_semantics=("parallel",), vmem_limit_bytes=VMEM_LIMIT),
        name="hyena_conv",
    )(bias, z, gate, hp)


def _rms_norm(x, g):
    xf = x.astype(F32)
    y = xf * lax.rsqrt(jnp.mean(xf * xf, axis=-1, keepdims=True) + EPS)
    return (y * g.astype(F32)).astype(x.dtype)


def _modulate(x, g, shift, scale):
    return _rms_norm(x, g) * (1 + scale) + shift


def _short_conv(u, w):
    up = jnp.pad(u, ((0, 0), (1, 1), (0, 0)))
    return up[:, :-2] * w[0] + up[:, 1:-1] * w[1] + up[:, 2:] * w[2]


def _gla_inputs(parts, wa2, ba):
    q = parts['gla_q'] * (GLA_DK ** -0.5)
    k = parts['gla_k']
    gf = jax.nn.log_sigmoid((parts['gla_af'] @ wa2[0] + ba[0]).astype(F32)) / GLA_TAU
    gb = jax.nn.log_sigmoid((parts['gla_ab'] @ wa2[1] + ba[1]).astype(F32)) / GLA_TAU
    return (q, k, k, parts['gla_v'], gf, gb)


def _hgrn_inputs(parts, lb):
    q = jax.nn.silu(parts['hg_q'])
    res = []
    for d, name in enumerate(('hg_ff', 'hg_fb')):
        z = parts[name].astype(F32)
        f = lb[d] + (1 - lb[d]) * jax.nn.sigmoid(z)
        res.append(((1 - lb[d]) * jax.nn.sigmoid(-z), jnp.log(jnp.maximum(f, F_TINY))))
    (kf, gf), (kb, gb) = res
    return (q, kf, kb, parts['hg_i'], gf, gb)


def _gated_branch(o, g, gate):
    B, L, W = o.shape
    y = _rms_norm(o.reshape(B, L, N_HEADS, W // N_HEADS), g).reshape(B, L, W)
    return y * jax.nn.silu(gate)


def _hyena_filters(L, p):
    j = jnp.arange(L, dtype=F32)
    t = j / max(L - 1, 1)
    w = 2 * math.pi * j / L
    f = jnp.linspace(1e-4, HY_BANDS - 1, HY_BANDS, dtype=F32)
    feats = jnp.concatenate([t[:, None], jnp.cos(w[:, None] * f), -jnp.sin(w[:, None] * f)], axis=-1)
    h = jnp.sin(p['hy_freq'][0] * (feats @ p['hy_w1'] + p['hy_b1']))
    h = jnp.sin(p['hy_freq'][1] * (h @ p['hy_w2'] + p['hy_b2']))
    h = (h @ p['hy_w3']).astype(F32)
    dist = jnp.abs(j - L // 2) / (L // 2)
    h = h * (jnp.exp(-dist[:, None] * jnp.abs(p['hy_decay'].astype(F32))) + HY_SHIFT)
    h = h / jnp.sum(jnp.abs(h), axis=0, keepdims=True)
    return h.reshape(L, HY_ORDER, HY_WIDTH)


def _to_blocks(a):
    B, L, C = a.shape
    return a.reshape(B, L // HB, HB, C).transpose(3, 1, 0, 2)


def _from_blocks(a):
    C, nb, B, _ = a.shape
    return a.transpose(2, 1, 3, 0).reshape(B, nb * HB, C)


def _pad_filter(h):
    L = h.shape[0]
    nb = L // HB
    left = (HB - 1) - (L // 2 - (nb // 2) * HB)
    return jnp.pad(h.T, ((0, 0), (left, (nb + 2) * HB - L - left)))[:, None, :]


def _hyena_branch(u3, p):
    L = u3.shape[1]
    v, x1, x2 = jnp.split(_short_conv(u3, p['hy_conv_w']), 1 + HY_ORDER, axis=-1)
    h = _hyena_filters(L, p)
    z = _to_blocks(v)
    for o, gate in enumerate((x1, x2)):
        z = _hyena_order(z, _to_blocks(gate), _pad_filter(h[:, o]), p['hy_bias'][o])
    return _from_blocks(z)


def _rope_tables(L):
    quarter = DA_DK // 4
    freqs = ROPE_BASE ** (-jnp.arange(quarter, dtype=F32) / quarter)
    row = (jnp.arange(L) // GRID_W).astype(F32)[:, None] * freqs
    col = (jnp.arange(L) % GRID_W).astype(F32)[:, None] * freqs
    return jnp.cos(row), jnp.sin(row), jnp.cos(col), jnp.sin(col)


def _da_prep(t, g, tables):
    B, L, W = t.shape
    t = _rms_norm(t.reshape(B, L, N_QK, DA_DK), g)
    if tables is not None:
        cr, sr, cc, sc = [a[None, :, None, :] for a in tables]
        e = DA_DK // 4
        a1, a2, b1, b2 = t[..., :e], t[..., e:2 * e], t[..., 2 * e:3 * e], t[..., 3 * e:]
        t = jnp.concatenate([a1 * cr - a2 * sr, a1 * sr + a2 * cr, b1 * cc - b2 * sc, b1 * sc + b2 * cc], axis=-1)
    return t.reshape(B, L, W)


def _value_rows(v):
    B, Lk, _ = v.shape
    vt = v.reshape(B, Lk, DA_HEADS, DA_DV).transpose(0, 2, 3, 1)
    extra = jnp.concatenate([jnp.ones((B, DA_HEADS, 1, Lk), v.dtype),
                             jnp.zeros((B, DA_HEADS, VROWS - DA_DV - 1, Lk), v.dtype)], axis=2)
    return jnp.concatenate([vt, extra], axis=2).astype(BF16)


def _split_parts(groups):
    parts = {}
    for (names, _), arr in zip(PROJ_GROUPS, groups):
        off = 0
        for nm in names:
            w = IN_WIDTHS[IN_NAMES.index(nm)]
            parts[nm] = arr[..., off:off + w]
            off += w
    return parts


def _group_weights(w_in):
    cols = []
    for names, width in PROJ_GROUPS:
        used = 0
        for nm in names:
            i = IN_NAMES.index(nm)
            cols.append(w_in[:, IN_OFFSETS[i]:IN_OFFSETS[i] + IN_WIDTHS[i]])
            used += IN_WIDTHS[i]
        if width > used:
            cols.append(jnp.zeros((w_in.shape[0], width - used), w_in.dtype))
    return jnp.concatenate(cols, axis=1).astype(BF16)


def _token_mixer(pc, px, p, lb, lam_init, last):
    gla_c, gla_x = _bidir(_gla_inputs(pc, p['gla_wa2'], p['gla_ba']),
                          _gla_inputs(px, p['gla_wa2'], p['gla_ba']), dk=GLA_DK, dv=GLA_DV)
    hg_c, hg_x = _bidir(_hgrn_inputs(pc, lb), _hgrn_inputs(px, lb), dk=HG_DK, dv=HG_DV)
    tables = _rope_tables(px['da_q'].shape[1])
    scale = DA_DK ** -0.5 * LOG2E
    qx = (_da_prep(px['da_q'], p['da_qnorm_g'], tables) * scale).transpose(0, 2, 1)
    kx = _da_prep(px['da_k'], p['da_knorm_g'], tables)
    kc = _da_prep(pc['da_k'], p['da_knorm_g'], None)
    k_all = jnp.concatenate([kc, kx], axis=1).astype(BF16)
    vt_all = _value_rows(jnp.concatenate([pc['da_v'], px['da_v']], axis=1))
    lp = p['da_lam'].astype(F32)
    lam = (jnp.exp(jnp.sum(lp[0] * lp[1])) - jnp.exp(jnp.sum(lp[2] * lp[3])) + lam_init).reshape(1)
    gcol = (p['da_norm_g'].astype(F32) * (1 - lam_init)).reshape(DA_DV, 1)
    da_x = _diff_attention(qx, k_all, vt_all, lam, gcol, tq=256, tk=256)

    def finish(parts, gla_o, hg_o, da_o):
        return (_gated_branch(gla_o, p['gla_norm_g'], parts['gla_g']),
                _hyena_branch(parts['hy'], p),
                _gated_branch(hg_o, p['hg_norm_g'], parts['hg_g']),
                da_o)

    outs_x = finish(px, gla_x, hg_x, da_x)
    if last:
        return None, outs_x
    Lc = kc.shape[1]
    qc = (_da_prep(pc['da_q'], p['da_qnorm_g'], None) * scale).transpose(0, 2, 1)
    da_c = _diff_attention(qc, k_all[:, :Lc], vt_all[..., :Lc], lam, gcol, tq=Lc, tk=Lc)
    return finish(pc, gla_c, hg_c, da_c), outs_x


def _expert_choice_moe(h, router, w1, w3, w2):
    B, L, D = h.shape
    cap = CAPACITY_FACTOR * L // N_EXPERTS
    aff = jax.nn.softmax((h @ router).astype(F32), axis=-1)
    g, idx = lax.top_k(aff.transpose(0, 2, 1), cap)
    xg = jax.vmap(lambda hb, ib: hb[ib])(h, idx)
    y = _expert_ffn(xg, g[..., None], w1, w3, w2)
    return jax.vmap(lambda yb, ib: jnp.zeros((L, D), h.dtype).at[ib.reshape(-1)].add(yb.reshape(-1, D)))(y, idx)


def kernel(x, c, ctx, c_ctx, ada_w, ada_b, norm1_g, norm2_g, w_in, gla_wa2, gla_ba, gla_norm_g,
           hy_conv_w, hy_w1, hy_b1, hy_w2, hy_b2, hy_w3, hy_freq, hy_decay, hy_bias,
           hg_lower, hg_norm_g, da_qnorm_g, da_knorm_g, da_lam, da_norm_g, w_branch, w_out,
           moe_router, moe_w1, moe_w3, moe_w2):
    B, L, D = x.shape
    P = jax.nn.softmax(hg_lower.astype(F32), axis=0)
    lower = jnp.cumsum(P, axis=0) - P[0]
    sc = jax.nn.silu(c)
    scc = jax.nn.silu(c_ctx)
    xc, xx = ctx, x
    for l in range(DEPTH):
        last = l == DEPTH - 1
        lam_init = 0.8 - 0.6 * math.exp(-0.3 * l)
        p = {'gla_wa2': gla_wa2[l], 'gla_ba': gla_ba[l], 'gla_norm_g': gla_norm_g[l],
             'hy_conv_w': hy_conv_w[l], 'hy_w1': hy_w1[l], 'hy_b1': hy_b1[l], 'hy_w2': hy_w2[l], 'hy_b2': hy_b2[l],
             'hy_w3': hy_w3[l], 'hy_freq': hy_freq[l], 'hy_decay': hy_decay[l], 'hy_bias': hy_bias[l],
             'hg_norm_g': hg_norm_g[l], 'da_qnorm_g': da_qnorm_g[l], 'da_knorm_g': da_knorm_g[l],
             'da_lam': da_lam[l], 'da_norm_g': da_norm_g[l]}
        mod_x = jnp.split((sc @ ada_w[l] + ada_b[l])[:, None, :], ADA_CHUNKS, axis=-1)
        mod_c1 = jnp.split((scc @ ada_w[l] + ada_b[l])[None, None, :], ADA_CHUNKS, axis=-1)
        mod_c = [jnp.broadcast_to(m, (B, 1, D)) for m in mod_c1]
        wg = _group_weights(w_in[l])
        wb = w_branch[l].astype(BF16)
        wo = w_out[l].astype(BF16)
        w1, w3, w2 = moe_w1[l].astype(BF16), moe_w3[l].astype(BF16), moe_w2[l].astype(BF16)
        gx = _in_proj(xx, norm1_g[l], mod_x[0], mod_x[1], wg, tm=256)
        gc = _in_proj(xc, norm1_g[l], mod_c[0], mod_c[1], wg, tm=256)
        px, pc = _split_parts(gx), _split_parts(gc)
        outs_c, outs_x = _token_mixer(pc, px, p, lower[l], lam_init, last)
        xx = _merge_out(outs_x, gx[4], wb, wo, xx, mod_x[2], tm=256)
        hx = _modulate(xx, norm2_g[l], mod_x[3], mod_x[4])
        xx = xx + mod_x[5] * _expert_choice_moe(hx, moe_router[l], w1, w3, w2)
        if not last:
            xc = _merge_out(outs_c, gc[4], wb, wo, xc, mod_c[2], tm=256)
            hc = _modulate(xc, norm2_g[l], mod_c[3], mod_c[4])
            xc = xc + mod_c[5] * _expert_choice_moe(hc, moe_router[l], w1, w3, w2)
    return xx
```

```python
import functools
import math

import jax
import jax.numpy as jnp
import numpy as np
from jax import lax
from jax.experimental import pallas as pl
from jax.experimental.pallas import tpu as pltpu

D_MODEL = 1024
DEPTH = 2
GRID_W = 64
N_BRANCH = 4
MIX_W = 256
GLA_HEADS = 4
GLA_DK = 32
GLA_DV = 64
GLA_RANK = 16
GLA_TAU = 16.0
HY_WIDTH = 256
HY_ORDER = 2
HY_BANDS = 16
HY_SHIFT = 0.05
HG_HEADS = 4
HG_DK = 64
HG_DV = 64
DA_HEADS = 4
DA_DK = 32
DA_DV = 64
ROPE_BASE = 10000.0
N_EXPERTS = 16
EXPERT_FF = 1024
CAPACITY_FACTOR = 2
ADA_CHUNKS = 6
EPS = 1e-6
F_TINY = 1e-20

IN_NAMES = ('gla_q', 'gla_k', 'gla_v', 'gla_af', 'gla_ab', 'gla_g', 'hy',
            'hg_q', 'hg_ff', 'hg_fb', 'hg_i', 'hg_g', 'da_q', 'da_k', 'da_v', 'merge')
IN_WIDTHS = (GLA_HEADS * GLA_DK, GLA_HEADS * GLA_DK, GLA_HEADS * GLA_DV, GLA_RANK, GLA_RANK, GLA_HEADS * GLA_DV,
             (1 + HY_ORDER) * HY_WIDTH,
             HG_HEADS * HG_DK, HG_HEADS * HG_DK, HG_HEADS * HG_DK, HG_HEADS * HG_DV, HG_HEADS * HG_DV,
             DA_HEADS * 2 * DA_DK, DA_HEADS * 2 * DA_DK, DA_HEADS * DA_DV,
             N_BRANCH * D_MODEL)
IN_OFFSETS = tuple(int(v) for v in np.cumsum((0,) + IN_WIDTHS)[:-1])

PROJ_GROUPS = (
    (('gla_q', 'gla_k', 'gla_v', 'gla_g', 'gla_af', 'gla_ab'), 896),
    (('hy',), 768),
    (('hg_q', 'hg_ff', 'hg_fb', 'hg_i', 'hg_g'), 1280),
    (('da_q', 'da_k', 'da_v'), 768),
    (('merge',), 4096),
)
PROJ_WIDTHS = tuple(w for _, w in PROJ_GROUPS)
LANES = 128
OUT_WIDTHS = (4 * GLA_HEADS * GLA_DK + 2 * GLA_HEADS * GLA_DV, PROJ_WIDTHS[1],
              5 * HG_HEADS * HG_DK + 2 * HG_HEADS * HG_DV, PROJ_WIDTHS[3], PROJ_WIDTHS[4])
VMEM_LIMIT = 56 * 1024 * 1024

BF16 = jnp.bfloat16
F32 = jnp.float32
N_HEADS = 4
RC = 64
N_QK = 2 * DA_HEADS
VROWS = DA_DV + 8
LOG2E = 1.4426950408889634
HB = 256


def _split_bf16(a):
    hi = a.astype(BF16)
    return hi, (a - hi.astype(F32)).astype(BF16)


def _in_proj_kernel(x_ref, g_ref, sh_ref, sc_ref, w_ref, wa_ref, ba_ref, lb_ref,
                    gla_ref, hy_ref, hg_ref, da_ref, mg_ref):
    x = x_ref[0]
    ms = jnp.mean(x * x, axis=-1, keepdims=True)
    h = x * lax.rsqrt(ms + EPS) * g_ref[...]
    h = (h * (1.0 + sc_ref[0]) + sh_ref[0]).astype(BF16)
    offs = np.cumsum((0,) + PROJ_WIDTHS)

    def proj(k):
        return jnp.dot(h, w_ref[:, int(offs[k]):int(offs[k + 1])], preferred_element_type=F32)

    r = proj(0)
    qk = GLA_HEADS * GLA_DK
    vw = GLA_HEADS * GLA_DV
    gla_ref[0, :, 0:qk] = r[:, 0:qk] * (GLA_DK ** -0.5)
    gla_ref[0, :, qk:2 * qk] = r[:, qk:2 * qk]
    a_hi, a_lo = _split_bf16(r[:, 2 * qk + 2 * vw:])
    z = (jnp.dot(a_hi, wa_ref[0], preferred_element_type=F32) + jnp.dot(a_lo, wa_ref[0], preferred_element_type=F32)
         + jnp.dot(a_hi, wa_ref[1], preferred_element_type=F32) + ba_ref[...])
    gla_ref[0, :, 2 * qk:4 * qk] = jax.nn.log_sigmoid(z) * (1.0 / GLA_TAU)
    gla_ref[0, :, 4 * qk:4 * qk + 2 * vw] = r[:, 2 * qk:2 * qk + 2 * vw]

    hy_ref[0] = proj(1)

    r = proj(2)
    hw = HG_HEADS * HG_DK
    q = r[:, 0:hw]
    hg_ref[0, :, 0:hw] = q * jax.nn.sigmoid(q)
    zf = r[:, hw:3 * hw]
    lb = lb_ref[...]
    hg_ref[0, :, hw:3 * hw] = (1.0 - lb) * jax.nn.sigmoid(-zf)
    hg_ref[0, :, 3 * hw:5 * hw] = jnp.log(jnp.maximum(lb + (1.0 - lb) * jax.nn.sigmoid(zf), F_TINY))
    hg_ref[0, :, 5 * hw:7 * hw] = r[:, 3 * hw:5 * hw]

    da_ref[0] = proj(3)
    mg_ref[0] = proj(4)


def _in_proj(x, g, shift, scale, w_groups, wa, ba, lb, tm):
    B, L, D = x.shape
    nw = w_groups.shape[1]
    const2 = lambda b, i: (0, 0)
    return pl.pallas_call(
        _in_proj_kernel,
        grid=(B, L // tm),
        in_specs=[
            pl.BlockSpec((1, tm, D), lambda b, i: (b, i, 0)),
            pl.BlockSpec((1, D), const2),
            pl.BlockSpec((1, 1, D), lambda b, i: (b, 0, 0)),
            pl.BlockSpec((1, 1, D), lambda b, i: (b, 0, 0)),
            pl.BlockSpec((D, nw), const2, pipeline_mode=pl.Buffered(1)),
            pl.BlockSpec(wa.shape, lambda b, i: (0, 0, 0)),
            pl.BlockSpec(ba.shape, const2),
            pl.BlockSpec(lb.shape, const2),
        ],
        out_specs=[pl.BlockSpec((1, tm, w), lambda b, i: (b, i, 0)) for w in OUT_WIDTHS],
        out_shape=[jax.ShapeDtypeStruct((B, L, w), F32) for w in OUT_WIDTHS],
        compiler_params=pltpu.CompilerParams(
            dimension_semantics=("parallel", "parallel"), vmem_limit_bytes=VMEM_LIMIT),
        name="in_proj",
    )(x, g.reshape(1, D), shift, scale, w_groups, wa, ba, lb)


def _merge_kernel(o0_ref, o1_ref, o2_ref, o3_ref, gate_ref, wb_ref, wo_ref, x_ref, m_ref, out_ref):
    D = D_MODEL
    acc = None
    for i, o_ref in enumerate((o0_ref, o1_ref, o2_ref, o3_ref)):
        t = jnp.dot(o_ref[0].astype(BF16), wb_ref[i], preferred_element_type=F32)
        t = jax.nn.sigmoid(gate_ref[0, :, i * D:(i + 1) * D]) * t
        acc = t if acc is None else acc + t
    mx = jnp.dot(acc.astype(BF16), wo_ref[...], preferred_element_type=F32)
    out_ref[0] = x_ref[0] + m_ref[0] * mx


def _merge_out(outs, gate_cols, w_branch, w_out, x, m, tm):
    B, L, D = x.shape
    return pl.pallas_call(
        _merge_kernel,
        grid=(B, L // tm),
        in_specs=[pl.BlockSpec((1, tm, MIX_W), lambda b, i: (b, i, 0)) for _ in range(N_BRANCH)] + [
            pl.BlockSpec((1, tm, N_BRANCH * D), lambda b, i: (b, i, 0)),
            pl.BlockSpec((N_BRANCH, MIX_W, D), lambda b, i: (0, 0, 0)),
            pl.BlockSpec((D, D), lambda b, i: (0, 0)),
            pl.BlockSpec((1, tm, D), lambda b, i: (b, i, 0)),
            pl.BlockSpec((1, 1, D), lambda b, i: (b, 0, 0)),
        ],
        out_specs=pl.BlockSpec((1, tm, D), lambda b, i: (b, i, 0)),
        out_shape=jax.ShapeDtypeStruct((B, L, D), F32),
        compiler_params=pltpu.CompilerParams(
            dimension_semantics=("parallel", "parallel"), vmem_limit_bytes=VMEM_LIMIT),
        name="merge_out",
    )(*outs, gate_cols, w_branch, w_out, x, m)


def _expert_ffn_kernel(x_ref, g_ref, w1_ref, w3_ref, w2_ref, out_ref, w1_scr, w3_scr, w2_scr):
    @pl.when(pl.program_id(1) == 0)
    def _():
        w1_scr[...] = w1_ref[0].astype(BF16)
        w3_scr[...] = w3_ref[0].astype(BF16)
        w2_scr[...] = w2_ref[0].astype(BF16)

    x = x_ref[0, 0].astype(BF16)
    a = jnp.dot(x, w1_scr[...], preferred_element_type=F32)
    b = jnp.dot(x, w3_scr[...], preferred_element_type=F32)
    h = (a * jax.nn.sigmoid(a) * b).astype(BF16)
    y = jnp.dot(h, w2_scr[...], preferred_element_type=F32)
    out_ref[0, 0] = y * g_ref[0, 0]


def _expert_ffn(xg, g, w1, w3, w2):
    B, E, cap, D = xg.shape
    F = w1.shape[-1]
    return pl.pallas_call(
        _expert_ffn_kernel,
        grid=(E, B),
        in_specs=[
            pl.BlockSpec((1, 1, cap, D), lambda e, b: (b, e, 0, 0)),
            pl.BlockSpec((1, 1, cap, 1), lambda e, b: (b, e, 0, 0)),
            pl.BlockSpec((1, D, F), lambda e, b: (e, 0, 0)),
            pl.BlockSpec((1, D, F), lambda e, b: (e, 0, 0)),
            pl.BlockSpec((1, F, D), lambda e, b: (e, 0, 0)),
        ],
        out_specs=pl.BlockSpec((1, 1, cap, D), lambda e, b: (b, e, 0, 0)),
        out_shape=jax.ShapeDtypeStruct((B, E, cap, D), F32),
        scratch_shapes=[pltpu.VMEM((D, F), BF16), pltpu.VMEM((D, F), BF16), pltpu.VMEM((F, D), BF16)],
        compiler_params=pltpu.CompilerParams(
            dimension_semantics=("parallel", "arbitrary"), vmem_limit_bytes=VMEM_LIMIT),
        name="expert_ffn",
    )(xg, g, w1, w3, w2)


def _recur_masks(C):
    t = np.arange(C)[:, None]
    s = np.tile(np.arange(C), N_HEADS)[None, :]
    ms = []
    n = 2
    while n < C:
        ms.append((t // n == s // n).astype(np.float32))
        n *= 2
    ms.append((t == s).astype(np.float32))
    return jnp.asarray(np.stack(ms))


def _recur_kernel(*refs, rev, C, nsub, dk, dv, finish):
    if finish:
        q_ref, k_ref, v_ref, g_ref, s0_ref, m_ref, prev_ref, gate_ref, gn_ref, o_ref, sfin_ref, s_scr = refs
    else:
        q_ref, k_ref, v_ref, g_ref, s0_ref, m_ref, o_ref, sfin_ref, s_scr = refs
    HK = N_HEADS * dk
    HV = N_HEADS * dv
    i = pl.program_id(1)
    if finish:
        head_mean = jnp.where(lax.broadcasted_iota(jnp.int32, (HV, HV), 0) // dv
                              == lax.broadcasted_iota(jnp.int32, (HV, HV), 1) // dv, 1.0 / dv, 0.0).astype(BF16)

    @pl.when(i == 0)
    def _():
        s_scr[...] = s0_ref[0]

    row = lax.broadcasted_iota(jnp.int32, (C, HK), 0)
    pos = (C - 1 - row) if rev else row
    lane_head_k = lax.broadcasted_iota(jnp.int32, (C, HK), 1) // dk
    lane_head_v = lax.broadcasted_iota(jnp.int32, (C, HV), 1) // dv
    bd = (lax.broadcasted_iota(jnp.int32, (HV, HK), 0) // dv
          == lax.broadcasted_iota(jnp.int32, (HV, HK), 1) // dk)

    def ahead(x, s):
        return pltpu.roll(x, (C - s) if rev else s, axis=0)

    def behind(x, s):
        return pltpu.roll(x, s if rev else (C - s), axis=0)

    def stack_heads(x, lane_head):
        return jnp.concatenate([jnp.where(lane_head == h, x, 0.0) for h in range(N_HEADS)], axis=0).astype(BF16)

    def nt(a, b):
        return lax.dot_general(a, b, (((1,), (1,)), ((), ())), preferred_element_type=F32)

    def chunk(j, carry):
        c = (nsub - 1 - j) if rev else j
        sl = pl.ds(pl.multiple_of(c * C, C), C)
        q = q_ref[0, sl, :]
        k = k_ref[0, sl, :]
        v = v_ref[0, sl, :]
        g = g_ref[0, sl, :]
        b = g
        s = 1
        while s < C:
            b = b + jnp.where(pos >= s, ahead(b, s), 0.0)
            s *= 2
        a = nt(q.astype(BF16), stack_heads(k, lane_head_k)) * m_ref[m_ref.shape[0] - 1]
        m, lvl = 1, 0
        while m < C:
            n = 2 * m
            off = pos % n
            rq = jnp.where(off == m, ahead(b, 1), 0.0)
            rk = jnp.where(off == m - 1, b, 0.0)
            s = 1
            while s < m:
                rq = rq + ahead(rq, s)
                rk = rk + behind(rk, s)
                s *= 2
            late = off >= m
            qt = jnp.where(late, q * jnp.exp(jnp.where(late, b - rq, 0.0)), 0.0)
            kt = jnp.where(late, 0.0, k * jnp.exp(jnp.where(late, 0.0, rk - b)))
            al = nt(qt.astype(BF16), stack_heads(kt, lane_head_k))
            a = a + (al if n == C else al * m_ref[lvl])
            m, lvl = n, lvl + 1
        st = s_scr[...]
        o = jnp.dot(a.astype(BF16), stack_heads(v, lane_head_v), preferred_element_type=F32)
        o = o + nt((q * jnp.exp(b)).astype(BF16), st.astype(BF16))
        if finish:
            o = o + prev_ref[0, sl, :]
            sq_hi, sq_lo = _split_bf16(o * o)
            ms = (jnp.dot(sq_hi, head_mean, preferred_element_type=F32)
                  + jnp.dot(sq_lo, head_mean, preferred_element_type=F32))
            gate = gate_ref[0, sl, :]
            o = o * lax.rsqrt(ms + EPS) * gn_ref[...] * (gate * jax.nn.sigmoid(gate))
        o_ref[0, sl, :] = o
        b_end = b[0:1, :] if rev else b[C - 1:C, :]
        kend = (k * jnp.exp(b_end - b)).astype(BF16)
        upd = lax.dot_general(v.astype(BF16), kend, (((0,), (0,)), ((), ())), preferred_element_type=F32)
        s_scr[...] = st * jnp.exp(b_end) + jnp.where(bd, upd, 0.0)
        return carry

    lax.fori_loop(0, nsub, chunk, 0)

    @pl.when(i == pl.num_programs(1) - 1)
    def _():
        sfin_ref[0] = s_scr[...]


def _recurrence(pack, cols, s0, prev, gnorm, *, rev, dk, dv, tb):
    B, L, _ = pack.shape
    HK, HV = N_HEADS * dk, N_HEADS * dv
    C = RC
    nblk = L // tb
    masks = _recur_masks(C)
    cq, ck, cg, cv, cgate = cols

    def tok(col):
        return (lambda b, i: (b, nblk - 1 - i, col)) if rev else (lambda b, i: (b, i, col))

    in_specs = [
        pl.BlockSpec((1, tb, HK), tok(cq)), pl.BlockSpec((1, tb, HK), tok(ck)),
        pl.BlockSpec((1, tb, HV), tok(cv)), pl.BlockSpec((1, tb, HK), tok(cg)),
        pl.BlockSpec((1, HV, HK), lambda b, i: (b, 0, 0)),
        pl.BlockSpec(masks.shape, lambda b, i: (0, 0, 0)),
    ]
    args = [pack, pack, pack, pack, s0, masks]
    if prev is not None:
        in_specs += [pl.BlockSpec((1, tb, HV), tok(0)), pl.BlockSpec((1, tb, HV), tok(cgate)),
                     pl.BlockSpec((1, HV), lambda b, i: (0, 0))]
        args += [prev, pack, gnorm]
    kern = functools.partial(_recur_kernel, rev=rev, C=C, nsub=tb // C, dk=dk, dv=dv, finish=prev is not None)
    return pl.pallas_call(
        kern,
        grid=(B, nblk),
        in_specs=in_specs,
        out_specs=[pl.BlockSpec((1, tb, HV), tok(0)), pl.BlockSpec((1, HV, HK), lambda b, i: (b, 0, 0))],
        out_shape=[jax.ShapeDtypeStruct((B, L, HV), F32), jax.ShapeDtypeStruct((B, HV, HK), F32)],
        scratch_shapes=[pltpu.VMEM((HV, HK), F32)],
        compiler_params=pltpu.CompilerParams(
            dimension_semantics=("parallel", "arbitrary"), vmem_limit_bytes=VMEM_LIMIT),
        name="recur_rev" if rev else "recur_fwd",
    )(*args)


def _bidir(pack_c, pack_l, cols_f, cols_b, gnorm, *, dk, dv):
    B, Lc, _ = pack_c.shape
    s0 = jnp.zeros((B, N_HEADS * dv, N_HEADS * dk), F32)
    gn = jnp.tile(gnorm.astype(F32), N_HEADS).reshape(1, N_HEADS * dv)
    kw = dict(dk=dk, dv=dv)
    oc_f, sc_f = _recurrence(pack_c, cols_f, s0, None, None, rev=False, tb=Lc, **kw)
    ol_f, _ = _recurrence(pack_l, cols_f, sc_f, None, None, rev=False, tb=256, **kw)
    oc, sc_b = _recurrence(pack_c, cols_b, s0, oc_f, gn, rev=True, tb=Lc, **kw)
    ol, _ = _recurrence(pack_l, cols_b, sc_b, ol_f, gn, rev=True, tb=256, **kw)
    return oc, ol


def _attn_kernel(lam_ref, qt_ref, k_ref, vt_ref, g_ref, o_ref, qm_scr, s_scr, p_scr, m_scr, a_scr, acc_scr, *, tk):
    W = qt_ref.shape[1]
    tq = qt_ref.shape[2]
    nkb = k_ref.shape[1] // tk
    qt = qt_ref[0]
    row_pair = lax.broadcasted_iota(jnp.int32, (W, tq), 0) // DA_DK
    for j in range(N_QK):
        qm_scr[j] = jnp.where(row_pair == j, qt, 0.0).astype(BF16)
    m_scr[...] = jnp.full(m_scr.shape, -jnp.inf, F32)
    acc_scr[...] = jnp.zeros(acc_scr.shape, F32)

    def body(kb, carry):
        ks = pl.ds(pl.multiple_of(kb * tk, tk), tk)
        kblk = k_ref[0, ks, :]
        for j in range(N_QK):
            s_scr[j] = jnp.dot(kblk, qm_scr[j], preferred_element_type=F32)
        for j in range(N_QK):
            s = s_scr[j]
            m_old = m_scr[j]
            m_new = jnp.maximum(m_old, jnp.max(s, axis=0, keepdims=True))
            a_scr[j] = jnp.exp2(m_old - m_new)
            p_scr[j] = jnp.exp2(s - m_new).astype(BF16)
            m_scr[j] = m_new
        for j in range(N_QK):
            pv = jnp.dot(vt_ref[0, j // 2, :, ks], p_scr[j], preferred_element_type=F32)
            acc_scr[j] = a_scr[j] * acc_scr[j] + pv
        return carry

    lax.fori_loop(0, nkb, body, 0)
    lam = lam_ref[0]
    outs = []
    for h in range(DA_HEADS):
        a1, a2 = acc_scr[2 * h], acc_scr[2 * h + 1]
        o = a1[:DA_DV] / a1[DA_DV:DA_DV + 1] - lam * (a2[:DA_DV] / a2[DA_DV:DA_DV + 1])
        ms = jnp.mean(o * o, axis=0, keepdims=True)
        outs.append(o * lax.rsqrt(ms + EPS) * g_ref[...])
    o_ref[0] = jnp.concatenate(outs, axis=0).T


def _diff_attention(qt, k, vt, lam, gcol, *, tq, tk):
    B, W, Lq = qt.shape
    Lk = k.shape[1]
    HV = DA_HEADS * DA_DV
    return pl.pallas_call(
        functools.partial(_attn_kernel, tk=tk),
        grid=(B, Lq // tq),
        in_specs=[
            pl.BlockSpec(memory_space=pltpu.SMEM),
            pl.BlockSpec((1, W, tq), lambda b, i: (b, 0, i)),
            pl.BlockSpec((1, Lk, W), lambda b, i: (b, 0, 0)),
            pl.BlockSpec((1, DA_HEADS, VROWS, Lk), lambda b, i: (b, 0, 0, 0)),
            pl.BlockSpec((DA_DV, 1), lambda b, i: (0, 0)),
        ],
        out_specs=pl.BlockSpec((1, tq, HV), lambda b, i: (b, i, 0)),
        out_shape=jax.ShapeDtypeStruct((B, Lq, HV), F32),
        scratch_shapes=[
            pltpu.VMEM((N_QK, W, tq), BF16),
            pltpu.VMEM((N_QK, tk, tq), F32),
            pltpu.VMEM((N_QK, tk, tq), BF16),
            pltpu.VMEM((N_QK, 1, tq), F32),
            pltpu.VMEM((N_QK, 1, tq), F32),
            pltpu.VMEM((N_QK, VROWS, tq), F32),
        ],
        compiler_params=pltpu.CompilerParams(
            dimension_semantics=("parallel", "parallel"), vmem_limit_bytes=VMEM_LIMIT),
        name="diff_attn",
    )(lam, qt, k, vt, gcol)


def _hyena_kernel(bias_ref, z_ref, gate_ref, hp_ref, o_ref, acc_scr, *, nb, nbatch):
    c = pl.program_id(0)
    half = nb // 2
    rows = nb * nbatch
    z = z_ref[0].reshape(rows, HB)
    zb = z.astype(BF16)
    hrow = hp_ref[0]
    acc_scr[...] = jnp.zeros(acc_scr.shape, F32)
    for off in range(-half, half + 1):
        lo = (off + half) * HB
        r = jnp.broadcast_to(hrow[:, lo:lo + 2 * HB], (HB, 2 * HB))
        w = pltpu.roll(r, 1, 1, stride=1, stride_axis=0)[:, HB:].astype(BF16)
        s0, s1 = max(0, -off), nb - max(0, off)
        if s1 <= s0:
            continue
        src = slice(s0 * nbatch, s1 * nbatch)
        dst = slice((s0 + off) * nbatch, (s1 + off) * nbatch)
        acc_scr[dst, :] += jnp.dot(zb[src], w, preferred_element_type=F32)
    y = acc_scr[...] + bias_ref[c] * z
    o_ref[0] = (gate_ref[0].reshape(rows, HB) * y).reshape(nb, nbatch, HB)


def _hyena_order(z, gate, hp, bias):
    C, nb, B, _ = z.shape
    blk = pl.BlockSpec((1, nb, B, HB), lambda c: (c, 0, 0, 0))
    return pl.pallas_call(
        functools.partial(_hyena_kernel, nb=nb, nbatch=B),
        grid=(C,),
        in_specs=[
            pl.BlockSpec(memory_space=pltpu.SMEM),
            blk, blk,
            pl.BlockSpec((1, 1, hp.shape[-1]), lambda c: (c, 0, 0)),
        ],
        out_specs=blk,
        out_shape=jax.ShapeDtypeStruct(z.shape, F32),
        scratch_shapes=[pltpu.VMEM((nb * B, HB), F32)],
        compiler_params=pltpu.CompilerParams(
            dimension_semantics=("parallel",), vmem_limit_bytes=VMEM_LIMIT),
        name="hyena_conv",
    )(bias, z, gate, hp)


def _rms_norm(x, g):
    xf = x.astype(F32)
    y = xf * lax.rsqrt(jnp.mean(xf * xf, axis=-1, keepdims=True) + EPS)
    return (y * g.astype(F32)).astype(x.dtype)


def _modulate(x, g, shift, scale):
    return _rms_norm(x, g) * (1 + scale) + shift


def _short_conv(u, w):
    up = jnp.pad(u, ((0, 0), (1, 1), (0, 0)))
    return up[:, :-2] * w[0] + up[:, 1:-1] * w[1] + up[:, 2:] * w[2]


def _gate_weights(wa2, ba):
    qk = GLA_HEADS * GLA_DK
    w = jnp.zeros((LANES, 2 * qk), F32)
    w = w.at[:GLA_RANK, :qk].set(wa2[0].astype(F32)).at[GLA_RANK:2 * GLA_RANK, qk:].set(wa2[1].astype(F32))
    hi = w.astype(BF16)
    lo = (w - hi.astype(F32)).astype(BF16)
    return jnp.stack([hi, lo]), ba.astype(F32).reshape(1, 2 * qk)


def _hyena_filters(L, p):
    j = jnp.arange(L, dtype=F32)
    t = j / max(L - 1, 1)
    w = 2 * math.pi * j / L
    f = jnp.linspace(1e-4, HY_BANDS - 1, HY_BANDS, dtype=F32)
    feats = jnp.concatenate([t[:, None], jnp.cos(w[:, None] * f), -jnp.sin(w[:, None] * f)], axis=-1)
    h = jnp.sin(p['hy_freq'][0] * (feats @ p['hy_w1'] + p['hy_b1']))
    h = jnp.sin(p['hy_freq'][1] * (h @ p['hy_w2'] + p['hy_b2']))
    h = (h @ p['hy_w3']).astype(F32)
    dist = jnp.abs(j - L // 2) / (L // 2)
    h = h * (jnp.exp(-dist[:, None] * jnp.abs(p['hy_decay'].astype(F32))) + HY_SHIFT)
    h = h / jnp.sum(jnp.abs(h), axis=0, keepdims=True)
    return h.reshape(L, HY_ORDER, HY_WIDTH)


def _to_blocks(a):
    B, L, C = a.shape
    return a.reshape(B, L // HB, HB, C).transpose(3, 1, 0, 2)


def _from_blocks(a):
    C, nb, B, _ = a.shape
    return a.transpose(2, 1, 3, 0).reshape(B, nb * HB, C)


def _pad_filter(h):
    L = h.shape[0]
    nb = L // HB
    left = (HB - 1) - (L // 2 - (nb // 2) * HB)
    return jnp.pad(h.T, ((0, 0), (left, (nb + 2) * HB - L - left)))[:, None, :]


def _hyena_branch(u3, p):
    L = u3.shape[1]
    v, x1, x2 = jnp.split(_short_conv(u3, p['hy_conv_w']), 1 + HY_ORDER, axis=-1)
    h = _hyena_filters(L, p)
    z = _to_blocks(v)
    for o, gate in enumerate((x1, x2)):
        z = _hyena_order(z, _to_blocks(gate), _pad_filter(h[:, o]), p['hy_bias'][o])
    return _from_blocks(z)


def _rope_tables(L):
    quarter = DA_DK // 4
    freqs = ROPE_BASE ** (-jnp.arange(quarter, dtype=F32) / quarter)
    row = (jnp.arange(L) // GRID_W).astype(F32)[:, None] * freqs
    col = (jnp.arange(L) % GRID_W).astype(F32)[:, None] * freqs
    return jnp.cos(row), jnp.sin(row), jnp.cos(col), jnp.sin(col)


def _da_prep(t, g, tables):
    B, L, W = t.shape
    t = _rms_norm(t.reshape(B, L, N_QK, DA_DK), g)
    if tables is not None:
        cr, sr, cc, sc = [a[None, :, None, :] for a in tables]
        e = DA_DK // 4
        a1, a2, b1, b2 = t[..., :e], t[..., e:2 * e], t[..., 2 * e:3 * e], t[..., 3 * e:]
        t = jnp.concatenate([a1 * cr - a2 * sr, a1 * sr + a2 * cr, b1 * cc - b2 * sc, b1 * sc + b2 * cc], axis=-1)
    return t.reshape(B, L, W)


def _value_rows(v):
    B, Lk, _ = v.shape
    vt = v.reshape(B, Lk, DA_HEADS, DA_DV).transpose(0, 2, 3, 1)
    extra = jnp.concatenate([jnp.ones((B, DA_HEADS, 1, Lk), v.dtype),
                             jnp.zeros((B, DA_HEADS, VROWS - DA_DV - 1, Lk), v.dtype)], axis=2)
    return jnp.concatenate([vt, extra], axis=2).astype(BF16)


def _da_parts(da):
    w = DA_HEADS * 2 * DA_DK
    return da[..., :w], da[..., w:2 * w], da[..., 2 * w:]


def _group_weights(w_in):
    cols = []
    for names, width in PROJ_GROUPS:
        used = 0
        for nm in names:
            i = IN_NAMES.index(nm)
            cols.append(w_in[:, IN_OFFSETS[i]:IN_OFFSETS[i] + IN_WIDTHS[i]])
            used += IN_WIDTHS[i]
        if width > used:
            cols.append(jnp.zeros((w_in.shape[0], width - used), w_in.dtype))
    return jnp.concatenate(cols, axis=1).astype(BF16)


GLA_COLS_F, GLA_COLS_B = (0, 1, 2, 2, 3), (0, 1, 3, 2, 3)
HG_COLS_F, HG_COLS_B = (0, 1, 3, 5, 6), (0, 2, 4, 5, 6)


def _token_mixer(gc, gx, p, lam_init, last):
    gla_c, gla_x = _bidir(gc[0], gx[0], GLA_COLS_F, GLA_COLS_B, p['gla_norm_g'], dk=GLA_DK, dv=GLA_DV)
    hg_c, hg_x = _bidir(gc[2], gx[2], HG_COLS_F, HG_COLS_B, p['hg_norm_g'], dk=HG_DK, dv=HG_DV)
    (dqc, dkc, dvc), (dqx, dkx, dvx) = _da_parts(gc[3]), _da_parts(gx[3])
    tables = _rope_tables(dqx.shape[1])
    scale = DA_DK ** -0.5 * LOG2E
    qx = (_da_prep(dqx, p['da_qnorm_g'], tables) * scale).transpose(0, 2, 1)
    kx = _da_prep(dkx, p['da_knorm_g'], tables)
    kc = _da_prep(dkc, p['da_knorm_g'], None)
    k_all = jnp.concatenate([kc, kx], axis=1).astype(BF16)
    vt_all = _value_rows(jnp.concatenate([dvc, dvx], axis=1))
    lp = p['da_lam'].astype(F32)
    lam = (jnp.exp(jnp.sum(lp[0] * lp[1])) - jnp.exp(jnp.sum(lp[2] * lp[3])) + lam_init).reshape(1)
    gcol = (p['da_norm_g'].astype(F32) * (1 - lam_init)).reshape(DA_DV, 1)
    da_x = _diff_attention(qx, k_all, vt_all, lam, gcol, tq=256, tk=256)
    outs_x = (gla_x, _hyena_branch(gx[1], p), hg_x, da_x)
    if last:
        return None, outs_x
    Lc = kc.shape[1]
    qc = (_da_prep(dqc, p['da_qnorm_g'], None) * scale).transpose(0, 2, 1)
    da_c = _diff_attention(qc, k_all[:, :Lc], vt_all[..., :Lc], lam, gcol, tq=Lc, tk=Lc)
    return (gla_c, _hyena_branch(gc[1], p), hg_c, da_c), outs_x


def _expert_choice_moe(h, router, w1, w3, w2):
    B, L, D = h.shape
    cap = CAPACITY_FACTOR * L // N_EXPERTS
    aff = jax.nn.softmax((h @ router).astype(F32), axis=-1)
    g, idx = lax.top_k(aff.transpose(0, 2, 1), cap)
    xg = jax.vmap(lambda hb, ib: hb[ib])(h, idx)
    y = _expert_ffn(xg, g[..., None], w1, w3, w2)
    return jax.vmap(lambda yb, ib: jnp.zeros((L, D), h.dtype).at[ib.reshape(-1)].add(yb.reshape(-1, D)))(y, idx)


def kernel(x, c, ctx, c_ctx, ada_w, ada_b, norm1_g, norm2_g, w_in, gla_wa2, gla_ba, gla_norm_g,
           hy_conv_w, hy_w1, hy_b1, hy_w2, hy_b2, hy_w3, hy_freq, hy_decay, hy_bias,
           hg_lower, hg_norm_g, da_qnorm_g, da_knorm_g, da_lam, da_norm_g, w_branch, w_out,
           moe_router, moe_w1, moe_w3, moe_w2):
    B, L, D = x.shape
    P = jax.nn.softmax(hg_lower.astype(F32), axis=0)
    lower = jnp.cumsum(P, axis=0) - P[0]
    sc = jax.nn.silu(c)
    scc = jax.nn.silu(c_ctx)
    xc, xx = ctx, x
    for l in range(DEPTH):
        last = l == DEPTH - 1
        lam_init = 0.8 - 0.6 * math.exp(-0.3 * l)
        p = {'gla_wa2': gla_wa2[l], 'gla_ba': gla_ba[l], 'gla_norm_g': gla_norm_g[l],
             'hy_conv_w': hy_conv_w[l], 'hy_w1': hy_w1[l], 'hy_b1': hy_b1[l], 'hy_w2': hy_w2[l], 'hy_b2': hy_b2[l],
             'hy_w3': hy_w3[l], 'hy_freq': hy_freq[l], 'hy_decay': hy_decay[l], 'hy_bias': hy_bias[l],
             'hg_norm_g': hg_norm_g[l], 'da_qnorm_g': da_qnorm_g[l], 'da_knorm_g': da_knorm_g[l],
             'da_lam': da_lam[l], 'da_norm_g': da_norm_g[l]}
        mod_x = jnp.split((sc @ ada_w[l] + ada_b[l])[:, None, :], ADA_CHUNKS, axis=-1)
        mod_c1 = jnp.split((scc @ ada_w[l] + ada_b[l])[None, None, :], ADA_CHUNKS, axis=-1)
        mod_c = [jnp.broadcast_to(m, (B, 1, D)) for m in mod_c1]
        wg = _group_weights(w_in[l])
        wa, ba = _gate_weights(gla_wa2[l], gla_ba[l])
        lb = lower[l].reshape(1, 2 * HG_HEADS * HG_DK)
        wb = w_branch[l].astype(BF16)
        wo = w_out[l].astype(BF16)
        w1, w3, w2 = moe_w1[l], moe_w3[l], moe_w2[l]
        gx = _in_proj(xx, norm1_g[l], mod_x[0], mod_x[1], wg, wa, ba, lb, tm=256)
        gc = _in_proj(xc, norm1_g[l], mod_c[0], mod_c[1], wg, wa, ba, lb, tm=256)
        outs_c, outs_x = _token_mixer(gc, gx, p, lam_init, last)
        xx = _merge_out(outs_x, gx[4], wb, wo, xx, mod_x[2], tm=256)
        hx = _modulate(xx, norm2_g[l], mod_x[3], mod_x[4])
        xx = xx + mod_x[5] * _expert_choice_moe(hx, moe_router[l], w1, w3, w2)
        if not last:
            xc = _merge_out(outs_c, gc[4], wb, wo, xc, mod_c[2], tm=256)
            hc = _modulate(xc, norm2_g[l], mod_c[3], mod_c[4])
            xc = xc + mod_c[5] * _expert_choice_moe(hc, moe_router[l], w1, w3, w2)
    return xx
```

```python
import functools
import math

import jax
import jax.numpy as jnp
import numpy as np
from jax import lax
from jax.experimental import pallas as pl
from jax.experimental.pallas import tpu as pltpu

D_MODEL = 1024
DEPTH = 2
GRID_W = 64
N_BRANCH = 4
MIX_W = 256
GLA_HEADS = 4
GLA_DK = 32
GLA_DV = 64
GLA_RANK = 16
GLA_TAU = 16.0
HY_WIDTH = 256
HY_ORDER = 2
HY_BANDS = 16
HY_SHIFT = 0.05
HG_HEADS = 4
HG_DK = 64
HG_DV = 64
DA_HEADS = 4
DA_DK = 32
DA_DV = 64
ROPE_BASE = 10000.0
N_EXPERTS = 16
EXPERT_FF = 1024
CAPACITY_FACTOR = 2
ADA_CHUNKS = 6
EPS = 1e-6
F_TINY = 1e-20

IN_NAMES = ('gla_q', 'gla_k', 'gla_v', 'gla_af', 'gla_ab', 'gla_g', 'hy',
            'hg_q', 'hg_ff', 'hg_fb', 'hg_i', 'hg_g', 'da_q', 'da_k', 'da_v', 'merge')
IN_WIDTHS = (GLA_HEADS * GLA_DK, GLA_HEADS * GLA_DK, GLA_HEADS * GLA_DV, GLA_RANK, GLA_RANK, GLA_HEADS * GLA_DV,
             (1 + HY_ORDER) * HY_WIDTH,
             HG_HEADS * HG_DK, HG_HEADS * HG_DK, HG_HEADS * HG_DK, HG_HEADS * HG_DV, HG_HEADS * HG_DV,
             DA_HEADS * 2 * DA_DK, DA_HEADS * 2 * DA_DK, DA_HEADS * DA_DV,
             N_BRANCH * D_MODEL)
IN_OFFSETS = tuple(int(v) for v in np.cumsum((0,) + IN_WIDTHS)[:-1])

PROJ_GROUPS = (
    (('gla_q', 'gla_k', 'gla_v', 'gla_g', 'gla_af', 'gla_ab'), 896),
    (('hy',), 768),
    (('hg_q', 'hg_ff', 'hg_fb', 'hg_i', 'hg_g'), 1280),
    (('da_q', 'da_k', 'da_v'), 768),
    (('merge',), 4096),
)
PROJ_WIDTHS = tuple(w for _, w in PROJ_GROUPS)
LANES = 128
OUT_WIDTHS = (4 * GLA_HEADS * GLA_DK + 2 * GLA_HEADS * GLA_DV, PROJ_WIDTHS[1],
              5 * HG_HEADS * HG_DK + 2 * HG_HEADS * HG_DV, PROJ_WIDTHS[3], PROJ_WIDTHS[4])
VMEM_LIMIT = 56 * 1024 * 1024

BF16 = jnp.bfloat16
F32 = jnp.float32
N_HEADS = 4
RC = 64
N_QK = 2 * DA_HEADS
VROWS = DA_DV + 8
LOG2E = 1.4426950408889634
HB = 256


def _split_bf16(a):
    hi = a.astype(BF16)
    return hi, (a - hi.astype(F32)).astype(BF16)


def _in_proj_kernel(x_ref, g_ref, sh_ref, sc_ref, w_ref, wa_ref, ba_ref, lb_ref,
                    gla_ref, hy_ref, hg_ref, da_ref, mg_ref):
    x = x_ref[0]
    ms = jnp.mean(x * x, axis=-1, keepdims=True)
    h = x * lax.rsqrt(ms + EPS) * g_ref[...]
    h = (h * (1.0 + sc_ref[0]) + sh_ref[0]).astype(BF16)
    offs = np.cumsum((0,) + PROJ_WIDTHS)

    def proj(k):
        return jnp.dot(h, w_ref[:, int(offs[k]):int(offs[k + 1])], preferred_element_type=F32)

    r = proj(0)
    qk = GLA_HEADS * GLA_DK
    vw = GLA_HEADS * GLA_DV
    gla_ref[0, :, 0:qk] = r[:, 0:qk] * (GLA_DK ** -0.5)
    gla_ref[0, :, qk:2 * qk] = r[:, qk:2 * qk]
    a_hi, a_lo = _split_bf16(r[:, 2 * qk + 2 * vw:])
    z = (jnp.dot(a_hi, wa_ref[0], preferred_element_type=F32) + jnp.dot(a_lo, wa_ref[0], preferred_element_type=F32)
         + jnp.dot(a_hi, wa_ref[1], preferred_element_type=F32) + ba_ref[...])
    gla_ref[0, :, 2 * qk:4 * qk] = jax.nn.log_sigmoid(z) * (1.0 / GLA_TAU)
    gla_ref[0, :, 4 * qk:4 * qk + 2 * vw] = r[:, 2 * qk:2 * qk + 2 * vw]

    hy_ref[0] = proj(1)

    r = proj(2)
    hw = HG_HEADS * HG_DK
    q = r[:, 0:hw]
    hg_ref[0, :, 0:hw] = q * jax.nn.sigmoid(q)
    zf = r[:, hw:3 * hw]
    lb = lb_ref[...]
    hg_ref[0, :, hw:3 * hw] = (1.0 - lb) * jax.nn.sigmoid(-zf)
    hg_ref[0, :, 3 * hw:5 * hw] = jnp.log(jnp.maximum(lb + (1.0 - lb) * jax.nn.sigmoid(zf), F_TINY))
    hg_ref[0, :, 5 * hw:7 * hw] = r[:, 3 * hw:5 * hw]

    da_ref[0] = proj(3)
    mg_ref[0] = proj(4)


def _in_proj(x, g, shift, scale, w_groups, wa, ba, lb, tm):
    B, L, D = x.shape
    nw = w_groups.shape[1]
    const2 = lambda b, i: (0, 0)
    return pl.pallas_call(
        _in_proj_kernel,
        grid=(B, L // tm),
        in_specs=[
            pl.BlockSpec((1, tm, D), lambda b, i: (b, i, 0)),
            pl.BlockSpec((1, D), const2),
            pl.BlockSpec((1, 1, D), lambda b, i: (b, 0, 0)),
            pl.BlockSpec((1, 1, D), lambda b, i: (b, 0, 0)),
            pl.BlockSpec((D, nw), const2, pipeline_mode=pl.Buffered(1)),
            pl.BlockSpec(wa.shape, lambda b, i: (0, 0, 0)),
            pl.BlockSpec(ba.shape, const2),
            pl.BlockSpec(lb.shape, const2),
        ],
        out_specs=[pl.BlockSpec((1, tm, w), lambda b, i: (b, i, 0)) for w in OUT_WIDTHS],
        out_shape=[jax.ShapeDtypeStruct((B, L, w), F32) for w in OUT_WIDTHS],
        compiler_params=pltpu.CompilerParams(
            dimension_semantics=("parallel", "parallel"), vmem_limit_bytes=VMEM_LIMIT),
        name="in_proj",
    )(x, g.reshape(1, D), shift, scale, w_groups, wa, ba, lb)


def _merge_kernel(o0_ref, o1_ref, o2_ref, o3_ref, gate_ref, wb_ref, wo_ref, x_ref, m_ref, out_ref):
    D = D_MODEL
    acc = None
    for i, o_ref in enumerate((o0_ref, o1_ref, o2_ref, o3_ref)):
        t = jnp.dot(o_ref[0].astype(BF16), wb_ref[i], preferred_element_type=F32)
        t = jax.nn.sigmoid(gate_ref[0, :, i * D:(i + 1) * D]) * t
        acc = t if acc is None else acc + t
    mx = jnp.dot(acc.astype(BF16), wo_ref[...], preferred_element_type=F32)
    out_ref[0] = x_ref[0] + m_ref[0] * mx


def _merge_out(outs, gate_cols, w_branch, w_out, x, m, tm):
    B, L, D = x.shape
    return pl.pallas_call(
        _merge_kernel,
        grid=(B, L // tm),
        in_specs=[pl.BlockSpec((1, tm, MIX_W), lambda b, i: (b, i, 0)) for _ in range(N_BRANCH)] + [
            pl.BlockSpec((1, tm, N_BRANCH * D), lambda b, i: (b, i, 0)),
            pl.BlockSpec((N_BRANCH, MIX_W, D), lambda b, i: (0, 0, 0)),
            pl.BlockSpec((D, D), lambda b, i: (0, 0)),
            pl.BlockSpec((1, tm, D), lambda b, i: (b, i, 0)),
            pl.BlockSpec((1, 1, D), lambda b, i: (b, 0, 0)),
        ],
        out_specs=pl.BlockSpec((1, tm, D), lambda b, i: (b, i, 0)),
        out_shape=jax.ShapeDtypeStruct((B, L, D), F32),
        compiler_params=pltpu.CompilerParams(
            dimension_semantics=("parallel", "parallel"), vmem_limit_bytes=VMEM_LIMIT),
        name="merge_out",
    )(*outs, gate_cols, w_branch, w_out, x, m)


def _expert_ffn_kernel(x_ref, g_ref, w1_ref, w3_ref, w2_ref, out_ref, w1_scr, w3_scr, w2_scr):
    @pl.when(pl.program_id(1) == 0)
    def _():
        w1_scr[...] = w1_ref[0].astype(BF16)
        w3_scr[...] = w3_ref[0].astype(BF16)
        w2_scr[...] = w2_ref[0].astype(BF16)

    x = x_ref[0, 0].astype(BF16)
    a = jnp.dot(x, w1_scr[...], preferred_element_type=F32)
    b = jnp.dot(x, w3_scr[...], preferred_element_type=F32)
    h = (a * jax.nn.sigmoid(a) * b).astype(BF16)
    y = jnp.dot(h, w2_scr[...], preferred_element_type=F32)
    out_ref[0, 0] = (y * g_ref[0, 0]).astype(out_ref.dtype)


def _combine_kernel(idx_ref, y_ref, x_ref, m_ref, out_ref, *, ks):
    tt = out_ref.shape[1]
    S = y_ref.shape[1]
    base = pl.program_id(1) * tt
    acc = None
    for s0 in range(0, S, ks):
        tok = lax.broadcasted_iota(jnp.int32, (tt, ks), 0) + base
        onehot = jnp.where(tok == idx_ref[0, :, s0:s0 + ks], 1.0, 0.0).astype(BF16)
        part = jnp.dot(onehot, y_ref[0, s0:s0 + ks, :], preferred_element_type=F32)
        acc = part if acc is None else acc + part
    out_ref[0] = x_ref[0] + m_ref[0] * acc


def _combine(idx, y, x, m, tt):
    B, L, D = x.shape
    S = y.shape[1]
    return pl.pallas_call(
        functools.partial(_combine_kernel, ks=min(S, 2048)),
        grid=(B, L // tt),
        in_specs=[
            pl.BlockSpec((1, 1, S), lambda b, i: (b, 0, 0)),
            pl.BlockSpec((1, S, D), lambda b, i: (b, 0, 0)),
            pl.BlockSpec((1, tt, D), lambda b, i: (b, i, 0)),
            pl.BlockSpec((1, 1, D), lambda b, i: (b, 0, 0)),
        ],
        out_specs=pl.BlockSpec((1, tt, D), lambda b, i: (b, i, 0)),
        out_shape=jax.ShapeDtypeStruct((B, L, D), F32),
        compiler_params=pltpu.CompilerParams(
            dimension_semantics=("parallel", "parallel"), vmem_limit_bytes=VMEM_LIMIT),
        name="moe_combine",
    )(idx, y, x, m)


def _expert_ffn(xg, g, w1, w3, w2):
    B, E, cap, D = xg.shape
    F = w1.shape[-1]
    return pl.pallas_call(
        _expert_ffn_kernel,
        grid=(E, B),
        in_specs=[
            pl.BlockSpec((1, 1, cap, D), lambda e, b: (b, e, 0, 0)),
            pl.BlockSpec((1, 1, cap, 1), lambda e, b: (b, e, 0, 0)),
            pl.BlockSpec((1, D, F), lambda e, b: (e, 0, 0)),
            pl.BlockSpec((1, D, F), lambda e, b: (e, 0, 0)),
            pl.BlockSpec((1, F, D), lambda e, b: (e, 0, 0)),
        ],
        out_specs=pl.BlockSpec((1, 1, cap, D), lambda e, b: (b, e, 0, 0)),
        out_shape=jax.ShapeDtypeStruct((B, E, cap, D), BF16),
        scratch_shapes=[pltpu.VMEM((D, F), BF16), pltpu.VMEM((D, F), BF16), pltpu.VMEM((F, D), BF16)],
        compiler_params=pltpu.CompilerParams(
            dimension_semantics=("parallel", "arbitrary"), vmem_limit_bytes=VMEM_LIMIT),
        name="expert_ffn",
    )(xg, g, w1, w3, w2)


def _recur_masks(C):
    t = np.arange(C)[:, None]
    s = np.tile(np.arange(C), N_HEADS)[None, :]
    ms = []
    n = 2
    while n < C:
        ms.append((t // n == s // n).astype(np.float32))
        n *= 2
    ms.append((t == s).astype(np.float32))
    return jnp.asarray(np.stack(ms))


def _recur_kernel(*refs, rev, C, nsub, dk, dv, finish):
    if finish:
        q_ref, k_ref, v_ref, g_ref, s0_ref, m_ref, prev_ref, gate_ref, gn_ref, o_ref, sfin_ref, s_scr = refs
    else:
        q_ref, k_ref, v_ref, g_ref, s0_ref, m_ref, o_ref, sfin_ref, s_scr = refs
    HK = N_HEADS * dk
    HV = N_HEADS * dv
    i = pl.program_id(1)
    if finish:
        head_mean = jnp.where(lax.broadcasted_iota(jnp.int32, (HV, HV), 0) // dv
                              == lax.broadcasted_iota(jnp.int32, (HV, HV), 1) // dv, 1.0 / dv, 0.0).astype(BF16)

    @pl.when(i == 0)
    def _():
        s_scr[...] = s0_ref[0]

    row = lax.broadcasted_iota(jnp.int32, (C, HK), 0)
    pos = (C - 1 - row) if rev else row
    lane_head_k = lax.broadcasted_iota(jnp.int32, (C, HK), 1) // dk
    lane_head_v = lax.broadcasted_iota(jnp.int32, (C, HV), 1) // dv
    bd = (lax.broadcasted_iota(jnp.int32, (HV, HK), 0) // dv
          == lax.broadcasted_iota(jnp.int32, (HV, HK), 1) // dk)

    def ahead(x, s):
        return pltpu.roll(x, (C - s) if rev else s, axis=0)

    def behind(x, s):
        return pltpu.roll(x, s if rev else (C - s), axis=0)

    def stack_heads(x, lane_head):
        return jnp.concatenate([jnp.where(lane_head == h, x, 0.0) for h in range(N_HEADS)], axis=0).astype(BF16)

    def nt(a, b):
        return lax.dot_general(a, b, (((1,), (1,)), ((), ())), preferred_element_type=F32)

    def chunk(j, carry):
        c = (nsub - 1 - j) if rev else j
        sl = pl.ds(pl.multiple_of(c * C, C), C)
        q = q_ref[0, sl, :]
        k = k_ref[0, sl, :]
        v = v_ref[0, sl, :]
        g = g_ref[0, sl, :]
        b = g
        s = 1
        while s < C:
            b = b + jnp.where(pos >= s, ahead(b, s), 0.0)
            s *= 2
        a = nt(q.astype(BF16), stack_heads(k, lane_head_k)) * m_ref[m_ref.shape[0] - 1]
        m, lvl = 1, 0
        while m < C:
            n = 2 * m
            off = pos % n
            rq = jnp.where(off == m, ahead(b, 1), 0.0)
            rk = jnp.where(off == m - 1, b, 0.0)
            s = 1
            while s < m:
                rq = rq + ahead(rq, s)
                rk = rk + behind(rk, s)
                s *= 2
            late = off >= m
            qt = jnp.where(late, q * jnp.exp(jnp.where(late, b - rq, 0.0)), 0.0)
            kt = jnp.where(late, 0.0, k * jnp.exp(jnp.where(late, 0.0, rk - b)))
            al = nt(qt.astype(BF16), stack_heads(kt, lane_head_k))
            a = a + (al if n == C else al * m_ref[lvl])
            m, lvl = n, lvl + 1
        st = s_scr[...]
        o = jnp.dot(a.astype(BF16), stack_heads(v, lane_head_v), preferred_element_type=F32)
        o = o + nt((q * jnp.exp(b)).astype(BF16), st.astype(BF16))
        if finish:
            o = o + prev_ref[0, sl, :]
            sq_hi, sq_lo = _split_bf16(o * o)
            ms = (jnp.dot(sq_hi, head_mean, preferred_element_type=F32)
                  + jnp.dot(sq_lo, head_mean, preferred_element_type=F32))
            gate = gate_ref[0, sl, :]
            o = o * lax.rsqrt(ms + EPS) * gn_ref[...] * (gate * jax.nn.sigmoid(gate))
        o_ref[0, sl, :] = o
        b_end = b[0:1, :] if rev else b[C - 1:C, :]
        kend = (k * jnp.exp(b_end - b)).astype(BF16)
        upd = lax.dot_general(v.astype(BF16), kend, (((0,), (0,)), ((), ())), preferred_element_type=F32)
        s_scr[...] = st * jnp.exp(b_end) + jnp.where(bd, upd, 0.0)
        return carry

    lax.fori_loop(0, nsub, chunk, 0)

    @pl.when(i == pl.num_programs(1) - 1)
    def _():
        sfin_ref[0] = s_scr[...]


def _recurrence(pack, cols, s0, prev, gnorm, *, rev, dk, dv, tb):
    B, L, _ = pack.shape
    HK, HV = N_HEADS * dk, N_HEADS * dv
    C = RC
    nblk = L // tb
    masks = _recur_masks(C)
    cq, ck, cg, cv, cgate = cols

    def tok(col):
        return (lambda b, i: (b, nblk - 1 - i, col)) if rev else (lambda b, i: (b, i, col))

    in_specs = [
        pl.BlockSpec((1, tb, HK), tok(cq)), pl.BlockSpec((1, tb, HK), tok(ck)),
        pl.BlockSpec((1, tb, HV), tok(cv)), pl.BlockSpec((1, tb, HK), tok(cg)),
        pl.BlockSpec((1, HV, HK), lambda b, i: (b, 0, 0)),
        pl.BlockSpec(masks.shape, lambda b, i: (0, 0, 0)),
    ]
    args = [pack, pack, pack, pack, s0, masks]
    if prev is not None:
        in_specs += [pl.BlockSpec((1, tb, HV), tok(0)), pl.BlockSpec((1, tb, HV), tok(cgate)),
                     pl.BlockSpec((1, HV), lambda b, i: (0, 0))]
        args += [prev, pack, gnorm]
    kern = functools.partial(_recur_kernel, rev=rev, C=C, nsub=tb // C, dk=dk, dv=dv, finish=prev is not None)
    return pl.pallas_call(
        kern,
        grid=(B, nblk),
        in_specs=in_specs,
        out_specs=[pl.BlockSpec((1, tb, HV), tok(0)), pl.BlockSpec((1, HV, HK), lambda b, i: (b, 0, 0))],
        out_shape=[jax.ShapeDtypeStruct((B, L, HV), F32), jax.ShapeDtypeStruct((B, HV, HK), F32)],
        scratch_shapes=[pltpu.VMEM((HV, HK), F32)],
        compiler_params=pltpu.CompilerParams(
            dimension_semantics=("parallel", "arbitrary"), vmem_limit_bytes=VMEM_LIMIT),
        name="recur_rev" if rev else "recur_fwd",
    )(*args)


def _bidir(pack_c, pack_l, cols_f, cols_b, gnorm, *, dk, dv):
    B, Lc, _ = pack_c.shape
    s0 = jnp.zeros((B, N_HEADS * dv, N_HEADS * dk), F32)
    gn = jnp.tile(gnorm.astype(F32), N_HEADS).reshape(1, N_HEADS * dv)
    kw = dict(dk=dk, dv=dv)
    oc_f, sc_f = _recurrence(pack_c, cols_f, s0, None, None, rev=False, tb=Lc, **kw)
    ol_f, _ = _recurrence(pack_l, cols_f, sc_f, None, None, rev=False, tb=256, **kw)
    oc, sc_b = _recurrence(pack_c, cols_b, s0, oc_f, gn, rev=True, tb=Lc, **kw)
    ol, _ = _recurrence(pack_l, cols_b, sc_b, ol_f, gn, rev=True, tb=256, **kw)
    return oc, ol


def _attn_kernel(lam_ref, qt_ref, k_ref, vt_ref, g_ref, o_ref, qm_scr, s_scr, p_scr, m_scr, a_scr, acc_scr, *, tk):
    W = qt_ref.shape[1]
    tq = qt_ref.shape[2]
    nkb = k_ref.shape[1] // tk
    qt = qt_ref[0]
    row_pair = lax.broadcasted_iota(jnp.int32, (W, tq), 0) // DA_DK
    for j in range(N_QK):
        qm_scr[j] = jnp.where(row_pair == j, qt, 0.0).astype(BF16)
    m_scr[...] = jnp.full(m_scr.shape, -jnp.inf, F32)
    acc_scr[...] = jnp.zeros(acc_scr.shape, F32)

    def body(kb, carry):
        ks = pl.ds(pl.multiple_of(kb * tk, tk), tk)
        kblk = k_ref[0, ks, :]
        for j in range(N_QK):
            s_scr[j] = jnp.dot(kblk, qm_scr[j], preferred_element_type=F32)
        for j in range(N_QK):
            s = s_scr[j]
            m_old = m_scr[j]
            m_new = jnp.maximum(m_old, jnp.max(s, axis=0, keepdims=True))
            a_scr[j] = jnp.exp2(m_old - m_new)
            p_scr[j] = jnp.exp2(s - m_new).astype(BF16)
            m_scr[j] = m_new
        for j in range(N_QK):
            pv = jnp.dot(vt_ref[0, j // 2, :, ks], p_scr[j], preferred_element_type=F32)
            acc_scr[j] = a_scr[j] * acc_scr[j] + pv
        return carry

    lax.fori_loop(0, nkb, body, 0)
    lam = lam_ref[0]
    outs = []
    for h in range(DA_HEADS):
        a1, a2 = acc_scr[2 * h], acc_scr[2 * h + 1]
        o = a1[:DA_DV] / a1[DA_DV:DA_DV + 1] - lam * (a2[:DA_DV] / a2[DA_DV:DA_DV + 1])
        ms = jnp.mean(o * o, axis=0, keepdims=True)
        outs.append(o * lax.rsqrt(ms + EPS) * g_ref[...])
    o_ref[0] = jnp.concatenate(outs, axis=0).T


def _diff_attention(qt, k, vt, lam, gcol, *, tq, tk):
    B, W, Lq = qt.shape
    Lk = k.shape[1]
    HV = DA_HEADS * DA_DV
    return pl.pallas_call(
        functools.partial(_attn_kernel, tk=tk),
        grid=(B, Lq // tq),
        in_specs=[
            pl.BlockSpec(memory_space=pltpu.SMEM),
            pl.BlockSpec((1, W, tq), lambda b, i: (b, 0, i)),
            pl.BlockSpec((1, Lk, W), lambda b, i: (b, 0, 0)),
            pl.BlockSpec((1, DA_HEADS, VROWS, Lk), lambda b, i: (b, 0, 0, 0)),
            pl.BlockSpec((DA_DV, 1), lambda b, i: (0, 0)),
        ],
        out_specs=pl.BlockSpec((1, tq, HV), lambda b, i: (b, i, 0)),
        out_shape=jax.ShapeDtypeStruct((B, Lq, HV), F32),
        scratch_shapes=[
            pltpu.VMEM((N_QK, W, tq), BF16),
            pltpu.VMEM((N_QK, tk, tq), F32),
            pltpu.VMEM((N_QK, tk, tq), BF16),
            pltpu.VMEM((N_QK, 1, tq), F32),
            pltpu.VMEM((N_QK, 1, tq), F32),
            pltpu.VMEM((N_QK, VROWS, tq), F32),
        ],
        compiler_params=pltpu.CompilerParams(
            dimension_semantics=("parallel", "parallel"), vmem_limit_bytes=VMEM_LIMIT),
        name="diff_attn",
    )(lam, qt, k, vt, gcol)


def _hyena_kernel(bias_ref, z_ref, gate_ref, hp_ref, o_ref, acc_scr, *, nb, nbatch):
    c = pl.program_id(0)
    half = nb // 2
    rows = nb * nbatch
    z = z_ref[0].reshape(rows, HB)
    zb = z.astype(BF16)
    hrow = hp_ref[0]
    acc_scr[...] = jnp.zeros(acc_scr.shape, F32)
    for off in range(-half, half + 1):
        lo = (off + half) * HB
        r = jnp.broadcast_to(hrow[:, lo:lo + 2 * HB], (HB, 2 * HB))
        w = pltpu.roll(r, 1, 1, stride=1, stride_axis=0)[:, HB:].astype(BF16)
        s0, s1 = max(0, -off), nb - max(0, off)
        if s1 <= s0:
            continue
        src = slice(s0 * nbatch, s1 * nbatch)
        dst = slice((s0 + off) * nbatch, (s1 + off) * nbatch)
        acc_scr[dst, :] += jnp.dot(zb[src], w, preferred_element_type=F32)
    y = acc_scr[...] + bias_ref[c] * z
    o_ref[0] = (gate_ref[0].reshape(rows, HB) * y).reshape(nb, nbatch, HB)


def _hyena_order(z, gate, hp, bias):
    C, nb, B, _ = z.shape
    blk = pl.BlockSpec((1, nb, B, HB), lambda c: (c, 0, 0, 0))
    return pl.pallas_call(
        functools.partial(_hyena_kernel, nb=nb, nbatch=B),
        grid=(C,),
        in_specs=[
            pl.BlockSpec(memory_space=pltpu.SMEM),
            blk, blk,
            pl.BlockSpec((1, 1, hp.shape[-1]), lambda c: (c, 0, 0)),
        ],
        out_specs=blk,
        out_shape=jax.ShapeDtypeStruct(z.shape, F32),
        scratch_shapes=[pltpu.VMEM((nb * B, HB), F32)],
        compiler_params=pltpu.CompilerParams(
            dimension_semantics=("parallel",), vmem_limit_bytes=VMEM_LIMIT),
        name="hyena_conv",
    )(bias, z, gate, hp)


def _rms_norm(x, g):
    xf = x.astype(F32)
    y = xf * lax.rsqrt(jnp.mean(xf * xf, axis=-1, keepdims=True) + EPS)
    return (y * g.astype(F32)).astype(x.dtype)


def _modulate(x, g, shift, scale):
    return _rms_norm(x, g) * (1 + scale) + shift


def _short_conv(u, w):
    up = jnp.pad(u, ((0, 0), (1, 1), (0, 0)))
    return up[:, :-2] * w[0] + up[:, 1:-1] * w[1] + up[:, 2:] * w[2]


def _gate_weights(wa2, ba):
    qk = GLA_HEADS * GLA_DK
    w = jnp.zeros((LANES, 2 * qk), F32)
    w = w.at[:GLA_RANK, :qk].set(wa2[0].astype(F32)).at[GLA_RANK:2 * GLA_RANK, qk:].set(wa2[1].astype(F32))
    hi = w.astype(BF16)
    lo = (w - hi.astype(F32)).astype(BF16)
    return jnp.stack([hi, lo]), ba.astype(F32).reshape(1, 2 * qk)


def _hyena_filters(L, p):
    j = jnp.arange(L, dtype=F32)
    t = j / max(L - 1, 1)
    w = 2 * math.pi * j / L
    f = jnp.linspace(1e-4, HY_BANDS - 1, HY_BANDS, dtype=F32)
    feats = jnp.concatenate([t[:, None], jnp.cos(w[:, None] * f), -jnp.sin(w[:, None] * f)], axis=-1)
    h = jnp.sin(p['hy_freq'][0] * (feats @ p['hy_w1'] + p['hy_b1']))
    h = jnp.sin(p['hy_freq'][1] * (h @ p['hy_w2'] + p['hy_b2']))
    h = (h @ p['hy_w3']).astype(F32)
    dist = jnp.abs(j - L // 2) / (L // 2)
    h = h * (jnp.exp(-dist[:, None] * jnp.abs(p['hy_decay'].astype(F32))) + HY_SHIFT)
    h = h / jnp.sum(jnp.abs(h), axis=0, keepdims=True)
    return h.reshape(L, HY_ORDER, HY_WIDTH)


def _to_blocks(a):
    B, L, C = a.shape
    return a.reshape(B, L // HB, HB, C).transpose(3, 1, 0, 2)


def _from_blocks(a):
    C, nb, B, _ = a.shape
    return a.transpose(2, 1, 3, 0).reshape(B, nb * HB, C)


def _pad_filter(h):
    L = h.shape[0]
    nb = L // HB
    left = (HB - 1) - (L // 2 - (nb // 2) * HB)
    return jnp.pad(h.T, ((0, 0), (left, (nb + 2) * HB - L - left)))[:, None, :]


def _hyena_branch(u3, p):
    L = u3.shape[1]
    v, x1, x2 = jnp.split(_short_conv(u3, p['hy_conv_w']), 1 + HY_ORDER, axis=-1)
    h = _hyena_filters(L, p)
    z = _to_blocks(v)
    for o, gate in enumerate((x1, x2)):
        z = _hyena_order(z, _to_blocks(gate), _pad_filter(h[:, o]), p['hy_bias'][o])
    return _from_blocks(z)


def _rope_tables(L):
    quarter = DA_DK // 4
    freqs = ROPE_BASE ** (-jnp.arange(quarter, dtype=F32) / quarter)
    row = (jnp.arange(L) // GRID_W).astype(F32)[:, None] * freqs
    col = (jnp.arange(L) % GRID_W).astype(F32)[:, None] * freqs
    return jnp.cos(row), jnp.sin(row), jnp.cos(col), jnp.sin(col)


def _da_prep(t, g, tables):
    B, L, W = t.shape
    t = _rms_norm(t.reshape(B, L, N_QK, DA_DK), g)
    if tables is not None:
        cr, sr, cc, sc = [a[None, :, None, :] for a in tables]
        e = DA_DK // 4
        a1, a2, b1, b2 = t[..., :e], t[..., e:2 * e], t[..., 2 * e:3 * e], t[..., 3 * e:]
        t = jnp.concatenate([a1 * cr - a2 * sr, a1 * sr + a2 * cr, b1 * cc - b2 * sc, b1 * sc + b2 * cc], axis=-1)
    return t.reshape(B, L, W)


def _value_rows(v):
    B, Lk, _ = v.shape
    vt = v.reshape(B, Lk, DA_HEADS, DA_DV).transpose(0, 2, 3, 1)
    extra = jnp.concatenate([jnp.ones((B, DA_HEADS, 1, Lk), v.dtype),
                             jnp.zeros((B, DA_HEADS, VROWS - DA_DV - 1, Lk), v.dtype)], axis=2)
    return jnp.concatenate([vt, extra], axis=2).astype(BF16)


def _da_parts(da):
    w = DA_HEADS * 2 * DA_DK
    return da[..., :w], da[..., w:2 * w], da[..., 2 * w:]


def _group_weights(w_in):
    cols = []
    for names, width in PROJ_GROUPS:
        used = 0
        for nm in names:
            i = IN_NAMES.index(nm)
            cols.append(w_in[:, IN_OFFSETS[i]:IN_OFFSETS[i] + IN_WIDTHS[i]])
            used += IN_WIDTHS[i]
        if width > used:
            cols.append(jnp.zeros((w_in.shape[0], width - used), w_in.dtype))
    return jnp.concatenate(cols, axis=1).astype(BF16)


GLA_COLS_F, GLA_COLS_B = (0, 1, 2, 2, 3), (0, 1, 3, 2, 3)
HG_COLS_F, HG_COLS_B = (0, 1, 3, 5, 6), (0, 2, 4, 5, 6)


def _token_mixer(gc, gx, p, lam_init, last):
    gla_c, gla_x = _bidir(gc[0], gx[0], GLA_COLS_F, GLA_COLS_B, p['gla_norm_g'], dk=GLA_DK, dv=GLA_DV)
    hg_c, hg_x = _bidir(gc[2], gx[2], HG_COLS_F, HG_COLS_B, p['hg_norm_g'], dk=HG_DK, dv=HG_DV)
    (dqc, dkc, dvc), (dqx, dkx, dvx) = _da_parts(gc[3]), _da_parts(gx[3])
    tables = _rope_tables(dqx.shape[1])
    scale = DA_DK ** -0.5 * LOG2E
    qx = (_da_prep(dqx, p['da_qnorm_g'], tables) * scale).transpose(0, 2, 1)
    kx = _da_prep(dkx, p['da_knorm_g'], tables)
    kc = _da_prep(dkc, p['da_knorm_g'], None)
    k_all = jnp.concatenate([kc, kx], axis=1).astype(BF16)
    vt_all = _value_rows(jnp.concatenate([dvc, dvx], axis=1))
    lp = p['da_lam'].astype(F32)
    lam = (jnp.exp(jnp.sum(lp[0] * lp[1])) - jnp.exp(jnp.sum(lp[2] * lp[3])) + lam_init).reshape(1)
    gcol = (p['da_norm_g'].astype(F32) * (1 - lam_init)).reshape(DA_DV, 1)
    da_x = _diff_attention(qx, k_all, vt_all, lam, gcol, tq=256, tk=256)
    outs_x = (gla_x, _hyena_branch(gx[1], p), hg_x, da_x)
    if last:
        return None, outs_x
    Lc = kc.shape[1]
    qc = (_da_prep(dqc, p['da_qnorm_g'], None) * scale).transpose(0, 2, 1)
    da_c = _diff_attention(qc, k_all[:, :Lc], vt_all[..., :Lc], lam, gcol, tq=Lc, tk=Lc)
    return (gla_c, _hyena_branch(gc[1], p), hg_c, da_c), outs_x


def _expert_choice_moe(x, m, h, router, w1, w3, w2):
    B, L, D = h.shape
    cap = CAPACITY_FACTOR * L // N_EXPERTS
    aff = jax.nn.softmax((h @ router).astype(F32), axis=-1)
    g, idx = lax.top_k(aff.transpose(0, 2, 1), cap)
    xg = jax.vmap(lambda hb, ib: hb[ib])(h, idx)
    y = _expert_ffn(xg, g[..., None], w1, w3, w2)
    return _combine(idx.reshape(B, 1, N_EXPERTS * cap), y.reshape(B, N_EXPERTS * cap, D), x, m, tt=min(L, 512))


def kernel(x, c, ctx, c_ctx, ada_w, ada_b, norm1_g, norm2_g, w_in, gla_wa2, gla_ba, gla_norm_g,
           hy_conv_w, hy_w1, hy_b1, hy_w2, hy_b2, hy_w3, hy_freq, hy_decay, hy_bias,
           hg_lower, hg_norm_g, da_qnorm_g, da_knorm_g, da_lam, da_norm_g, w_branch, w_out,
           moe_router, moe_w1, moe_w3, moe_w2):
    B, L, D = x.shape
    P = jax.nn.softmax(hg_lower.astype(F32), axis=0)
    lower = jnp.cumsum(P, axis=0) - P[0]
    sc = jax.nn.silu(c)
    scc = jax.nn.silu(c_ctx)
    xc, xx = ctx, x
    for l in range(DEPTH):
        last = l == DEPTH - 1
        lam_init = 0.8 - 0.6 * math.exp(-0.3 * l)
        p = {'gla_wa2': gla_wa2[l], 'gla_ba': gla_ba[l], 'gla_norm_g': gla_norm_g[l],
             'hy_conv_w': hy_conv_w[l], 'hy_w1': hy_w1[l], 'hy_b1': hy_b1[l], 'hy_w2': hy_w2[l], 'hy_b2': hy_b2[l],
             'hy_w3': hy_w3[l], 'hy_freq': hy_freq[l], 'hy_decay': hy_decay[l], 'hy_bias': hy_bias[l],
             'hg_norm_g': hg_norm_g[l], 'da_qnorm_g': da_qnorm_g[l], 'da_knorm_g': da_knorm_g[l],
             'da_lam': da_lam[l], 'da_norm_g': da_norm_g[l]}
        mod_x = jnp.split((sc @ ada_w[l] + ada_b[l])[:, None, :], ADA_CHUNKS, axis=-1)
        mod_c1 = jnp.split((scc @ ada_w[l] + ada_b[l])[None, None, :], ADA_CHUNKS, axis=-1)
        mod_c = [jnp.broadcast_to(m, (B, 1, D)) for m in mod_c1]
        wg = _group_weights(w_in[l])
        wa, ba = _gate_weights(gla_wa2[l], gla_ba[l])
        lb = lower[l].reshape(1, 2 * HG_HEADS * HG_DK)
        wb = w_branch[l].astype(BF16)
        wo = w_out[l].astype(BF16)
        w1, w3, w2 = moe_w1[l], moe_w3[l], moe_w2[l]
        gx = _in_proj(xx, norm1_g[l], mod_x[0], mod_x[1], wg, wa, ba, lb, tm=256)
        gc = _in_proj(xc, norm1_g[l], mod_c[0], mod_c[1], wg, wa, ba, lb, tm=256)
        outs_c, outs_x = _token_mixer(gc, gx, p, lam_init, last)
        xx = _merge_out(outs_x, gx[4], wb, wo, xx, mod_x[2], tm=256)
        hx = _modulate(xx, norm2_g[l], mod_x[3], mod_x[4])
        xx = _expert_choice_moe(xx, mod_x[5], hx, moe_router[l], w1, w3, w2)
        if not last:
            xc = _merge_out(outs_c, gc[4], wb, wo, xc, mod_c[2], tm=256)
            hc = _modulate(xc, norm2_g[l], mod_c[3], mod_c[4])
            xc = _expert_choice_moe(xc, mod_c[5], hc, moe_router[l], w1, w3, w2)
    return xx
```

```python
import functools
import math

import jax
import jax.numpy as jnp
import numpy as np
from jax import lax
from jax.experimental import pallas as pl
from jax.experimental.pallas import tpu as pltpu

D_MODEL = 1024
DEPTH = 2
GRID_W = 64
N_BRANCH = 4
MIX_W = 256
GLA_HEADS = 4
GLA_DK = 32
GLA_DV = 64
GLA_RANK = 16
GLA_TAU = 16.0
HY_WIDTH = 256
HY_ORDER = 2
HY_BANDS = 16
HY_SHIFT = 0.05
HG_HEADS = 4
HG_DK = 64
HG_DV = 64
DA_HEADS = 4
DA_DK = 32
DA_DV = 64
ROPE_BASE = 10000.0
N_EXPERTS = 16
EXPERT_FF = 1024
CAPACITY_FACTOR = 2
ADA_CHUNKS = 6
EPS = 1e-6
F_TINY = 1e-20

IN_NAMES = ('gla_q', 'gla_k', 'gla_v', 'gla_af', 'gla_ab', 'gla_g', 'hy',
            'hg_q', 'hg_ff', 'hg_fb', 'hg_i', 'hg_g', 'da_q', 'da_k', 'da_v', 'merge')
IN_WIDTHS = (GLA_HEADS * GLA_DK, GLA_HEADS * GLA_DK, GLA_HEADS * GLA_DV, GLA_RANK, GLA_RANK, GLA_HEADS * GLA_DV,
             (1 + HY_ORDER) * HY_WIDTH,
             HG_HEADS * HG_DK, HG_HEADS * HG_DK, HG_HEADS * HG_DK, HG_HEADS * HG_DV, HG_HEADS * HG_DV,
             DA_HEADS * 2 * DA_DK, DA_HEADS * 2 * DA_DK, DA_HEADS * DA_DV,
             N_BRANCH * D_MODEL)
IN_OFFSETS = tuple(int(v) for v in np.cumsum((0,) + IN_WIDTHS)[:-1])

PROJ_GROUPS = (
    (('gla_q', 'gla_k', 'gla_v', 'gla_g', 'gla_af', 'gla_ab'), 896),
    (('hy',), 768),
    (('hg_q', 'hg_ff', 'hg_fb', 'hg_i', 'hg_g'), 1280),
    (('da_q', 'da_k', 'da_v'), 768),
    (('merge',), 4096),
)
PROJ_WIDTHS = tuple(w for _, w in PROJ_GROUPS)
LANES = 128
OUT_WIDTHS = (4 * GLA_HEADS * GLA_DK + 2 * GLA_HEADS * GLA_DV, PROJ_WIDTHS[1],
              5 * HG_HEADS * HG_DK + 2 * HG_HEADS * HG_DV, PROJ_WIDTHS[3], PROJ_WIDTHS[4])
VMEM_LIMIT = 56 * 1024 * 1024

BF16 = jnp.bfloat16
F32 = jnp.float32
N_HEADS = 4
RC = 64
N_QK = 2 * DA_HEADS
VROWS = DA_DV + 8
LOG2E = 1.4426950408889634
HB = 256


def _split_bf16(a):
    hi = a.astype(BF16)
    return hi, (a - hi.astype(F32)).astype(BF16)


def _in_proj_kernel(x_ref, g_ref, sh_ref, sc_ref, w_ref, wa_ref, ba_ref, lb_ref,
                    gla_ref, hy_ref, hg_ref, da_ref, mg_ref):
    x = x_ref[0]
    ms = jnp.mean(x * x, axis=-1, keepdims=True)
    h = x * lax.rsqrt(ms + EPS) * g_ref[...]
    h = (h * (1.0 + sc_ref[0]) + sh_ref[0]).astype(BF16)
    offs = np.cumsum((0,) + PROJ_WIDTHS)

    def proj(k):
        return jnp.dot(h, w_ref[:, int(offs[k]):int(offs[k + 1])], preferred_element_type=F32)

    r = proj(0)
    qk = GLA_HEADS * GLA_DK
    vw = GLA_HEADS * GLA_DV
    gla_ref[0, :, 0:qk] = r[:, 0:qk] * (GLA_DK ** -0.5)
    gla_ref[0, :, qk:2 * qk] = r[:, qk:2 * qk]
    a_hi, a_lo = _split_bf16(r[:, 2 * qk + 2 * vw:])
    z = (jnp.dot(a_hi, wa_ref[0], preferred_element_type=F32) + jnp.dot(a_lo, wa_ref[0], preferred_element_type=F32)
         + jnp.dot(a_hi, wa_ref[1], preferred_element_type=F32) + ba_ref[...])
    gla_ref[0, :, 2 * qk:4 * qk] = jax.nn.log_sigmoid(z) * (1.0 / GLA_TAU)
    gla_ref[0, :, 4 * qk:4 * qk + 2 * vw] = r[:, 2 * qk:2 * qk + 2 * vw]

    hy_ref[0] = proj(1)

    r = proj(2)
    hw = HG_HEADS * HG_DK
    q = r[:, 0:hw]
    hg_ref[0, :, 0:hw] = q * jax.nn.sigmoid(q)
    zf = r[:, hw:3 * hw]
    lb = lb_ref[...]
    hg_ref[0, :, hw:3 * hw] = (1.0 - lb) * jax.nn.sigmoid(-zf)
    hg_ref[0, :, 3 * hw:5 * hw] = jnp.log(jnp.maximum(lb + (1.0 - lb) * jax.nn.sigmoid(zf), F_TINY))
    hg_ref[0, :, 5 * hw:7 * hw] = r[:, 3 * hw:5 * hw]

    da_ref[0] = proj(3)
    mg_ref[0] = proj(4)


def _in_proj(x, g, shift, scale, w_groups, wa, ba, lb, tm):
    B, L, D = x.shape
    nw = w_groups.shape[1]
    const2 = lambda b, i: (0, 0)
    return pl.pallas_call(
        _in_proj_kernel,
        grid=(B, L // tm),
        in_specs=[
            pl.BlockSpec((1, tm, D), lambda b, i: (b, i, 0)),
            pl.BlockSpec((1, D), const2),
            pl.BlockSpec((1, 1, D), lambda b, i: (b, 0, 0)),
            pl.BlockSpec((1, 1, D), lambda b, i: (b, 0, 0)),
            pl.BlockSpec((D, nw), const2, pipeline_mode=pl.Buffered(1)),
            pl.BlockSpec(wa.shape, lambda b, i: (0, 0, 0)),
            pl.BlockSpec(ba.shape, const2),
            pl.BlockSpec(lb.shape, const2),
        ],
        out_specs=[pl.BlockSpec((1, tm, w), lambda b, i: (b, i, 0)) for w in OUT_WIDTHS],
        out_shape=[jax.ShapeDtypeStruct((B, L, w), F32) for w in OUT_WIDTHS],
        compiler_params=pltpu.CompilerParams(
            dimension_semantics=("parallel", "parallel"), vmem_limit_bytes=VMEM_LIMIT),
        name="in_proj",
    )(x, g.reshape(1, D), shift, scale, w_groups, wa, ba, lb)


def _merge_kernel(o0_ref, o1_ref, o2_ref, o3_ref, gate_ref, wb_ref, wo_ref, x_ref, m_ref, out_ref):
    D = D_MODEL
    acc = None
    for i, o_ref in enumerate((o0_ref, o1_ref, o2_ref, o3_ref)):
        t = jnp.dot(o_ref[0].astype(BF16), wb_ref[i], preferred_element_type=F32)
        t = jax.nn.sigmoid(gate_ref[0, :, i * D:(i + 1) * D]) * t
        acc = t if acc is None else acc + t
    mx = jnp.dot(acc.astype(BF16), wo_ref[...], preferred_element_type=F32)
    out_ref[0] = x_ref[0] + m_ref[0] * mx


def _merge_out(outs, gate_cols, w_branch, w_out, x, m, tm):
    B, L, D = x.shape
    return pl.pallas_call(
        _merge_kernel,
        grid=(B, L // tm),
        in_specs=[pl.BlockSpec((1, tm, MIX_W), lambda b, i: (b, i, 0)) for _ in range(N_BRANCH)] + [
            pl.BlockSpec((1, tm, N_BRANCH * D), lambda b, i: (b, i, 0)),
            pl.BlockSpec((N_BRANCH, MIX_W, D), lambda b, i: (0, 0, 0)),
            pl.BlockSpec((D, D), lambda b, i: (0, 0)),
            pl.BlockSpec((1, tm, D), lambda b, i: (b, i, 0)),
            pl.BlockSpec((1, 1, D), lambda b, i: (b, 0, 0)),
        ],
        out_specs=pl.BlockSpec((1, tm, D), lambda b, i: (b, i, 0)),
        out_shape=jax.ShapeDtypeStruct((B, L, D), F32),
        compiler_params=pltpu.CompilerParams(
            dimension_semantics=("parallel", "parallel"), vmem_limit_bytes=VMEM_LIMIT),
        name="merge_out",
    )(*outs, gate_cols, w_branch, w_out, x, m)


def _expert_ffn_kernel(x_ref, g_ref, w1_ref, w3_ref, w2_ref, out_ref, w1_scr, w3_scr, w2_scr):
    @pl.when(pl.program_id(1) == 0)
    def _():
        w1_scr[...] = w1_ref[0].astype(BF16)
        w3_scr[...] = w3_ref[0].astype(BF16)
        w2_scr[...] = w2_ref[0].astype(BF16)

    x = x_ref[0, 0].astype(BF16)
    a = jnp.dot(x, w1_scr[...], preferred_element_type=F32)
    b = jnp.dot(x, w3_scr[...], preferred_element_type=F32)
    h = (a * jax.nn.sigmoid(a) * b).astype(BF16)
    y = jnp.dot(h, w2_scr[...], preferred_element_type=F32)
    out_ref[0, 0] = (y * g_ref[0, 0]).astype(out_ref.dtype)


def _combine_kernel(idx_ref, y_ref, x_ref, m_ref, out_ref, *, ks):
    tt = out_ref.shape[1]
    S = y_ref.shape[1]
    base = pl.program_id(1) * tt
    acc = None
    for s0 in range(0, S, ks):
        tok = lax.broadcasted_iota(jnp.int32, (tt, ks), 0) + base
        onehot = jnp.where(tok == idx_ref[0, :, s0:s0 + ks], 1.0, 0.0).astype(BF16)
        part = jnp.dot(onehot, y_ref[0, s0:s0 + ks, :], preferred_element_type=F32)
        acc = part if acc is None else acc + part
    out_ref[0] = x_ref[0] + m_ref[0] * acc


def _combine(idx, y, x, m, tt):
    B, L, D = x.shape
    S = y.shape[1]
    return pl.pallas_call(
        functools.partial(_combine_kernel, ks=min(S, 2048)),
        grid=(B, L // tt),
        in_specs=[
            pl.BlockSpec((1, 1, S), lambda b, i: (b, 0, 0)),
            pl.BlockSpec((1, S, D), lambda b, i: (b, 0, 0)),
            pl.BlockSpec((1, tt, D), lambda b, i: (b, i, 0)),
            pl.BlockSpec((1, 1, D), lambda b, i: (b, 0, 0)),
        ],
        out_specs=pl.BlockSpec((1, tt, D), lambda b, i: (b, i, 0)),
        out_shape=jax.ShapeDtypeStruct((B, L, D), F32),
        compiler_params=pltpu.CompilerParams(
            dimension_semantics=("parallel", "parallel"), vmem_limit_bytes=VMEM_LIMIT),
        name="moe_combine",
    )(idx, y, x, m)


def _expert_ffn(xg, g, w1, w3, w2):
    B, E, cap, D = xg.shape
    F = w1.shape[-1]
    return pl.pallas_call(
        _expert_ffn_kernel,
        grid=(E, B),
        in_specs=[
            pl.BlockSpec((1, 1, cap, D), lambda e, b: (b, e, 0, 0)),
            pl.BlockSpec((1, 1, cap, 1), lambda e, b: (b, e, 0, 0)),
            pl.BlockSpec((1, D, F), lambda e, b: (e, 0, 0)),
            pl.BlockSpec((1, D, F), lambda e, b: (e, 0, 0)),
            pl.BlockSpec((1, F, D), lambda e, b: (e, 0, 0)),
        ],
        out_specs=pl.BlockSpec((1, 1, cap, D), lambda e, b: (b, e, 0, 0)),
        out_shape=jax.ShapeDtypeStruct((B, E, cap, D), BF16),
        scratch_shapes=[pltpu.VMEM((D, F), BF16), pltpu.VMEM((D, F), BF16), pltpu.VMEM((F, D), BF16)],
        compiler_params=pltpu.CompilerParams(
            dimension_semantics=("parallel", "arbitrary"), vmem_limit_bytes=VMEM_LIMIT),
        name="expert_ffn",
    )(xg, g, w1, w3, w2)


def _recur_tables(C, rev):
    pos = (C - 1 - np.arange(C)) if rev else np.arange(C)
    row_of = np.argsort(pos)
    levels = int(np.log2(C))
    diff = np.zeros((levels + 1, C, C), np.float32)
    mask = np.zeros((levels + 1, C, C), np.float32)
    for t in range(C):
        p = pos[t]
        diff[0, t, row_of[:p + 1]] = 1.0
        mask[0, t, t] = 1.0
        for l in range(1, levels + 1):
            n, m = 2 ** l, 2 ** (l - 1)
            off = p % n
            mid = p - off + m - 1
            if off >= m:
                diff[l, t, row_of[mid + 1:p + 1]] = 1.0
                mask[l, t, row_of[p - off:mid + 1]] = 1.0
            else:
                diff[l, t, row_of[p + 1:mid + 1]] = 1.0
    return (jnp.asarray(np.tile(diff.reshape((levels + 1) * C, C), (1, 3)), BF16),
            jnp.asarray(np.tile(mask, (1, 1, N_HEADS))))


def _recur_kernel(*refs, rev, C, nsub, dk, dv, finish):
    if finish:
        (q_ref, k_ref, v_ref, g_ref, s0_ref, d_ref, m_ref, prev_ref, gate_ref, gn_ref,
         o_ref, sfin_ref, s_scr, e_scr, a_scr) = refs
    else:
        q_ref, k_ref, v_ref, g_ref, s0_ref, d_ref, m_ref, o_ref, sfin_ref, s_scr, e_scr, a_scr = refs
    HK = N_HEADS * dk
    HV = N_HEADS * dv
    i = pl.program_id(1)
    if finish:
        head_mean = jnp.where(lax.broadcasted_iota(jnp.int32, (HV, HV), 0) // dv
                              == lax.broadcasted_iota(jnp.int32, (HV, HV), 1) // dv, 1.0 / dv, 0.0).astype(BF16)

    @pl.when(i == 0)
    def _():
        s_scr[...] = s0_ref[0]

    head_k = [(lax.broadcasted_iota(jnp.int32, (1, HK), 1) // dk == h).astype(BF16) for h in range(N_HEADS)]
    head_v = [(lax.broadcasted_iota(jnp.int32, (1, HV), 1) // dv == h).astype(BF16) for h in range(N_HEADS)]
    bd = (lax.broadcasted_iota(jnp.int32, (HV, HK), 0) // dv
          == lax.broadcasted_iota(jnp.int32, (HV, HK), 1) // dk)
    levels = m_ref.shape[0] - 1

    def stack_heads(x, head):
        xb = x.astype(BF16)
        return jnp.concatenate([xb * head[h] for h in range(N_HEADS)], axis=0)

    def nt(a, b):
        return lax.dot_general(a, b, (((1,), (1,)), ((), ())), preferred_element_type=F32)

    order = [(nsub - 1 - j) if rev else j for j in range(nsub)]
    for c in order:
        g = g_ref[0, c * C:(c + 1) * C, :]
        g_hi = g.astype(BF16)
        r1 = g - g_hi.astype(F32)
        g_mid = r1.astype(BF16)
        g_lo = (r1 - g_mid.astype(F32)).astype(BF16)
        e_scr[c] = jnp.dot(d_ref[...], jnp.concatenate([g_hi, g_mid, g_lo], axis=0), preferred_element_type=F32)
    for c in order:
        q = q_ref[0, c * C:(c + 1) * C, :]
        k = k_ref[0, c * C:(c + 1) * C, :]
        a = nt(q.astype(BF16), stack_heads(k, head_k)) * m_ref[0]
        for lvl in range(1, levels + 1):
            x = jnp.exp(e_scr[c, lvl * C:(lvl + 1) * C, :])
            a = a + nt((q * x).astype(BF16), stack_heads(k * x, head_k)) * m_ref[lvl]
        a_scr[c] = a.astype(BF16)
    for c in order:
        sl = slice(c * C, (c + 1) * C)
        q = q_ref[0, sl, :]
        k = k_ref[0, sl, :]
        v = v_ref[0, sl, :]
        b = e_scr[c, 0:C, :]
        st = s_scr[...]
        o = jnp.dot(a_scr[c], stack_heads(v, head_v), preferred_element_type=F32)
        o = o + nt((q * jnp.exp(b)).astype(BF16), st.astype(BF16))
        if finish:
            o = o + prev_ref[0, sl, :]
            sq_hi, sq_lo = _split_bf16(o * o)
            ms = (jnp.dot(sq_hi, head_mean, preferred_element_type=F32)
                  + jnp.dot(sq_lo, head_mean, preferred_element_type=F32))
            gate = gate_ref[0, sl, :]
            o = o * lax.rsqrt(ms + EPS) * gn_ref[...] * (gate * jax.nn.sigmoid(gate))
        o_ref[0, sl, :] = o
        b_end = b[0:1, :] if rev else b[C - 1:C, :]
        kend = (k * jnp.exp(b_end - b)).astype(BF16)
        upd = lax.dot_general(v.astype(BF16), kend, (((0,), (0,)), ((), ())), preferred_element_type=F32)
        s_scr[...] = st * jnp.exp(b_end) + jnp.where(bd, upd, 0.0)

    @pl.when(i == pl.num_programs(1) - 1)
    def _():
        sfin_ref[0] = s_scr[...]


def _recurrence(pack, cols, s0, prev, gnorm, *, rev, dk, dv, tb):
    B, L, _ = pack.shape
    HK, HV = N_HEADS * dk, N_HEADS * dv
    C = RC
    nblk = L // tb
    diff, masks = _recur_tables(C, rev)
    cq, ck, cg, cv, cgate = cols

    def tok(col):
        return (lambda b, i: (b, nblk - 1 - i, col)) if rev else (lambda b, i: (b, i, col))

    in_specs = [
        pl.BlockSpec((1, tb, HK), tok(cq)), pl.BlockSpec((1, tb, HK), tok(ck)),
        pl.BlockSpec((1, tb, HV), tok(cv)), pl.BlockSpec((1, tb, HK), tok(cg)),
        pl.BlockSpec((1, HV, HK), lambda b, i: (b, 0, 0)),
        pl.BlockSpec(diff.shape, lambda b, i: (0, 0)),
        pl.BlockSpec(masks.shape, lambda b, i: (0, 0, 0)),
    ]
    args = [pack, pack, pack, pack, s0, diff, masks]
    if prev is not None:
        in_specs += [pl.BlockSpec((1, tb, HV), tok(0)), pl.BlockSpec((1, tb, HV), tok(cgate)),
                     pl.BlockSpec((1, HV), lambda b, i: (0, 0))]
        args += [prev, pack, gnorm]
    kern = functools.partial(_recur_kernel, rev=rev, C=C, nsub=tb // C, dk=dk, dv=dv, finish=prev is not None)
    return pl.pallas_call(
        kern,
        grid=(B, nblk),
        in_specs=in_specs,
        out_specs=[pl.BlockSpec((1, tb, HV), tok(0)), pl.BlockSpec((1, HV, HK), lambda b, i: (b, 0, 0))],
        out_shape=[jax.ShapeDtypeStruct((B, L, HV), F32), jax.ShapeDtypeStruct((B, HV, HK), F32)],
        scratch_shapes=[pltpu.VMEM((HV, HK), F32), pltpu.VMEM((tb // C, diff.shape[0], HK), F32),
                        pltpu.VMEM((tb // C, C, N_HEADS * C), BF16)],
        compiler_params=pltpu.CompilerParams(
            dimension_semantics=("parallel", "arbitrary"), vmem_limit_bytes=VMEM_LIMIT),
        name="recur_rev" if rev else "recur_fwd",
    )(*args)


def _bidir(pack_c, pack_l, cols_f, cols_b, gnorm, *, dk, dv):
    B, Lc, _ = pack_c.shape
    s0 = jnp.zeros((B, N_HEADS * dv, N_HEADS * dk), F32)
    gn = jnp.tile(gnorm.astype(F32), N_HEADS).reshape(1, N_HEADS * dv)
    kw = dict(dk=dk, dv=dv)
    oc_f, sc_f = _recurrence(pack_c, cols_f, s0, None, None, rev=False, tb=Lc, **kw)
    ol_f, _ = _recurrence(pack_l, cols_f, sc_f, None, None, rev=False, tb=256, **kw)
    oc, sc_b = _recurrence(pack_c, cols_b, s0, oc_f, gn, rev=True, tb=Lc, **kw)
    ol, _ = _recurrence(pack_l, cols_b, sc_b, ol_f, gn, rev=True, tb=256, **kw)
    return oc, ol


def _attn_kernel(lam_ref, qt_ref, k_ref, vt_ref, g_ref, o_ref, qm_scr, s_scr, p_scr, m_scr, a_scr, acc_scr, *, tk):
    W = qt_ref.shape[1]
    tq = qt_ref.shape[2]
    nkb = k_ref.shape[1] // tk
    qt = qt_ref[0]
    row_pair = lax.broadcasted_iota(jnp.int32, (W, tq), 0) // DA_DK
    for j in range(N_QK):
        qm_scr[j] = jnp.where(row_pair == j, qt, 0.0).astype(BF16)
    m_scr[...] = jnp.full(m_scr.shape, -jnp.inf, F32)
    acc_scr[...] = jnp.zeros(acc_scr.shape, F32)

    def body(kb, carry):
        ks = pl.ds(pl.multiple_of(kb * tk, tk), tk)
        kblk = k_ref[0, ks, :]
        for j in range(N_QK):
            s_scr[j] = jnp.dot(kblk, qm_scr[j], preferred_element_type=F32)
        for j in range(N_QK):
            s = s_scr[j]
            m_old = m_scr[j]
            m_new = jnp.maximum(m_old, jnp.max(s, axis=0, keepdims=True))
            a_scr[j] = jnp.exp2(m_old - m_new)
            p_scr[j] = jnp.exp2(s - m_new).astype(BF16)
            m_scr[j] = m_new
        for j in range(N_QK):
            pv = jnp.dot(vt_ref[0, j // 2, :, ks], p_scr[j], preferred_element_type=F32)
            acc_scr[j] = a_scr[j] * acc_scr[j] + pv
        return carry

    lax.fori_loop(0, nkb, body, 0)
    lam = lam_ref[0]
    outs = []
    for h in range(DA_HEADS):
        a1, a2 = acc_scr[2 * h], acc_scr[2 * h + 1]
        o = a1[:DA_DV] / a1[DA_DV:DA_DV + 1] - lam * (a2[:DA_DV] / a2[DA_DV:DA_DV + 1])
        ms = jnp.mean(o * o, axis=0, keepdims=True)
        outs.append(o * lax.rsqrt(ms + EPS) * g_ref[...])
    o_ref[0] = jnp.concatenate(outs, axis=0).T


def _diff_attention(qt, k, vt, lam, gcol, *, tq, tk):
    B, W, Lq = qt.shape
    Lk = k.shape[1]
    HV = DA_HEADS * DA_DV
    return pl.pallas_call(
        functools.partial(_attn_kernel, tk=tk),
        grid=(B, Lq // tq),
        in_specs=[
            pl.BlockSpec(memory_space=pltpu.SMEM),
            pl.BlockSpec((1, W, tq), lambda b, i: (b, 0, i)),
            pl.BlockSpec((1, Lk, W), lambda b, i: (b, 0, 0)),
            pl.BlockSpec((1, DA_HEADS, VROWS, Lk), lambda b, i: (b, 0, 0, 0)),
            pl.BlockSpec((DA_DV, 1), lambda b, i: (0, 0)),
        ],
        out_specs=pl.BlockSpec((1, tq, HV), lambda b, i: (b, i, 0)),
        out_shape=jax.ShapeDtypeStruct((B, Lq, HV), F32),
        scratch_shapes=[
            pltpu.VMEM((N_QK, W, tq), BF16),
            pltpu.VMEM((N_QK, tk, tq), F32),
            pltpu.VMEM((N_QK, tk, tq), BF16),
            pltpu.VMEM((N_QK, 1, tq), F32),
            pltpu.VMEM((N_QK, 1, tq), F32),
            pltpu.VMEM((N_QK, VROWS, tq), F32),
        ],
        compiler_params=pltpu.CompilerParams(
            dimension_semantics=("parallel", "parallel"), vmem_limit_bytes=VMEM_LIMIT),
        name="diff_attn",
    )(lam, qt, k, vt, gcol)


def _hyena_kernel(bias_ref, z_ref, gate_ref, hp_ref, o_ref, acc_scr, *, nb, nbatch):
    c = pl.program_id(0)
    half = nb // 2
    rows = nb * nbatch
    z = z_ref[0].reshape(rows, HB)
    zb = z.astype(BF16)
    hrow = hp_ref[0]
    acc_scr[...] = jnp.zeros(acc_scr.shape, F32)
    for off in range(-half, half + 1):
        lo = (off + half) * HB
        r = jnp.broadcast_to(hrow[:, lo:lo + 2 * HB], (HB, 2 * HB))
        w = pltpu.roll(r, 1, 1, stride=1, stride_axis=0)[:, HB:].astype(BF16)
        s0, s1 = max(0, -off), nb - max(0, off)
        if s1 <= s0:
            continue
        src = slice(s0 * nbatch, s1 * nbatch)
        dst = slice((s0 + off) * nbatch, (s1 + off) * nbatch)
        acc_scr[dst, :] += jnp.dot(zb[src], w, preferred_element_type=F32)
    y = acc_scr[...] + bias_ref[c] * z
    o_ref[0] = (gate_ref[0].reshape(rows, HB) * y).reshape(nb, nbatch, HB)


def _hyena_order(z, gate, hp, bias):
    C, nb, B, _ = z.shape
    blk = pl.BlockSpec((1, nb, B, HB), lambda c: (c, 0, 0, 0))
    return pl.pallas_call(
        functools.partial(_hyena_kernel, nb=nb, nbatch=B),
        grid=(C,),
        in_specs=[
            pl.BlockSpec(memory_space=pltpu.SMEM),
            blk, blk,
            pl.BlockSpec((1, 1, hp.shape[-1]), lambda c: (c, 0, 0)),
        ],
        out_specs=blk,
        out_shape=jax.ShapeDtypeStruct(z.shape, F32),
        scratch_shapes=[pltpu.VMEM((nb * B, HB), F32)],
        compiler_params=pltpu.CompilerParams(
            dimension_semantics=("parallel",), vmem_limit_bytes=VMEM_LIMIT),
        name="hyena_conv",
    )(bias, z, gate, hp)


def _rms_norm(x, g):
    xf = x.astype(F32)
    y = xf * lax.rsqrt(jnp.mean(xf * xf, axis=-1, keepdims=True) + EPS)
    return (y * g.astype(F32)).astype(x.dtype)


def _modulate(x, g, shift, scale):
    return _rms_norm(x, g) * (1 + scale) + shift


def _short_conv(u, w):
    up = jnp.pad(u, ((0, 0), (1, 1), (0, 0)))
    return up[:, :-2] * w[0] + up[:, 1:-1] * w[1] + up[:, 2:] * w[2]


def _gate_weights(wa2, ba):
    qk = GLA_HEADS * GLA_DK
    w = jnp.zeros((LANES, 2 * qk), F32)
    w = w.at[:GLA_RANK, :qk].set(wa2[0].astype(F32)).at[GLA_RANK:2 * GLA_RANK, qk:].set(wa2[1].astype(F32))
    hi = w.astype(BF16)
    lo = (w - hi.astype(F32)).astype(BF16)
    return jnp.stack([hi, lo]), ba.astype(F32).reshape(1, 2 * qk)


def _hyena_filters(L, p):
    j = jnp.arange(L, dtype=F32)
    t = j / max(L - 1, 1)
    w = 2 * math.pi * j / L
    f = jnp.linspace(1e-4, HY_BANDS - 1, HY_BANDS, dtype=F32)
    feats = jnp.concatenate([t[:, None], jnp.cos(w[:, None] * f), -jnp.sin(w[:, None] * f)], axis=-1)
    h = jnp.sin(p['hy_freq'][0] * (feats @ p['hy_w1'] + p['hy_b1']))
    h = jnp.sin(p['hy_freq'][1] * (h @ p['hy_w2'] + p['hy_b2']))
    h = (h @ p['hy_w3']).astype(F32)
    dist = jnp.abs(j - L // 2) / (L // 2)
    h = h * (jnp.exp(-dist[:, None] * jnp.abs(p['hy_decay'].astype(F32))) + HY_SHIFT)
    h = h / jnp.sum(jnp.abs(h), axis=0, keepdims=True)
    return h.reshape(L, HY_ORDER, HY_WIDTH)


def _to_blocks(a):
    B, L, C = a.shape
    return a.reshape(B, L // HB, HB, C).transpose(3, 1, 0, 2)


def _from_blocks(a):
    C, nb, B, _ = a.shape
    return a.transpose(2, 1, 3, 0).reshape(B, nb * HB, C)


def _pad_filter(h):
    L = h.shape[0]
    nb = L // HB
    left = (HB - 1) - (L // 2 - (nb // 2) * HB)
    return jnp.pad(h.T, ((0, 0), (left, (nb + 2) * HB - L - left)))[:, None, :]


def _hyena_branch(u3, p):
    L = u3.shape[1]
    v, x1, x2 = jnp.split(_short_conv(u3, p['hy_conv_w']), 1 + HY_ORDER, axis=-1)
    h = _hyena_filters(L, p)
    z = _to_blocks(v)
    for o, gate in enumerate((x1, x2)):
        z = _hyena_order(z, _to_blocks(gate), _pad_filter(h[:, o]), p['hy_bias'][o])
    return _from_blocks(z)


def _rope_tables(L):
    quarter = DA_DK // 4
    freqs = ROPE_BASE ** (-jnp.arange(quarter, dtype=F32) / quarter)
    row = (jnp.arange(L) // GRID_W).astype(F32)[:, None] * freqs
    col = (jnp.arange(L) % GRID_W).astype(F32)[:, None] * freqs
    return jnp.cos(row), jnp.sin(row), jnp.cos(col), jnp.sin(col)


def _da_prep(t, g, tables):
    B, L, W = t.shape
    t = _rms_norm(t.reshape(B, L, N_QK, DA_DK), g)
    if tables is not None:
        cr, sr, cc, sc = [a[None, :, None, :] for a in tables]
        e = DA_DK // 4
        a1, a2, b1, b2 = t[..., :e], t[..., e:2 * e], t[..., 2 * e:3 * e], t[..., 3 * e:]
        t = jnp.concatenate([a1 * cr - a2 * sr, a1 * sr + a2 * cr, b1 * cc - b2 * sc, b1 * sc + b2 * cc], axis=-1)
    return t.reshape(B, L, W)


def _value_rows(v):
    B, Lk, _ = v.shape
    vt = v.reshape(B, Lk, DA_HEADS, DA_DV).transpose(0, 2, 3, 1)
    extra = jnp.concatenate([jnp.ones((B, DA_HEADS, 1, Lk), v.dtype),
                             jnp.zeros((B, DA_HEADS, VROWS - DA_DV - 1, Lk), v.dtype)], axis=2)
    return jnp.concatenate([vt, extra], axis=2).astype(BF16)


def _da_parts(da):
    w = DA_HEADS * 2 * DA_DK
    return da[..., :w], da[..., w:2 * w], da[..., 2 * w:]


def _group_weights(w_in):
    cols = []
    for names, width in PROJ_GROUPS:
        used = 0
        for nm in names:
            i = IN_NAMES.index(nm)
            cols.append(w_in[:, IN_OFFSETS[i]:IN_OFFSETS[i] + IN_WIDTHS[i]])
            used += IN_WIDTHS[i]
        if width > used:
            cols.append(jnp.zeros((w_in.shape[0], width - used), w_in.dtype))
    return jnp.concatenate(cols, axis=1).astype(BF16)


GLA_COLS_F, GLA_COLS_B = (0, 1, 2, 2, 3), (0, 1, 3, 2, 3)
HG_COLS_F, HG_COLS_B = (0, 1, 3, 5, 6), (0, 2, 4, 5, 6)


def _token_mixer(gc, gx, p, lam_init, last):
    gla_c, gla_x = _bidir(gc[0], gx[0], GLA_COLS_F, GLA_COLS_B, p['gla_norm_g'], dk=GLA_DK, dv=GLA_DV)
    hg_c, hg_x = _bidir(gc[2], gx[2], HG_COLS_F, HG_COLS_B, p['hg_norm_g'], dk=HG_DK, dv=HG_DV)
    (dqc, dkc, dvc), (dqx, dkx, dvx) = _da_parts(gc[3]), _da_parts(gx[3])
    tables = _rope_tables(dqx.shape[1])
    scale = DA_DK ** -0.5 * LOG2E
    qx = (_da_prep(dqx, p['da_qnorm_g'], tables) * scale).transpose(0, 2, 1)
    kx = _da_prep(dkx, p['da_knorm_g'], tables)
    kc = _da_prep(dkc, p['da_knorm_g'], None)
    k_all = jnp.concatenate([kc, kx], axis=1).astype(BF16)
    vt_all = _value_rows(jnp.concatenate([dvc, dvx], axis=1))
    lp = p['da_lam'].astype(F32)
    lam = (jnp.exp(jnp.sum(lp[0] * lp[1])) - jnp.exp(jnp.sum(lp[2] * lp[3])) + lam_init).reshape(1)
    gcol = (p['da_norm_g'].astype(F32) * (1 - lam_init)).reshape(DA_DV, 1)
    da_x = _diff_attention(qx, k_all, vt_all, lam, gcol, tq=256, tk=256)
    outs_x = (gla_x, _hyena_branch(gx[1], p), hg_x, da_x)
    if last:
        return None, outs_x
    Lc = kc.shape[1]
    qc = (_da_prep(dqc, p['da_qnorm_g'], None) * scale).transpose(0, 2, 1)
    da_c = _diff_attention(qc, k_all[:, :Lc], vt_all[..., :Lc], lam, gcol, tq=Lc, tk=Lc)
    return (gla_c, _hyena_branch(gc[1], p), hg_c, da_c), outs_x


def _expert_choice_moe(x, m, h, router, w1, w3, w2):
    B, L, D = h.shape
    cap = CAPACITY_FACTOR * L // N_EXPERTS
    aff = jax.nn.softmax((h @ router).astype(F32), axis=-1)
    g, idx = lax.top_k(aff.transpose(0, 2, 1), cap)
    xg = jax.vmap(lambda hb, ib: hb[ib])(h, idx)
    y = _expert_ffn(xg, g[..., None], w1, w3, w2)
    return _combine(idx.reshape(B, 1, N_EXPERTS * cap), y.reshape(B, N_EXPERTS * cap, D), x, m, tt=min(L, 512))


def kernel(x, c, ctx, c_ctx, ada_w, ada_b, norm1_g, norm2_g, w_in, gla_wa2, gla_ba, gla_norm_g,
           hy_conv_w, hy_w1, hy_b1, hy_w2, hy_b2, hy_w3, hy_freq, hy_decay, hy_bias,
           hg_lower, hg_norm_g, da_qnorm_g, da_knorm_g, da_lam, da_norm_g, w_branch, w_out,
           moe_router, moe_w1, moe_w3, moe_w2):
    B, L, D = x.shape
    P = jax.nn.softmax(hg_lower.astype(F32), axis=0)
    lower = jnp.cumsum(P, axis=0) - P[0]
    sc = jax.nn.silu(c)
    scc = jax.nn.silu(c_ctx)
    xc, xx = ctx, x
    for l in range(DEPTH):
        last = l == DEPTH - 1
        lam_init = 0.8 - 0.6 * math.exp(-0.3 * l)
        p = {'gla_wa2': gla_wa2[l], 'gla_ba': gla_ba[l], 'gla_norm_g': gla_norm_g[l],
             'hy_conv_w': hy_conv_w[l], 'hy_w1': hy_w1[l], 'hy_b1': hy_b1[l], 'hy_w2': hy_w2[l], 'hy_b2': hy_b2[l],
             'hy_w3': hy_w3[l], 'hy_freq': hy_freq[l], 'hy_decay': hy_decay[l], 'hy_bias': hy_bias[l],
             'hg_norm_g': hg_norm_g[l], 'da_qnorm_g': da_qnorm_g[l], 'da_knorm_g': da_knorm_g[l],
             'da_lam': da_lam[l], 'da_norm_g': da_norm_g[l]}
        mod_x = jnp.split((sc @ ada_w[l] + ada_b[l])[:, None, :], ADA_CHUNKS, axis=-1)
        mod_c1 = jnp.split((scc @ ada_w[l] + ada_b[l])[None, None, :], ADA_CHUNKS, axis=-1)
        mod_c = [jnp.broadcast_to(m, (B, 1, D)) for m in mod_c1]
        wg = _group_weights(w_in[l])
        wa, ba = _gate_weights(gla_wa2[l], gla_ba[l])
        lb = lower[l].reshape(1, 2 * HG_HEADS * HG_DK)
        wb = w_branch[l].astype(BF16)
        wo = w_out[l].astype(BF16)
        w1, w3, w2 = moe_w1[l], moe_w3[l], moe_w2[l]
        gx = _in_proj(xx, norm1_g[l], mod_x[0], mod_x[1], wg, wa, ba, lb, tm=256)
        gc = _in_proj(xc, norm1_g[l], mod_c[0], mod_c[1], wg, wa, ba, lb, tm=256)
        outs_c, outs_x = _token_mixer(gc, gx, p, lam_init, last)
        xx = _merge_out(outs_x, gx[4], wb, wo, xx, mod_x[2], tm=256)
        hx = _modulate(xx, norm2_g[l], mod_x[3], mod_x[4])
        xx = _expert_choice_moe(xx, mod_x[5], hx, moe_router[l], w1, w3, w2)
        if not last:
            xc = _merge_out(outs_c, gc[4], wb, wo, xc, mod_c[2], tm=256)
            hc = _modulate(xc, norm2_g[l], mod_c[3], mod_c[4])
            xc = _expert_choice_moe(xc, mod_c[5], hc, moe_router[l], w1, w3, w2)
    return xx
```

```python
import functools
import math

import jax
import jax.numpy as jnp
import numpy as np
from jax import lax
from jax.experimental import pallas as pl
from jax.experimental.pallas import tpu as pltpu

D_MODEL = 1024
DEPTH = 2
GRID_W = 64
N_BRANCH = 4
MIX_W = 256
GLA_HEADS = 4
GLA_DK = 32
GLA_DV = 64
GLA_RANK = 16
GLA_TAU = 16.0
HY_WIDTH = 256
HY_ORDER = 2
HY_BANDS = 16
HY_SHIFT = 0.05
HG_HEADS = 4
HG_DK = 64
HG_DV = 64
DA_HEADS = 4
DA_DK = 32
DA_DV = 64
ROPE_BASE = 10000.0
N_EXPERTS = 16
EXPERT_FF = 1024
CAPACITY_FACTOR = 2
ADA_CHUNKS = 6
EPS = 1e-6
F_TINY = 1e-20

IN_NAMES = ('gla_q', 'gla_k', 'gla_v', 'gla_af', 'gla_ab', 'gla_g', 'hy',
            'hg_q', 'hg_ff', 'hg_fb', 'hg_i', 'hg_g', 'da_q', 'da_k', 'da_v', 'merge')
IN_WIDTHS = (GLA_HEADS * GLA_DK, GLA_HEADS * GLA_DK, GLA_HEADS * GLA_DV, GLA_RANK, GLA_RANK, GLA_HEADS * GLA_DV,
             (1 + HY_ORDER) * HY_WIDTH,
             HG_HEADS * HG_DK, HG_HEADS * HG_DK, HG_HEADS * HG_DK, HG_HEADS * HG_DV, HG_HEADS * HG_DV,
             DA_HEADS * 2 * DA_DK, DA_HEADS * 2 * DA_DK, DA_HEADS * DA_DV,
             N_BRANCH * D_MODEL)
IN_OFFSETS = tuple(int(v) for v in np.cumsum((0,) + IN_WIDTHS)[:-1])

PROJ_GROUPS = (
    (('gla_q', 'gla_k', 'gla_v', 'gla_g', 'gla_af', 'gla_ab'), 896),
    (('hy',), 768),
    (('hg_q', 'hg_ff', 'hg_fb', 'hg_i', 'hg_g'), 1280),
    (('da_q', 'da_k', 'da_v'), 768),
    (('merge',), 4096),
)
PROJ_WIDTHS = tuple(w for _, w in PROJ_GROUPS)
LANES = 128
OUT_WIDTHS = (4 * GLA_HEADS * GLA_DK + 2 * GLA_HEADS * GLA_DV, PROJ_WIDTHS[1],
              5 * HG_HEADS * HG_DK + 2 * HG_HEADS * HG_DV, PROJ_WIDTHS[3], PROJ_WIDTHS[4])
OUT_DTYPES = (jnp.float32,) * 4 + (jnp.bfloat16,)
VMEM_LIMIT = 56 * 1024 * 1024

BF16 = jnp.bfloat16
F32 = jnp.float32
N_HEADS = 4
RC = 64
N_QK = 2 * DA_HEADS
VROWS = DA_DV + 8
LOG2E = 1.4426950408889634
HB = 256


def _split_bf16(a):
    hi = a.astype(BF16)
    return hi, (a - hi.astype(F32)).astype(BF16)


def _in_proj_kernel(x_ref, g_ref, sh_ref, sc_ref, w_ref, wa_ref, ba_ref, lb_ref,
                    gla_ref, hy_ref, hg_ref, da_ref, mg_ref):
    x = x_ref[0]
    ms = jnp.mean(x * x, axis=-1, keepdims=True)
    h = x * lax.rsqrt(ms + EPS) * g_ref[...]
    h = (h * (1.0 + sc_ref[0]) + sh_ref[0]).astype(BF16)
    offs = np.cumsum((0,) + PROJ_WIDTHS)

    def proj(k):
        return jnp.dot(h, w_ref[:, int(offs[k]):int(offs[k + 1])], preferred_element_type=F32)

    r = proj(0)
    qk = GLA_HEADS * GLA_DK
    vw = GLA_HEADS * GLA_DV
    gla_ref[0, :, 0:qk] = r[:, 0:qk] * (GLA_DK ** -0.5)
    gla_ref[0, :, qk:2 * qk] = r[:, qk:2 * qk]
    a_hi, a_lo = _split_bf16(r[:, 2 * qk + 2 * vw:])
    z = (jnp.dot(a_hi, wa_ref[0], preferred_element_type=F32) + jnp.dot(a_lo, wa_ref[0], preferred_element_type=F32)
         + jnp.dot(a_hi, wa_ref[1], preferred_element_type=F32) + ba_ref[...])
    gla_ref[0, :, 2 * qk:4 * qk] = jax.nn.log_sigmoid(z) * (1.0 / GLA_TAU)
    gla_ref[0, :, 4 * qk:4 * qk + 2 * vw] = r[:, 2 * qk:2 * qk + 2 * vw]

    hy_ref[0] = proj(1)

    r = proj(2)
    hw = HG_HEADS * HG_DK
    q = r[:, 0:hw]
    hg_ref[0, :, 0:hw] = q * jax.nn.sigmoid(q)
    zf = r[:, hw:3 * hw]
    lb = lb_ref[...]
    hg_ref[0, :, hw:3 * hw] = (1.0 - lb) * jax.nn.sigmoid(-zf)
    hg_ref[0, :, 3 * hw:5 * hw] = jnp.log(jnp.maximum(lb + (1.0 - lb) * jax.nn.sigmoid(zf), F_TINY))
    hg_ref[0, :, 5 * hw:7 * hw] = r[:, 3 * hw:5 * hw]

    da_ref[0] = proj(3)
    mg_ref[0] = proj(4).astype(mg_ref.dtype)


def _in_proj(x, g, shift, scale, w_groups, wa, ba, lb, tm):
    B, L, D = x.shape
    nw = w_groups.shape[1]
    const2 = lambda b, i: (0, 0)
    return pl.pallas_call(
        _in_proj_kernel,
        grid=(B, L // tm),
        in_specs=[
            pl.BlockSpec((1, tm, D), lambda b, i: (b, i, 0)),
            pl.BlockSpec((1, D), const2),
            pl.BlockSpec((1, 1, D), lambda b, i: (b, 0, 0)),
            pl.BlockSpec((1, 1, D), lambda b, i: (b, 0, 0)),
            pl.BlockSpec((D, nw), const2, pipeline_mode=pl.Buffered(1)),
            pl.BlockSpec(wa.shape, lambda b, i: (0, 0, 0)),
            pl.BlockSpec(ba.shape, const2),
            pl.BlockSpec(lb.shape, const2),
        ],
        out_specs=[pl.BlockSpec((1, tm, w), lambda b, i: (b, i, 0)) for w in OUT_WIDTHS],
        out_shape=[jax.ShapeDtypeStruct((B, L, w), dt) for w, dt in zip(OUT_WIDTHS, OUT_DTYPES)],
        compiler_params=pltpu.CompilerParams(
            dimension_semantics=("parallel", "parallel"), vmem_limit_bytes=VMEM_LIMIT),
        name="in_proj",
    )(x, g.reshape(1, D), shift, scale, w_groups, wa, ba, lb)


def _merge_kernel(o0_ref, o1_ref, o2_ref, o3_ref, gate_ref, wb_ref, wo_ref, x_ref, m_ref, g_ref, sh_ref, sc_ref,
                  out_ref, h_ref):
    D = D_MODEL
    acc = None
    for i, o_ref in enumerate((o0_ref, o1_ref, o2_ref, o3_ref)):
        t = jnp.dot(o_ref[0].astype(BF16), wb_ref[i], preferred_element_type=F32)
        t = jax.nn.sigmoid(gate_ref[0, :, i * D:(i + 1) * D].astype(F32)) * t
        acc = t if acc is None else acc + t
    mx = jnp.dot(acc.astype(BF16), wo_ref[...], preferred_element_type=F32)
    out = x_ref[0] + m_ref[0] * mx
    out_ref[0] = out
    ms = jnp.mean(out * out, axis=-1, keepdims=True)
    h_ref[0] = out * lax.rsqrt(ms + EPS) * g_ref[...] * (1.0 + sc_ref[0]) + sh_ref[0]


def _merge_out(outs, gate_cols, w_branch, w_out, x, m, g, shift, scale, tm):
    B, L, D = x.shape
    tok = pl.BlockSpec((1, tm, D), lambda b, i: (b, i, 0))
    per_sample = pl.BlockSpec((1, 1, D), lambda b, i: (b, 0, 0))
    return pl.pallas_call(
        _merge_kernel,
        grid=(B, L // tm),
        in_specs=[pl.BlockSpec((1, tm, MIX_W), lambda b, i: (b, i, 0)) for _ in range(N_BRANCH)] + [
            pl.BlockSpec((1, tm, N_BRANCH * D), lambda b, i: (b, i, 0)),
            pl.BlockSpec((N_BRANCH, MIX_W, D), lambda b, i: (0, 0, 0)),
            pl.BlockSpec((D, D), lambda b, i: (0, 0)),
            tok, per_sample,
            pl.BlockSpec((1, D), lambda b, i: (0, 0)), per_sample, per_sample,
        ],
        out_specs=[tok, tok],
        out_shape=[jax.ShapeDtypeStruct((B, L, D), F32), jax.ShapeDtypeStruct((B, L, D), F32)],
        compiler_params=pltpu.CompilerParams(
            dimension_semantics=("parallel", "parallel"), vmem_limit_bytes=VMEM_LIMIT),
        name="merge_out",
    )(*outs, gate_cols, w_branch, w_out, x, m, g.reshape(1, D), shift, scale)


def _expert_ffn_kernel(x_ref, g_ref, w1_ref, w3_ref, w2_ref, out_ref, w1_scr, w3_scr, w2_scr):
    @pl.when(pl.program_id(1) == 0)
    def _():
        w1_scr[...] = w1_ref[0].astype(BF16)
        w3_scr[...] = w3_ref[0].astype(BF16)
        w2_scr[...] = w2_ref[0].astype(BF16)

    x = x_ref[0, 0].astype(BF16)
    a = jnp.dot(x, w1_scr[...], preferred_element_type=F32)
    b = jnp.dot(x, w3_scr[...], preferred_element_type=F32)
    h = (a * jax.nn.sigmoid(a) * b).astype(BF16)
    y = jnp.dot(h, w2_scr[...], preferred_element_type=F32)
    out_ref[0, 0] = (y * g_ref[0, 0]).astype(out_ref.dtype)


def _combine_kernel(idx_ref, y_ref, x_ref, m_ref, out_ref, *, ks):
    tt = out_ref.shape[1]
    S = y_ref.shape[1]
    base = pl.program_id(1) * tt
    acc = None
    for s0 in range(0, S, ks):
        tok = lax.broadcasted_iota(jnp.int32, (tt, ks), 0) + base
        onehot = jnp.where(tok == idx_ref[0, :, s0:s0 + ks], 1.0, 0.0).astype(BF16)
        part = jnp.dot(onehot, y_ref[0, s0:s0 + ks, :], preferred_element_type=F32)
        acc = part if acc is None else acc + part
    out_ref[0] = x_ref[0] + m_ref[0] * acc


def _combine(idx, y, x, m, tt):
    B, L, D = x.shape
    S = y.shape[1]
    return pl.pallas_call(
        functools.partial(_combine_kernel, ks=min(S, 2048)),
        grid=(B, L // tt),
        in_specs=[
            pl.BlockSpec((1, 1, S), lambda b, i: (b, 0, 0)),
            pl.BlockSpec((1, S, D), lambda b, i: (b, 0, 0)),
            pl.BlockSpec((1, tt, D), lambda b, i: (b, i, 0)),
            pl.BlockSpec((1, 1, D), lambda b, i: (b, 0, 0)),
        ],
        out_specs=pl.BlockSpec((1, tt, D), lambda b, i: (b, i, 0)),
        out_shape=jax.ShapeDtypeStruct((B, L, D), F32),
        compiler_params=pltpu.CompilerParams(
            dimension_semantics=("parallel", "parallel"), vmem_limit_bytes=VMEM_LIMIT),
        name="moe_combine",
    )(idx, y, x, m)


def _expert_ffn(xg, g, w1, w3, w2):
    B, E, cap, D = xg.shape
    F = w1.shape[-1]
    return pl.pallas_call(
        _expert_ffn_kernel,
        grid=(E, B),
        in_specs=[
            pl.BlockSpec((1, 1, cap, D), lambda e, b: (b, e, 0, 0)),
            pl.BlockSpec((1, 1, cap, 1), lambda e, b: (b, e, 0, 0)),
            pl.BlockSpec((1, D, F), lambda e, b: (e, 0, 0)),
            pl.BlockSpec((1, D, F), lambda e, b: (e, 0, 0)),
            pl.BlockSpec((1, F, D), lambda e, b: (e, 0, 0)),
        ],
        out_specs=pl.BlockSpec((1, 1, cap, D), lambda e, b: (b, e, 0, 0)),
        out_shape=jax.ShapeDtypeStruct((B, E, cap, D), BF16),
        scratch_shapes=[pltpu.VMEM((D, F), BF16), pltpu.VMEM((D, F), BF16), pltpu.VMEM((F, D), BF16)],
        compiler_params=pltpu.CompilerParams(
            dimension_semantics=("parallel", "arbitrary"), vmem_limit_bytes=VMEM_LIMIT),
        name="expert_ffn",
    )(xg, g, w1, w3, w2)


def _recur_tables(C, rev):
    pos = (C - 1 - np.arange(C)) if rev else np.arange(C)
    row_of = np.argsort(pos)
    levels = int(np.log2(C))
    diff = np.zeros((levels + 1, C, C), np.float32)
    mask = np.zeros((levels + 1, C, C), np.float32)
    for t in range(C):
        p = pos[t]
        diff[0, t, row_of[:p + 1]] = 1.0
        mask[0, t, t] = 1.0
        for l in range(1, levels + 1):
            n, m = 2 ** l, 2 ** (l - 1)
            off = p % n
            mid = p - off + m - 1
            if off >= m:
                diff[l, t, row_of[mid + 1:p + 1]] = 1.0
                mask[l, t, row_of[p - off:mid + 1]] = 1.0
            else:
                diff[l, t, row_of[p + 1:mid + 1]] = 1.0
    return (jnp.asarray(np.tile(diff.reshape((levels + 1) * C, C), (1, 3)), BF16),
            jnp.asarray(np.tile(mask, (1, 1, N_HEADS))))


def _recur_kernel(*refs, rev, C, nsub, dk, dv, finish):
    if finish:
        (q_ref, k_ref, v_ref, g_ref, s0_ref, d_ref, m_ref, prev_ref, gate_ref, gn_ref,
         o_ref, sfin_ref, s_scr, e_scr, a_scr) = refs
    else:
        q_ref, k_ref, v_ref, g_ref, s0_ref, d_ref, m_ref, o_ref, sfin_ref, s_scr, e_scr, a_scr = refs
    HK = N_HEADS * dk
    HV = N_HEADS * dv
    i = pl.program_id(1)
    if finish:
        head_mean = jnp.where(lax.broadcasted_iota(jnp.int32, (HV, HV), 0) // dv
                              == lax.broadcasted_iota(jnp.int32, (HV, HV), 1) // dv, 1.0 / dv, 0.0).astype(BF16)

    @pl.when(i == 0)
    def _():
        s_scr[...] = s0_ref[0]

    head_k = [(lax.broadcasted_iota(jnp.int32, (1, HK), 1) // dk == h).astype(BF16) for h in range(N_HEADS)]
    head_v = [(lax.broadcasted_iota(jnp.int32, (1, HV), 1) // dv == h).astype(BF16) for h in range(N_HEADS)]
    bd = (lax.broadcasted_iota(jnp.int32, (HV, HK), 0) // dv
          == lax.broadcasted_iota(jnp.int32, (HV, HK), 1) // dk)
    levels = m_ref.shape[0] - 1

    def stack_heads(x, head):
        xb = x.astype(BF16)
        return jnp.concatenate([xb * head[h] for h in range(N_HEADS)], axis=0)

    def nt(a, b):
        return lax.dot_general(a, b, (((1,), (1,)), ((), ())), preferred_element_type=F32)

    order = [(nsub - 1 - j) if rev else j for j in range(nsub)]
    for c in order:
        g = g_ref[0, c * C:(c + 1) * C, :]
        g_hi = g.astype(BF16)
        r1 = g - g_hi.astype(F32)
        g_mid = r1.astype(BF16)
        g_lo = (r1 - g_mid.astype(F32)).astype(BF16)
        e_scr[c] = jnp.dot(d_ref[...], jnp.concatenate([g_hi, g_mid, g_lo], axis=0), preferred_element_type=F32)
    for c in order:
        q = q_ref[0, c * C:(c + 1) * C, :]
        k = k_ref[0, c * C:(c + 1) * C, :]
        a = nt(q.astype(BF16), stack_heads(k, head_k)) * m_ref[0]
        for lvl in range(1, levels + 1):
            x = jnp.exp(e_scr[c, lvl * C:(lvl + 1) * C, :])
            a = a + nt((q * x).astype(BF16), stack_heads(k * x, head_k)) * m_ref[lvl]
        a_scr[c] = a.astype(BF16)
    for c in order:
        sl = slice(c * C, (c + 1) * C)
        q = q_ref[0, sl, :]
        k = k_ref[0, sl, :]
        v = v_ref[0, sl, :]
        b = e_scr[c, 0:C, :]
        st = s_scr[...]
        o = jnp.dot(a_scr[c], stack_heads(v, head_v), preferred_element_type=F32)
        o = o + nt((q * jnp.exp(b)).astype(BF16), st.astype(BF16))
        if finish:
            o = o + prev_ref[0, sl, :]
            sq_hi, sq_lo = _split_bf16(o * o)
            ms = (jnp.dot(sq_hi, head_mean, preferred_element_type=F32)
                  + jnp.dot(sq_lo, head_mean, preferred_element_type=F32))
            gate = gate_ref[0, sl, :]
            o = o * lax.rsqrt(ms + EPS) * gn_ref[...] * (gate * jax.nn.sigmoid(gate))
        o_ref[0, sl, :] = o
        b_end = b[0:1, :] if rev else b[C - 1:C, :]
        kend = (k * jnp.exp(b_end - b)).astype(BF16)
        upd = lax.dot_general(v.astype(BF16), kend, (((0,), (0,)), ((), ())), preferred_element_type=F32)
        s_scr[...] = st * jnp.exp(b_end) + jnp.where(bd, upd, 0.0)

    @pl.when(i == pl.num_programs(1) - 1)
    def _():
        sfin_ref[0] = s_scr[...]


def _recurrence(pack, cols, s0, prev, gnorm, *, rev, dk, dv, tb):
    B, L, _ = pack.shape
    HK, HV = N_HEADS * dk, N_HEADS * dv
    C = RC
    nblk = L // tb
    diff, masks = _recur_tables(C, rev)
    cq, ck, cg, cv, cgate = cols

    def tok(col):
        return (lambda b, i: (b, nblk - 1 - i, col)) if rev else (lambda b, i: (b, i, col))

    in_specs = [
        pl.BlockSpec((1, tb, HK), tok(cq)), pl.BlockSpec((1, tb, HK), tok(ck)),
        pl.BlockSpec((1, tb, HV), tok(cv)), pl.BlockSpec((1, tb, HK), tok(cg)),
        pl.BlockSpec((1, HV, HK), lambda b, i: (b, 0, 0)),
        pl.BlockSpec(diff.shape, lambda b, i: (0, 0)),
        pl.BlockSpec(masks.shape, lambda b, i: (0, 0, 0)),
    ]
    args = [pack, pack, pack, pack, s0, diff, masks]
    if prev is not None:
        in_specs += [pl.BlockSpec((1, tb, HV), tok(0)), pl.BlockSpec((1, tb, HV), tok(cgate)),
                     pl.BlockSpec((1, HV), lambda b, i: (0, 0))]
        args += [prev, pack, gnorm]
    kern = functools.partial(_recur_kernel, rev=rev, C=C, nsub=tb // C, dk=dk, dv=dv, finish=prev is not None)
    return pl.pallas_call(
        kern,
        grid=(B, nblk),
        in_specs=in_specs,
        out_specs=[pl.BlockSpec((1, tb, HV), tok(0)), pl.BlockSpec((1, HV, HK), lambda b, i: (b, 0, 0))],
        out_shape=[jax.ShapeDtypeStruct((B, L, HV), F32), jax.ShapeDtypeStruct((B, HV, HK), F32)],
        scratch_shapes=[pltpu.VMEM((HV, HK), F32), pltpu.VMEM((tb // C, diff.shape[0], HK), F32),
                        pltpu.VMEM((tb // C, C, N_HEADS * C), BF16)],
        compiler_params=pltpu.CompilerParams(
            dimension_semantics=("parallel", "arbitrary"), vmem_limit_bytes=VMEM_LIMIT),
        name="recur_rev" if rev else "recur_fwd",
    )(*args)


def _bidir(pack_c, pack_l, cols_f, cols_b, gnorm, *, dk, dv):
    B, Lc, _ = pack_c.shape
    s0 = jnp.zeros((B, N_HEADS * dv, N_HEADS * dk), F32)
    gn = jnp.tile(gnorm.astype(F32), N_HEADS).reshape(1, N_HEADS * dv)
    kw = dict(dk=dk, dv=dv)
    oc_f, sc_f = _recurrence(pack_c, cols_f, s0, None, None, rev=False, tb=Lc, **kw)
    ol_f, _ = _recurrence(pack_l, cols_f, sc_f, None, None, rev=False, tb=256, **kw)
    oc, sc_b = _recurrence(pack_c, cols_b, s0, oc_f, gn, rev=True, tb=Lc, **kw)
    ol, _ = _recurrence(pack_l, cols_b, sc_b, ol_f, gn, rev=True, tb=256, **kw)
    return oc, ol


def _attn_kernel(lam_ref, qt_ref, k_ref, vt_ref, g_ref, o_ref, qm_scr, s_scr, p_scr, m_scr, a_scr, acc_scr, *,
                 first, tk):
    W = qt_ref.shape[1]
    tq = qt_ref.shape[2]
    Lk = k_ref.shape[1]
    qt = qt_ref[0]
    row_pair = lax.broadcasted_iota(jnp.int32, (W, tq), 0) // DA_DK
    for j in range(N_QK):
        qm_scr[j] = jnp.where(row_pair == j, qt, 0.0).astype(BF16)
    m_scr[...] = jnp.full(m_scr.shape, -jnp.inf, F32)
    acc_scr[...] = jnp.zeros(acc_scr.shape, F32)

    def block(start, size):
        ks = pl.ds(start, size)
        kblk = k_ref[0, ks, :]
        for j in range(N_QK):
            s_scr[j, :size] = jnp.dot(kblk, qm_scr[j], preferred_element_type=F32)
        for j in range(N_QK):
            s = s_scr[j, :size]
            m_old = m_scr[j]
            m_new = jnp.maximum(m_old, jnp.max(s, axis=0, keepdims=True))
            a_scr[j] = jnp.exp2(m_old - m_new)
            p_scr[j, :size] = jnp.exp2(s - m_new).astype(BF16)
            m_scr[j] = m_new
        for j in range(N_QK):
            pv = jnp.dot(vt_ref[0, j // 2, :, ks], p_scr[j, :size], preferred_element_type=F32)
            acc_scr[j] = a_scr[j] * acc_scr[j] + pv

    block(0, first)

    def body(i, carry):
        block(pl.multiple_of(first + i * tk, math.gcd(first, tk)), tk)
        return carry

    lax.fori_loop(0, (Lk - first) // tk, body, 0)
    lam = lam_ref[0]
    outs = []
    for h in range(DA_HEADS):
        a1, a2 = acc_scr[2 * h], acc_scr[2 * h + 1]
        o = a1[:DA_DV] / a1[DA_DV:DA_DV + 1] - lam * (a2[:DA_DV] / a2[DA_DV:DA_DV + 1])
        ms = jnp.mean(o * o, axis=0, keepdims=True)
        outs.append(o * lax.rsqrt(ms + EPS) * g_ref[...])
    o_ref[0] = jnp.concatenate(outs, axis=0).T


def _diff_attention(qt, k, vt, lam, gcol, *, tq, first, tk):
    B, W, Lq = qt.shape
    Lk = k.shape[1]
    assert (Lk - first) % tk == 0
    rows = max(first, tk)
    HV = DA_HEADS * DA_DV
    return pl.pallas_call(
        functools.partial(_attn_kernel, first=first, tk=tk),
        grid=(B, Lq // tq),
        in_specs=[
            pl.BlockSpec(memory_space=pltpu.SMEM),
            pl.BlockSpec((1, W, tq), lambda b, i: (b, 0, i)),
            pl.BlockSpec((1, Lk, W), lambda b, i: (b, 0, 0)),
            pl.BlockSpec((1, DA_HEADS, VROWS, Lk), lambda b, i: (b, 0, 0, 0)),
            pl.BlockSpec((DA_DV, 1), lambda b, i: (0, 0)),
        ],
        out_specs=pl.BlockSpec((1, tq, HV), lambda b, i: (b, i, 0)),
        out_shape=jax.ShapeDtypeStruct((B, Lq, HV), F32),
        scratch_shapes=[
            pltpu.VMEM((N_QK, W, tq), BF16),
            pltpu.VMEM((N_QK, rows, tq), F32),
            pltpu.VMEM((N_QK, rows, tq), BF16),
            pltpu.VMEM((N_QK, 1, tq), F32),
            pltpu.VMEM((N_QK, 1, tq), F32),
            pltpu.VMEM((N_QK, VROWS, tq), F32),
        ],
        compiler_params=pltpu.CompilerParams(
            dimension_semantics=("parallel", "parallel"), vmem_limit_bytes=VMEM_LIMIT),
        name="diff_attn",
    )(lam, qt, k, vt, gcol)


def _hyena_kernel(bias_ref, z_ref, gate_ref, hp_ref, o_ref, acc_scr, *, nb, nbatch):
    c = pl.program_id(0)
    half = nb // 2
    rows = nb * nbatch
    z = z_ref[0].reshape(rows, HB)
    zb = z.astype(BF16)
    hrow = hp_ref[0]
    acc_scr[...] = jnp.zeros(acc_scr.shape, F32)
    for off in range(-half, half + 1):
        lo = (off + half) * HB
        r = jnp.broadcast_to(hrow[:, lo:lo + 2 * HB], (HB, 2 * HB))
        w = pltpu.roll(r, 1, 1, stride=1, stride_axis=0)[:, HB:].astype(BF16)
        s0, s1 = max(0, -off), nb - max(0, off)
        if s1 <= s0:
            continue
        src = slice(s0 * nbatch, s1 * nbatch)
        dst = slice((s0 + off) * nbatch, (s1 + off) * nbatch)
        acc_scr[dst, :] += jnp.dot(zb[src], w, preferred_element_type=F32)
    y = acc_scr[...] + bias_ref[c] * z
    o_ref[0] = (gate_ref[0].reshape(rows, HB) * y).reshape(nb, nbatch, HB)


def _hyena_order(z, gate, hp, bias):
    C, nb, B, _ = z.shape
    blk = pl.BlockSpec((1, nb, B, HB), lambda c: (c, 0, 0, 0))
    return pl.pallas_call(
        functools.partial(_hyena_kernel, nb=nb, nbatch=B),
        grid=(C,),
        in_specs=[
            pl.BlockSpec(memory_space=pltpu.SMEM),
            blk, blk,
            pl.BlockSpec((1, 1, hp.shape[-1]), lambda c: (c, 0, 0)),
        ],
        out_specs=blk,
        out_shape=jax.ShapeDtypeStruct(z.shape, F32),
        scratch_shapes=[pltpu.VMEM((nb * B, HB), F32)],
        compiler_params=pltpu.CompilerParams(
            dimension_semantics=("parallel",), vmem_limit_bytes=VMEM_LIMIT),
        name="hyena_conv",
    )(bias, z, gate, hp)


def _rms_norm(x, g):
    xf = x.astype(F32)
    y = xf * lax.rsqrt(jnp.mean(xf * xf, axis=-1, keepdims=True) + EPS)
    return (y * g.astype(F32)).astype(x.dtype)


def _modulate(x, g, shift, scale):
    return _rms_norm(x, g) * (1 + scale) + shift


def _short_conv(u, w):
    up = jnp.pad(u, ((0, 0), (1, 1), (0, 0)))
    return up[:, :-2] * w[0] + up[:, 1:-1] * w[1] + up[:, 2:] * w[2]


def _gate_weights(wa2, ba):
    qk = GLA_HEADS * GLA_DK
    w = jnp.zeros((LANES, 2 * qk), F32)
    w = w.at[:GLA_RANK, :qk].set(wa2[0].astype(F32)).at[GLA_RANK:2 * GLA_RANK, qk:].set(wa2[1].astype(F32))
    hi = w.astype(BF16)
    lo = (w - hi.astype(F32)).astype(BF16)
    return jnp.stack([hi, lo]), ba.astype(F32).reshape(1, 2 * qk)


def _hyena_filters(L, p):
    j = jnp.arange(L, dtype=F32)
    t = j / max(L - 1, 1)
    w = 2 * math.pi * j / L
    f = jnp.linspace(1e-4, HY_BANDS - 1, HY_BANDS, dtype=F32)
    feats = jnp.concatenate([t[:, None], jnp.cos(w[:, None] * f), -jnp.sin(w[:, None] * f)], axis=-1)
    h = jnp.sin(p['hy_freq'][0] * (feats @ p['hy_w1'] + p['hy_b1']))
    h = jnp.sin(p['hy_freq'][1] * (h @ p['hy_w2'] + p['hy_b2']))
    h = (h @ p['hy_w3']).astype(F32)
    dist = jnp.abs(j - L // 2) / (L // 2)
    h = h * (jnp.exp(-dist[:, None] * jnp.abs(p['hy_decay'].astype(F32))) + HY_SHIFT)
    h = h / jnp.sum(jnp.abs(h), axis=0, keepdims=True)
    return h.reshape(L, HY_ORDER, HY_WIDTH)


def _to_blocks(a):
    B, L, C = a.shape
    return a.reshape(B, L // HB, HB, C).transpose(3, 1, 0, 2)


def _from_blocks(a):
    C, nb, B, _ = a.shape
    return a.transpose(2, 1, 3, 0).reshape(B, nb * HB, C)


def _pad_filter(h):
    L = h.shape[0]
    nb = L // HB
    left = (HB - 1) - (L // 2 - (nb // 2) * HB)
    return jnp.pad(h.T, ((0, 0), (left, (nb + 2) * HB - L - left)))[:, None, :]


def _hyena_branch(u3, p):
    L = u3.shape[1]
    v, x1, x2 = jnp.split(_short_conv(u3, p['hy_conv_w']), 1 + HY_ORDER, axis=-1)
    h = _hyena_filters(L, p)
    z = _to_blocks(v)
    for o, gate in enumerate((x1, x2)):
        z = _hyena_order(z, _to_blocks(gate), _pad_filter(h[:, o]), p['hy_bias'][o])
    return _from_blocks(z)


def _rope_tables(L):
    quarter = DA_DK // 4
    freqs = ROPE_BASE ** (-jnp.arange(quarter, dtype=F32) / quarter)
    row = (jnp.arange(L) // GRID_W).astype(F32)[:, None] * freqs
    col = (jnp.arange(L) % GRID_W).astype(F32)[:, None] * freqs
    return jnp.cos(row), jnp.sin(row), jnp.cos(col), jnp.sin(col)


def _da_prep(t, g, tables):
    B, L, W = t.shape
    t = _rms_norm(t.reshape(B, L, N_QK, DA_DK), g)
    if tables is not None:
        cr, sr, cc, sc = [a[None, :, None, :] for a in tables]
        e = DA_DK // 4
        a1, a2, b1, b2 = t[..., :e], t[..., e:2 * e], t[..., 2 * e:3 * e], t[..., 3 * e:]
        t = jnp.concatenate([a1 * cr - a2 * sr, a1 * sr + a2 * cr, b1 * cc - b2 * sc, b1 * sc + b2 * cc], axis=-1)
    return t.reshape(B, L, W)


def _value_rows(v):
    B, Lk, _ = v.shape
    vt = v.reshape(B, Lk, DA_HEADS, DA_DV).transpose(0, 2, 3, 1)
    extra = jnp.concatenate([jnp.ones((B, DA_HEADS, 1, Lk), v.dtype),
                             jnp.zeros((B, DA_HEADS, VROWS - DA_DV - 1, Lk), v.dtype)], axis=2)
    return jnp.concatenate([vt, extra], axis=2).astype(BF16)


def _da_parts(da):
    w = DA_HEADS * 2 * DA_DK
    return da[..., :w], da[..., w:2 * w], da[..., 2 * w:]


def _group_weights(w_in):
    cols = []
    for names, width in PROJ_GROUPS:
        used = 0
        for nm in names:
            i = IN_NAMES.index(nm)
            cols.append(w_in[:, IN_OFFSETS[i]:IN_OFFSETS[i] + IN_WIDTHS[i]])
            used += IN_WIDTHS[i]
        if width > used:
            cols.append(jnp.zeros((w_in.shape[0], width - used), w_in.dtype))
    return jnp.concatenate(cols, axis=1).astype(BF16)


GLA_COLS_F, GLA_COLS_B = (0, 1, 2, 2, 3), (0, 1, 3, 2, 3)
HG_COLS_F, HG_COLS_B = (0, 1, 3, 5, 6), (0, 2, 4, 5, 6)


def _token_mixer(gc, gx, p, lam_init, last):
    gla_c, gla_x = _bidir(gc[0], gx[0], GLA_COLS_F, GLA_COLS_B, p['gla_norm_g'], dk=GLA_DK, dv=GLA_DV)
    hg_c, hg_x = _bidir(gc[2], gx[2], HG_COLS_F, HG_COLS_B, p['hg_norm_g'], dk=HG_DK, dv=HG_DV)
    (dqc, dkc, dvc), (dqx, dkx, dvx) = _da_parts(gc[3]), _da_parts(gx[3])
    tables = _rope_tables(dqx.shape[1])
    scale = DA_DK ** -0.5 * LOG2E
    qx = (_da_prep(dqx, p['da_qnorm_g'], tables) * scale).transpose(0, 2, 1)
    kx = _da_prep(dkx, p['da_knorm_g'], tables)
    kc = _da_prep(dkc, p['da_knorm_g'], None)
    k_all = jnp.concatenate([kc, kx], axis=1).astype(BF16)
    vt_all = _value_rows(jnp.concatenate([dvc, dvx], axis=1))
    lp = p['da_lam'].astype(F32)
    lam = (jnp.exp(jnp.sum(lp[0] * lp[1])) - jnp.exp(jnp.sum(lp[2] * lp[3])) + lam_init).reshape(1)
    gcol = (p['da_norm_g'].astype(F32) * (1 - lam_init)).reshape(DA_DV, 1)
    Lc = kc.shape[1]
    da_x = _diff_attention(qx, k_all, vt_all, lam, gcol, tq=256, first=Lc, tk=512)
    outs_x = (gla_x, _hyena_branch(gx[1], p), hg_x, da_x)
    if last:
        return None, outs_x
    qc = (_da_prep(dqc, p['da_qnorm_g'], None) * scale).transpose(0, 2, 1)
    da_c = _diff_attention(qc, k_all[:, :Lc], vt_all[..., :Lc], lam, gcol, tq=Lc, first=Lc, tk=Lc)
    return (gla_c, _hyena_branch(gc[1], p), hg_c, da_c), outs_x


def _expert_choice_moe(x, m, h, router, w1, w3, w2):
    B, L, D = h.shape
    cap = CAPACITY_FACTOR * L // N_EXPERTS
    aff = jax.nn.softmax((h @ router).astype(F32), axis=-1)
    g, idx = lax.top_k(aff.transpose(0, 2, 1), cap)
    xg = jax.vmap(lambda hb, ib: hb[ib])(h, idx)
    y = _expert_ffn(xg, g[..., None], w1, w3, w2)
    return _combine(idx.reshape(B, 1, N_EXPERTS * cap), y.reshape(B, N_EXPERTS * cap, D), x, m, tt=min(L, 512))


def kernel(x, c, ctx, c_ctx, ada_w, ada_b, norm1_g, norm2_g, w_in, gla_wa2, gla_ba, gla_norm_g,
           hy_conv_w, hy_w1, hy_b1, hy_w2, hy_b2, hy_w3, hy_freq, hy_decay, hy_bias,
           hg_lower, hg_norm_g, da_qnorm_g, da_knorm_g, da_lam, da_norm_g, w_branch, w_out,
           moe_router, moe_w1, moe_w3, moe_w2):
    B, L, D = x.shape
    P = jax.nn.softmax(hg_lower.astype(F32), axis=0)
    lower = jnp.cumsum(P, axis=0) - P[0]
    sc = jax.nn.silu(c)
    scc = jax.nn.silu(c_ctx)
    xc, xx = ctx, x
    for l in range(DEPTH):
        last = l == DEPTH - 1
        lam_init = 0.8 - 0.6 * math.exp(-0.3 * l)
        p = {'gla_wa2': gla_wa2[l], 'gla_ba': gla_ba[l], 'gla_norm_g': gla_norm_g[l],
             'hy_conv_w': hy_conv_w[l], 'hy_w1': hy_w1[l], 'hy_b1': hy_b1[l], 'hy_w2': hy_w2[l], 'hy_b2': hy_b2[l],
             'hy_w3': hy_w3[l], 'hy_freq': hy_freq[l], 'hy_decay': hy_decay[l], 'hy_bias': hy_bias[l],
             'hg_norm_g': hg_norm_g[l], 'da_qnorm_g': da_qnorm_g[l], 'da_knorm_g': da_knorm_g[l],
             'da_lam': da_lam[l], 'da_norm_g': da_norm_g[l]}
        mod_x = jnp.split((sc @ ada_w[l] + ada_b[l])[:, None, :], ADA_CHUNKS, axis=-1)
        mod_c1 = jnp.split((scc @ ada_w[l] + ada_b[l])[None, None, :], ADA_CHUNKS, axis=-1)
        mod_c = [jnp.broadcast_to(m, (B, 1, D)) for m in mod_c1]
        wg = _group_weights(w_in[l])
        wa, ba = _gate_weights(gla_wa2[l], gla_ba[l])
        lb = lower[l].reshape(1, 2 * HG_HEADS * HG_DK)
        wb = w_branch[l].astype(BF16)
        wo = w_out[l].astype(BF16)
        w1, w3, w2 = moe_w1[l], moe_w3[l], moe_w2[l]
        gx = _in_proj(xx, norm1_g[l], mod_x[0], mod_x[1], wg, wa, ba, lb, tm=256)
        gc = _in_proj(xc, norm1_g[l], mod_c[0], mod_c[1], wg, wa, ba, lb, tm=256)
        outs_c, outs_x = _token_mixer(gc, gx, p, lam_init, last)
        xx, hx = _merge_out(outs_x, gx[4], wb, wo, xx, mod_x[2], norm2_g[l], mod_x[3], mod_x[4], tm=256)
        xx = _expert_choice_moe(xx, mod_x[5], hx, moe_router[l], w1, w3, w2)
        if not last:
            xc, hc = _merge_out(outs_c, gc[4], wb, wo, xc, mod_c[2], norm2_g[l], mod_c[3], mod_c[4], tm=256)
            xc = _expert_choice_moe(xc, mod_c[5], hc, moe_router[l], w1, w3, w2)
    return xx
```

```python
import functools
import math

import jax
import jax.numpy as jnp
import numpy as np
from jax import lax
from jax.experimental import pallas as pl
from jax.experimental.pallas import tpu as pltpu

D_MODEL = 1024
DEPTH = 2
GRID_W = 64
N_BRANCH = 4
MIX_W = 256
GLA_HEADS = 4
GLA_DK = 32
GLA_DV = 64
GLA_RANK = 16
GLA_TAU = 16.0
HY_WIDTH = 256
HY_ORDER = 2
HY_BANDS = 16
HY_SHIFT = 0.05
HG_HEADS = 4
HG_DK = 64
HG_DV = 64
DA_HEADS = 4
DA_DK = 32
DA_DV = 64
ROPE_BASE = 10000.0
N_EXPERTS = 16
EXPERT_FF = 1024
CAPACITY_FACTOR = 2
ADA_CHUNKS = 6
EPS = 1e-6
F_TINY = 1e-20

IN_NAMES = ('gla_q', 'gla_k', 'gla_v', 'gla_af', 'gla_ab', 'gla_g', 'hy',
            'hg_q', 'hg_ff', 'hg_fb', 'hg_i', 'hg_g', 'da_q', 'da_k', 'da_v', 'merge')
IN_WIDTHS = (GLA_HEADS * GLA_DK, GLA_HEADS * GLA_DK, GLA_HEADS * GLA_DV, GLA_RANK, GLA_RANK, GLA_HEADS * GLA_DV,
             (1 + HY_ORDER) * HY_WIDTH,
             HG_HEADS * HG_DK, HG_HEADS * HG_DK, HG_HEADS * HG_DK, HG_HEADS * HG_DV, HG_HEADS * HG_DV,
             DA_HEADS * 2 * DA_DK, DA_HEADS * 2 * DA_DK, DA_HEADS * DA_DV,
             N_BRANCH * D_MODEL)
IN_OFFSETS = tuple(int(v) for v in np.cumsum((0,) + IN_WIDTHS)[:-1])

PROJ_GROUPS = (
    (('gla_q', 'gla_k', 'gla_v', 'gla_g', 'gla_af', 'gla_ab'), 896),
    (('hy',), 768),
    (('hg_q', 'hg_ff', 'hg_fb', 'hg_i', 'hg_g'), 1280),
    (('da_q', 'da_k', 'da_v'), 768),
    (('merge',), 4096),
)
PROJ_WIDTHS = tuple(w for _, w in PROJ_GROUPS)
LANES = 128
OUT_WIDTHS = (4 * GLA_HEADS * GLA_DK + 2 * GLA_HEADS * GLA_DV, PROJ_WIDTHS[1],
              5 * HG_HEADS * HG_DK + 2 * HG_HEADS * HG_DV, PROJ_WIDTHS[3], PROJ_WIDTHS[4])
OUT_DTYPES = (jnp.float32,) * 4 + (jnp.bfloat16,)
VMEM_LIMIT = 56 * 1024 * 1024

BF16 = jnp.bfloat16
F32 = jnp.float32
N_HEADS = 4
RC = 64
N_QK = 2 * DA_HEADS
VROWS = DA_DV + 8
LOG2E = 1.4426950408889634
FFN_MIN_ROWS = 256
HB = 256


def _split_bf16(a):
    hi = a.astype(BF16)
    return hi, (a - hi.astype(F32)).astype(BF16)


def _in_proj_kernel(x_ref, g_ref, sh_ref, sc_ref, w_ref, wa_ref, ba_ref, lb_ref,
                    gla_ref, hy_ref, hg_ref, da_ref, mg_ref):
    x = x_ref[0]
    ms = jnp.mean(x * x, axis=-1, keepdims=True)
    h = x * lax.rsqrt(ms + EPS) * g_ref[...]
    h = (h * (1.0 + sc_ref[0]) + sh_ref[0]).astype(BF16)
    offs = np.cumsum((0,) + PROJ_WIDTHS)

    def proj(k):
        return jnp.dot(h, w_ref[:, int(offs[k]):int(offs[k + 1])], preferred_element_type=F32)

    r = proj(0)
    qk = GLA_HEADS * GLA_DK
    vw = GLA_HEADS * GLA_DV
    gla_ref[0, :, 0:qk] = r[:, 0:qk] * (GLA_DK ** -0.5)
    gla_ref[0, :, qk:2 * qk] = r[:, qk:2 * qk]
    a_hi, a_lo = _split_bf16(r[:, 2 * qk + 2 * vw:])
    z = (jnp.dot(a_hi, wa_ref[0], preferred_element_type=F32) + jnp.dot(a_lo, wa_ref[0], preferred_element_type=F32)
         + jnp.dot(a_hi, wa_ref[1], preferred_element_type=F32) + ba_ref[...])
    gla_ref[0, :, 2 * qk:4 * qk] = jax.nn.log_sigmoid(z) * (1.0 / GLA_TAU)
    gla_ref[0, :, 4 * qk:4 * qk + 2 * vw] = r[:, 2 * qk:2 * qk + 2 * vw]

    hy_ref[0] = proj(1)

    r = proj(2)
    hw = HG_HEADS * HG_DK
    q = r[:, 0:hw]
    hg_ref[0, :, 0:hw] = q * jax.nn.sigmoid(q)
    zf = r[:, hw:3 * hw]
    lb = lb_ref[...]
    hg_ref[0, :, hw:3 * hw] = (1.0 - lb) * jax.nn.sigmoid(-zf)
    hg_ref[0, :, 3 * hw:5 * hw] = jnp.log(jnp.maximum(lb + (1.0 - lb) * jax.nn.sigmoid(zf), F_TINY))
    hg_ref[0, :, 5 * hw:7 * hw] = r[:, 3 * hw:5 * hw]

    da_ref[0] = proj(3)
    mg_ref[0] = proj(4).astype(mg_ref.dtype)


def _in_proj(x, g, shift, scale, w_groups, wa, ba, lb, tm):
    B, L, D = x.shape
    nw = w_groups.shape[1]
    const2 = lambda b, i: (0, 0)
    return pl.pallas_call(
        _in_proj_kernel,
        grid=(B, L // tm),
        in_specs=[
            pl.BlockSpec((1, tm, D), lambda b, i: (b, i, 0)),
            pl.BlockSpec((1, D), const2),
            pl.BlockSpec((1, 1, D), lambda b, i: (b, 0, 0)),
            pl.BlockSpec((1, 1, D), lambda b, i: (b, 0, 0)),
            pl.BlockSpec((D, nw), const2, pipeline_mode=pl.Buffered(1)),
            pl.BlockSpec(wa.shape, lambda b, i: (0, 0, 0)),
            pl.BlockSpec(ba.shape, const2),
            pl.BlockSpec(lb.shape, const2),
        ],
        out_specs=[pl.BlockSpec((1, tm, w), lambda b, i: (b, i, 0)) for w in OUT_WIDTHS],
        out_shape=[jax.ShapeDtypeStruct((B, L, w), dt) for w, dt in zip(OUT_WIDTHS, OUT_DTYPES)],
        compiler_params=pltpu.CompilerParams(
            dimension_semantics=("parallel", "parallel"), vmem_limit_bytes=VMEM_LIMIT),
        name="in_proj",
    )(x, g.reshape(1, D), shift, scale, w_groups, wa, ba, lb)


def _merge_kernel(o0_ref, o1_ref, o2_ref, o3_ref, gate_ref, wb_ref, wo_ref, x_ref, m_ref, g_ref, sh_ref, sc_ref,
                  out_ref, h_ref):
    D = D_MODEL
    acc = None
    for i, o_ref in enumerate((o0_ref, o1_ref, o2_ref, o3_ref)):
        t = jnp.dot(o_ref[0].astype(BF16), wb_ref[i], preferred_element_type=F32)
        t = jax.nn.sigmoid(gate_ref[0, :, i * D:(i + 1) * D].astype(F32)) * t
        acc = t if acc is None else acc + t
    mx = jnp.dot(acc.astype(BF16), wo_ref[...], preferred_element_type=F32)
    out = x_ref[0] + m_ref[0] * mx
    out_ref[0] = out
    ms = jnp.mean(out * out, axis=-1, keepdims=True)
    h_ref[0] = out * lax.rsqrt(ms + EPS) * g_ref[...] * (1.0 + sc_ref[0]) + sh_ref[0]


def _merge_out(outs, gate_cols, w_branch, w_out, x, m, g, shift, scale, tm):
    B, L, D = x.shape
    tok = pl.BlockSpec((1, tm, D), lambda b, i: (b, i, 0))
    per_sample = pl.BlockSpec((1, 1, D), lambda b, i: (b, 0, 0))
    return pl.pallas_call(
        _merge_kernel,
        grid=(B, L // tm),
        in_specs=[pl.BlockSpec((1, tm, MIX_W), lambda b, i: (b, i, 0)) for _ in range(N_BRANCH)] + [
            pl.BlockSpec((1, tm, N_BRANCH * D), lambda b, i: (b, i, 0)),
            pl.BlockSpec((N_BRANCH, MIX_W, D), lambda b, i: (0, 0, 0)),
            pl.BlockSpec((D, D), lambda b, i: (0, 0)),
            tok, per_sample,
            pl.BlockSpec((1, D), lambda b, i: (0, 0)), per_sample, per_sample,
        ],
        out_specs=[tok, tok],
        out_shape=[jax.ShapeDtypeStruct((B, L, D), F32), jax.ShapeDtypeStruct((B, L, D), F32)],
        compiler_params=pltpu.CompilerParams(
            dimension_semantics=("parallel", "parallel"), vmem_limit_bytes=VMEM_LIMIT),
        name="merge_out",
    )(*outs, gate_cols, w_branch, w_out, x, m, g.reshape(1, D), shift, scale)


def _expert_ffn_kernel(x_ref, g_ref, w1_ref, w3_ref, w2_ref, out_ref, w1_scr, w3_scr, w2_scr):
    @pl.when(pl.program_id(1) == 0)
    def _():
        w1_scr[...] = w1_ref[0].astype(BF16)
        w3_scr[...] = w3_ref[0].astype(BF16)
        w2_scr[...] = w2_ref[0].astype(BF16)

    x = x_ref[0, 0].astype(BF16)
    a = jnp.dot(x, w1_scr[...], preferred_element_type=F32)
    b = jnp.dot(x, w3_scr[...], preferred_element_type=F32)
    h = (a * jax.nn.sigmoid(a) * b).astype(BF16)
    y = jnp.dot(h, w2_scr[...], preferred_element_type=F32)
    out_ref[0, 0] = (y * g_ref[0, 0]).astype(out_ref.dtype)


def _combine_kernel(idx_ref, y_ref, x_ref, m_ref, out_ref, *, ks):
    tt = out_ref.shape[1]
    S = y_ref.shape[1]
    base = pl.program_id(1) * tt
    acc = None
    for s0 in range(0, S, ks):
        tok = lax.broadcasted_iota(jnp.int32, (tt, ks), 0) + base
        onehot = jnp.where(tok == idx_ref[0, :, s0:s0 + ks], 1.0, 0.0).astype(BF16)
        part = jnp.dot(onehot, y_ref[0, s0:s0 + ks, :], preferred_element_type=F32)
        acc = part if acc is None else acc + part
    out_ref[0] = x_ref[0] + m_ref[0] * acc


def _combine(idx, y, x, m, tt):
    B, L, D = x.shape
    S = y.shape[1]
    return pl.pallas_call(
        functools.partial(_combine_kernel, ks=min(S, 2048)),
        grid=(B, L // tt),
        in_specs=[
            pl.BlockSpec((1, 1, S), lambda b, i: (b, 0, 0)),
            pl.BlockSpec((1, S, D), lambda b, i: (b, 0, 0)),
            pl.BlockSpec((1, tt, D), lambda b, i: (b, i, 0)),
            pl.BlockSpec((1, 1, D), lambda b, i: (b, 0, 0)),
        ],
        out_specs=pl.BlockSpec((1, tt, D), lambda b, i: (b, i, 0)),
        out_shape=jax.ShapeDtypeStruct((B, L, D), F32),
        compiler_params=pltpu.CompilerParams(
            dimension_semantics=("parallel", "parallel"), vmem_limit_bytes=VMEM_LIMIT),
        name="moe_combine",
    )(idx, y, x, m)


def _expert_ffn(xg, g, w1, w3, w2):
    B, E, cap, D = xg.shape
    F = w1.shape[-1]
    return pl.pallas_call(
        _expert_ffn_kernel,
        grid=(E, B),
        in_specs=[
            pl.BlockSpec((1, 1, cap, D), lambda e, b: (b, e, 0, 0)),
            pl.BlockSpec((1, 1, cap, 1), lambda e, b: (b, e, 0, 0)),
            pl.BlockSpec((1, D, F), lambda e, b: (e, 0, 0)),
            pl.BlockSpec((1, D, F), lambda e, b: (e, 0, 0)),
            pl.BlockSpec((1, F, D), lambda e, b: (e, 0, 0)),
        ],
        out_specs=pl.BlockSpec((1, 1, cap, D), lambda e, b: (b, e, 0, 0)),
        out_shape=jax.ShapeDtypeStruct((B, E, cap, D), BF16),
        scratch_shapes=[pltpu.VMEM((D, F), BF16), pltpu.VMEM((D, F), BF16), pltpu.VMEM((F, D), BF16)],
        compiler_params=pltpu.CompilerParams(
            dimension_semantics=("parallel", "arbitrary"), vmem_limit_bytes=VMEM_LIMIT),
        name="expert_ffn",
    )(xg, g, w1, w3, w2)


def _recur_tables(C, rev):
    pos = (C - 1 - np.arange(C)) if rev else np.arange(C)
    row_of = np.argsort(pos)
    levels = int(np.log2(C))
    diff = np.zeros((levels + 1, C, C), np.float32)
    mask = np.zeros((levels + 1, C, C), np.float32)
    for t in range(C):
        p = pos[t]
        diff[0, t, row_of[:p + 1]] = 1.0
        mask[0, t, t] = 1.0
        for l in range(1, levels + 1):
            n, m = 2 ** l, 2 ** (l - 1)
            off = p % n
            mid = p - off + m - 1
            if off >= m:
                diff[l, t, row_of[mid + 1:p + 1]] = 1.0
                mask[l, t, row_of[p - off:mid + 1]] = 1.0
            else:
                diff[l, t, row_of[p + 1:mid + 1]] = 1.0
    return (jnp.asarray(np.tile(diff.reshape((levels + 1) * C, C), (1, 3)), BF16),
            jnp.asarray(np.tile(mask, (1, 1, N_HEADS))))


def _recur_kernel(*refs, rev, C, nsub, dk, dv, finish):
    if finish:
        (q_ref, k_ref, v_ref, g_ref, s0_ref, d_ref, m_ref, prev_ref, gate_ref, gn_ref,
         o_ref, sfin_ref, s_scr, e_scr, a_scr) = refs
    else:
        q_ref, k_ref, v_ref, g_ref, s0_ref, d_ref, m_ref, o_ref, sfin_ref, s_scr, e_scr, a_scr = refs
    HK = N_HEADS * dk
    HV = N_HEADS * dv
    i = pl.program_id(1)
    if finish:
        head_mean = jnp.where(lax.broadcasted_iota(jnp.int32, (HV, HV), 0) // dv
                              == lax.broadcasted_iota(jnp.int32, (HV, HV), 1) // dv, 1.0 / dv, 0.0).astype(BF16)

    @pl.when(i == 0)
    def _():
        s_scr[...] = s0_ref[0]

    head_k = [(lax.broadcasted_iota(jnp.int32, (1, HK), 1) // dk == h).astype(BF16) for h in range(N_HEADS)]
    head_v = [(lax.broadcasted_iota(jnp.int32, (1, HV), 1) // dv == h).astype(BF16) for h in range(N_HEADS)]
    bd = (lax.broadcasted_iota(jnp.int32, (HV, HK), 0) // dv
          == lax.broadcasted_iota(jnp.int32, (HV, HK), 1) // dk)
    levels = m_ref.shape[0] - 1

    def stack_heads(x, head):
        xb = x.astype(BF16)
        return jnp.concatenate([xb * head[h] for h in range(N_HEADS)], axis=0)

    def nt(a, b):
        return lax.dot_general(a, b, (((1,), (1,)), ((), ())), preferred_element_type=F32)

    order = [(nsub - 1 - j) if rev else j for j in range(nsub)]
    for c in order:
        g = g_ref[0, c * C:(c + 1) * C, :]
        g_hi = g.astype(BF16)
        r1 = g - g_hi.astype(F32)
        g_mid = r1.astype(BF16)
        g_lo = (r1 - g_mid.astype(F32)).astype(BF16)
        e_scr[c] = jnp.dot(d_ref[...], jnp.concatenate([g_hi, g_mid, g_lo], axis=0), preferred_element_type=F32)
    for c in order:
        q = q_ref[0, c * C:(c + 1) * C, :]
        k = k_ref[0, c * C:(c + 1) * C, :]
        a = nt(q.astype(BF16), stack_heads(k, head_k)) * m_ref[0]
        for lvl in range(1, levels + 1):
            x = jnp.exp(e_scr[c, lvl * C:(lvl + 1) * C, :])
            a = a + nt((q * x).astype(BF16), stack_heads(k * x, head_k)) * m_ref[lvl]
        a_scr[c] = a.astype(BF16)
    for c in order:
        sl = slice(c * C, (c + 1) * C)
        q = q_ref[0, sl, :]
        k = k_ref[0, sl, :]
        v = v_ref[0, sl, :]
        b = e_scr[c, 0:C, :]
        st = s_scr[...]
        o = jnp.dot(a_scr[c], stack_heads(v, head_v), preferred_element_type=F32)
        o = o + nt((q * jnp.exp(b)).astype(BF16), st.astype(BF16))
        if finish:
            o = o + prev_ref[0, sl, :]
            sq_hi, sq_lo = _split_bf16(o * o)
            ms = (jnp.dot(sq_hi, head_mean, preferred_element_type=F32)
                  + jnp.dot(sq_lo, head_mean, preferred_element_type=F32))
            gate = gate_ref[0, sl, :]
            o = o * lax.rsqrt(ms + EPS) * gn_ref[...] * (gate * jax.nn.sigmoid(gate))
        o_ref[0, sl, :] = o
        b_end = b[0:1, :] if rev else b[C - 1:C, :]
        kend = (k * jnp.exp(b_end - b)).astype(BF16)
        upd = lax.dot_general(v.astype(BF16), kend, (((0,), (0,)), ((), ())), preferred_element_type=F32)
        s_scr[...] = st * jnp.exp(b_end) + jnp.where(bd, upd, 0.0)

    @pl.when(i == pl.num_programs(1) - 1)
    def _():
        sfin_ref[0] = s_scr[...]


def _recurrence(pack, cols, s0, prev, gnorm, *, rev, dk, dv, tb):
    B, L, _ = pack.shape
    HK, HV = N_HEADS * dk, N_HEADS * dv
    C = RC
    nblk = L // tb
    diff, masks = _recur_tables(C, rev)
    cq, ck, cg, cv, cgate = cols

    def tok(col):
        return (lambda b, i: (b, nblk - 1 - i, col)) if rev else (lambda b, i: (b, i, col))

    in_specs = [
        pl.BlockSpec((1, tb, HK), tok(cq)), pl.BlockSpec((1, tb, HK), tok(ck)),
        pl.BlockSpec((1, tb, HV), tok(cv)), pl.BlockSpec((1, tb, HK), tok(cg)),
        pl.BlockSpec((1, HV, HK), lambda b, i: (b, 0, 0)),
        pl.BlockSpec(diff.shape, lambda b, i: (0, 0)),
        pl.BlockSpec(masks.shape, lambda b, i: (0, 0, 0)),
    ]
    args = [pack, pack, pack, pack, s0, diff, masks]
    if prev is not None:
        in_specs += [pl.BlockSpec((1, tb, HV), tok(0)), pl.BlockSpec((1, tb, HV), tok(cgate)),
                     pl.BlockSpec((1, HV), lambda b, i: (0, 0))]
        args += [prev, pack, gnorm]
    kern = functools.partial(_recur_kernel, rev=rev, C=C, nsub=tb // C, dk=dk, dv=dv, finish=prev is not None)
    return pl.pallas_call(
        kern,
        grid=(B, nblk),
        in_specs=in_specs,
        out_specs=[pl.BlockSpec((1, tb, HV), tok(0)), pl.BlockSpec((1, HV, HK), lambda b, i: (b, 0, 0))],
        out_shape=[jax.ShapeDtypeStruct((B, L, HV), F32), jax.ShapeDtypeStruct((B, HV, HK), F32)],
        scratch_shapes=[pltpu.VMEM((HV, HK), F32), pltpu.VMEM((tb // C, diff.shape[0], HK), F32),
                        pltpu.VMEM((tb // C, C, N_HEADS * C), BF16)],
        compiler_params=pltpu.CompilerParams(
            dimension_semantics=("parallel", "arbitrary"), vmem_limit_bytes=VMEM_LIMIT),
        name="recur_rev" if rev else "recur_fwd",
    )(*args)


def _bidir(pack_c, pack_l, cols_f, cols_b, gnorm, *, dk, dv):
    B, Lc, _ = pack_c.shape
    s0 = jnp.zeros((B, N_HEADS * dv, N_HEADS * dk), F32)
    gn = jnp.tile(gnorm.astype(F32), N_HEADS).reshape(1, N_HEADS * dv)
    kw = dict(dk=dk, dv=dv)
    oc_f, sc_f = _recurrence(pack_c, cols_f, s0, None, None, rev=False, tb=Lc, **kw)
    ol_f, _ = _recurrence(pack_l, cols_f, sc_f, None, None, rev=False, tb=256, **kw)
    oc, sc_b = _recurrence(pack_c, cols_b, s0, oc_f, gn, rev=True, tb=Lc, **kw)
    ol, _ = _recurrence(pack_l, cols_b, sc_b, ol_f, gn, rev=True, tb=256, **kw)
    return oc, ol


def _attn_kernel(lam_ref, qt_ref, k_ref, vt_ref, g_ref, o_ref, qm_scr, s_scr, p_scr, m_scr, a_scr, acc_scr, *,
                 first, tk):
    W = qt_ref.shape[1]
    tq = qt_ref.shape[2]
    Lk = k_ref.shape[1]
    qt = qt_ref[0]
    row_pair = lax.broadcasted_iota(jnp.int32, (W, tq), 0) // DA_DK
    for j in range(N_QK):
        qm_scr[j] = jnp.where(row_pair == j, qt, 0.0).astype(BF16)
    m_scr[...] = jnp.full(m_scr.shape, -jnp.inf, F32)
    acc_scr[...] = jnp.zeros(acc_scr.shape, F32)

    def block(start, size):
        ks = pl.ds(start, size)
        kblk = k_ref[0, ks, :]
        for j in range(N_QK):
            s_scr[j, :size] = jnp.dot(kblk, qm_scr[j], preferred_element_type=F32)
        for j in range(N_QK):
            s = s_scr[j, :size]
            m_old = m_scr[j]
            m_new = jnp.maximum(m_old, jnp.max(s, axis=0, keepdims=True))
            a_scr[j] = jnp.exp2(m_old - m_new)
            p_scr[j, :size] = jnp.exp2(s - m_new).astype(BF16)
            m_scr[j] = m_new
        for j in range(N_QK):
            pv = jnp.dot(vt_ref[0, j // 2, :, ks], p_scr[j, :size], preferred_element_type=F32)
            acc_scr[j] = a_scr[j] * acc_scr[j] + pv

    block(0, first)

    def body(i, carry):
        block(pl.multiple_of(first + i * tk, math.gcd(first, tk)), tk)
        return carry

    lax.fori_loop(0, (Lk - first) // tk, body, 0)
    lam = lam_ref[0]
    outs = []
    for h in range(DA_HEADS):
        a1, a2 = acc_scr[2 * h], acc_scr[2 * h + 1]
        o = a1[:DA_DV] / a1[DA_DV:DA_DV + 1] - lam * (a2[:DA_DV] / a2[DA_DV:DA_DV + 1])
        ms = jnp.mean(o * o, axis=0, keepdims=True)
        outs.append(o * lax.rsqrt(ms + EPS) * g_ref[...])
    o_ref[0] = jnp.concatenate(outs, axis=0).T


def _diff_attention(qt, k, vt, lam, gcol, *, tq, first, tk):
    B, W, Lq = qt.shape
    Lk = k.shape[1]
    assert (Lk - first) % tk == 0
    rows = max(first, tk)
    HV = DA_HEADS * DA_DV
    return pl.pallas_call(
        functools.partial(_attn_kernel, first=first, tk=tk),
        grid=(B, Lq // tq),
        in_specs=[
            pl.BlockSpec(memory_space=pltpu.SMEM),
            pl.BlockSpec((1, W, tq), lambda b, i: (b, 0, i)),
            pl.BlockSpec((1, Lk, W), lambda b, i: (b, 0, 0)),
            pl.BlockSpec((1, DA_HEADS, VROWS, Lk), lambda b, i: (b, 0, 0, 0)),
            pl.BlockSpec((DA_DV, 1), lambda b, i: (0, 0)),
        ],
        out_specs=pl.BlockSpec((1, tq, HV), lambda b, i: (b, i, 0)),
        out_shape=jax.ShapeDtypeStruct((B, Lq, HV), F32),
        scratch_shapes=[
            pltpu.VMEM((N_QK, W, tq), BF16),
            pltpu.VMEM((N_QK, rows, tq), F32),
            pltpu.VMEM((N_QK, rows, tq), BF16),
            pltpu.VMEM((N_QK, 1, tq), F32),
            pltpu.VMEM((N_QK, 1, tq), F32),
            pltpu.VMEM((N_QK, VROWS, tq), F32),
        ],
        compiler_params=pltpu.CompilerParams(
            dimension_semantics=("parallel", "parallel"), vmem_limit_bytes=VMEM_LIMIT),
        name="diff_attn",
    )(lam, qt, k, vt, gcol)


def _hyena_kernel(w_ref, bias_ref, uv_ref, u1_ref, u2_ref, hp_ref, o_ref, acc_scr, *, nb, nbatch):
    c = pl.program_id(0)
    half = nb // 2
    rows = nb * nbatch
    lane = lax.broadcasted_iota(jnp.int32, (rows, HB), 1)
    row = lax.broadcasted_iota(jnp.int32, (rows, HB), 0)

    def short_conv(u_ref, ch):
        u = u_ref[0].reshape(rows, HB)
        prev = pltpu.roll(u, 1, axis=1)
        prev = jnp.where(lane == 0, jnp.where(row >= nbatch, pltpu.roll(prev, nbatch % rows, axis=0), 0.0), prev)
        nxt = pltpu.roll(u, HB - 1, axis=1)
        nxt = jnp.where(lane == HB - 1,
                        jnp.where(row < rows - nbatch, pltpu.roll(nxt, (rows - nbatch) % rows, axis=0), 0.0), nxt)
        return w_ref[0, ch] * prev + w_ref[1, ch] * u + w_ref[2, ch] * nxt

    def long_conv(z, o):
        zb = z.astype(BF16)
        hrow = hp_ref[o, 0]
        acc_scr[...] = jnp.zeros(acc_scr.shape, F32)
        for off in range(-half, half + 1):
            lo = (off + half) * HB
            r = jnp.broadcast_to(hrow[:, lo:lo + 2 * HB], (HB, 2 * HB))
            w = pltpu.roll(r, 1, 1, stride=1, stride_axis=0)[:, HB:].astype(BF16)
            s0, s1 = max(0, -off), nb - max(0, off)
            if s1 <= s0:
                continue
            src = slice(s0 * nbatch, s1 * nbatch)
            dst = slice((s0 + off) * nbatch, (s1 + off) * nbatch)
            acc_scr[dst, :] += jnp.dot(zb[src], w, preferred_element_type=F32)
        return acc_scr[...]

    z = short_conv(uv_ref, c)
    for o, u_ref in enumerate((u1_ref, u2_ref)):
        gate = short_conv(u_ref, (o + 1) * HY_WIDTH + c)
        z = gate * (long_conv(z, o) + bias_ref[o, c] * z)
    o_ref[0] = z.reshape(nb, nbatch, HB)


def _hyena_fused(ut, conv_w, hp, bias):
    _, nb, B, _ = ut.shape
    C = HY_WIDTH
    blk = lambda k: pl.BlockSpec((1, nb, B, HB), lambda c: (c + k * C, 0, 0, 0))
    return pl.pallas_call(
        functools.partial(_hyena_kernel, nb=nb, nbatch=B),
        grid=(C,),
        in_specs=[
            pl.BlockSpec(memory_space=pltpu.SMEM),
            pl.BlockSpec(memory_space=pltpu.SMEM),
            blk(0), blk(1), blk(2),
            pl.BlockSpec((HY_ORDER, 1, 1, hp.shape[-1]), lambda c: (0, c, 0, 0)),
        ],
        out_specs=pl.BlockSpec((1, nb, B, HB), lambda c: (c, 0, 0, 0)),
        out_shape=jax.ShapeDtypeStruct((C, nb, B, HB), F32),
        scratch_shapes=[pltpu.VMEM((nb * B, HB), F32)],
        compiler_params=pltpu.CompilerParams(
            dimension_semantics=("parallel",), vmem_limit_bytes=VMEM_LIMIT),
        name="hyena_branch",
    )(conv_w, bias, ut, ut, ut, hp)


def _rms_norm(x, g):
    xf = x.astype(F32)
    y = xf * lax.rsqrt(jnp.mean(xf * xf, axis=-1, keepdims=True) + EPS)
    return (y * g.astype(F32)).astype(x.dtype)


def _gate_weights(wa2, ba):
    qk = GLA_HEADS * GLA_DK
    w = jnp.zeros((LANES, 2 * qk), F32)
    w = w.at[:GLA_RANK, :qk].set(wa2[0].astype(F32)).at[GLA_RANK:2 * GLA_RANK, qk:].set(wa2[1].astype(F32))
    hi = w.astype(BF16)
    lo = (w - hi.astype(F32)).astype(BF16)
    return jnp.stack([hi, lo]), ba.astype(F32).reshape(1, 2 * qk)


def _hyena_filters(L, p):
    j = jnp.arange(L, dtype=F32)
    t = j / max(L - 1, 1)
    w = 2 * math.pi * j / L
    f = jnp.linspace(1e-4, HY_BANDS - 1, HY_BANDS, dtype=F32)
    feats = jnp.concatenate([t[:, None], jnp.cos(w[:, None] * f), -jnp.sin(w[:, None] * f)], axis=-1)
    h = jnp.sin(p['hy_freq'][0] * (feats @ p['hy_w1'] + p['hy_b1']))
    h = jnp.sin(p['hy_freq'][1] * (h @ p['hy_w2'] + p['hy_b2']))
    h = (h @ p['hy_w3']).astype(F32)
    dist = jnp.abs(j - L // 2) / (L // 2)
    h = h * (jnp.exp(-dist[:, None] * jnp.abs(p['hy_decay'].astype(F32))) + HY_SHIFT)
    h = h / jnp.sum(jnp.abs(h), axis=0, keepdims=True)
    return h.reshape(L, HY_ORDER, HY_WIDTH)


def _to_blocks(a):
    B, L, C = a.shape
    return a.reshape(B, L // HB, HB, C).transpose(3, 1, 0, 2)


def _from_blocks(a):
    C, nb, B, _ = a.shape
    return a.transpose(2, 1, 3, 0).reshape(B, nb * HB, C)


def _pad_filter(h):
    L = h.shape[0]
    nb = L // HB
    left = (HB - 1) - (L // 2 - (nb // 2) * HB)
    return jnp.pad(h.T, ((0, 0), (left, (nb + 2) * HB - L - left)))[:, None, :]


def _hyena_branch(u3, p):
    h = _hyena_filters(u3.shape[1], p)
    hp = jnp.stack([_pad_filter(h[:, o]) for o in range(HY_ORDER)])
    return _from_blocks(_hyena_fused(_to_blocks(u3), p['hy_conv_w'], hp, p['hy_bias']))


def _rope_tables(L):
    quarter = DA_DK // 4
    freqs = ROPE_BASE ** (-jnp.arange(quarter, dtype=F32) / quarter)
    row = (jnp.arange(L) // GRID_W).astype(F32)[:, None] * freqs
    col = (jnp.arange(L) % GRID_W).astype(F32)[:, None] * freqs
    return jnp.cos(row), jnp.sin(row), jnp.cos(col), jnp.sin(col)


def _da_prep(t, g, tables):
    B, L, W = t.shape
    t = _rms_norm(t.reshape(B, L, N_QK, DA_DK), g)
    if tables is not None:
        cr, sr, cc, sc = [a[None, :, None, :] for a in tables]
        e = DA_DK // 4
        a1, a2, b1, b2 = t[..., :e], t[..., e:2 * e], t[..., 2 * e:3 * e], t[..., 3 * e:]
        t = jnp.concatenate([a1 * cr - a2 * sr, a1 * sr + a2 * cr, b1 * cc - b2 * sc, b1 * sc + b2 * cc], axis=-1)
    return t.reshape(B, L, W)


def _value_rows(v):
    B, Lk, _ = v.shape
    vt = v.reshape(B, Lk, DA_HEADS, DA_DV).transpose(0, 2, 3, 1)
    extra = jnp.concatenate([jnp.ones((B, DA_HEADS, 1, Lk), v.dtype),
                             jnp.zeros((B, DA_HEADS, VROWS - DA_DV - 1, Lk), v.dtype)], axis=2)
    return jnp.concatenate([vt, extra], axis=2).astype(BF16)


def _da_parts(da):
    w = DA_HEADS * 2 * DA_DK
    return da[..., :w], da[..., w:2 * w], da[..., 2 * w:]


def _group_weights(w_in):
    cols = []
    for names, width in PROJ_GROUPS:
        used = 0
        for nm in names:
            i = IN_NAMES.index(nm)
            cols.append(w_in[:, IN_OFFSETS[i]:IN_OFFSETS[i] + IN_WIDTHS[i]])
            used += IN_WIDTHS[i]
        if width > used:
            cols.append(jnp.zeros((w_in.shape[0], width - used), w_in.dtype))
    return jnp.concatenate(cols, axis=1).astype(BF16)


GLA_COLS_F, GLA_COLS_B = (0, 1, 2, 2, 3), (0, 1, 3, 2, 3)
HG_COLS_F, HG_COLS_B = (0, 1, 3, 5, 6), (0, 2, 4, 5, 6)


def _token_mixer(gc, gx, p, lam_init, last):
    gla_c, gla_x = _bidir(gc[0], gx[0], GLA_COLS_F, GLA_COLS_B, p['gla_norm_g'], dk=GLA_DK, dv=GLA_DV)
    hg_c, hg_x = _bidir(gc[2], gx[2], HG_COLS_F, HG_COLS_B, p['hg_norm_g'], dk=HG_DK, dv=HG_DV)
    (dqc, dkc, dvc), (dqx, dkx, dvx) = _da_parts(gc[3]), _da_parts(gx[3])
    tables = _rope_tables(dqx.shape[1])
    scale = DA_DK ** -0.5 * LOG2E
    qx = (_da_prep(dqx, p['da_qnorm_g'], tables) * scale).transpose(0, 2, 1)
    kx = _da_prep(dkx, p['da_knorm_g'], tables)
    kc = _da_prep(dkc, p['da_knorm_g'], None)
    k_all = jnp.concatenate([kc, kx], axis=1).astype(BF16)
    vt_all = _value_rows(jnp.concatenate([dvc, dvx], axis=1))
    lp = p['da_lam'].astype(F32)
    lam = (jnp.exp(jnp.sum(lp[0] * lp[1])) - jnp.exp(jnp.sum(lp[2] * lp[3])) + lam_init).reshape(1)
    gcol = (p['da_norm_g'].astype(F32) * (1 - lam_init)).reshape(DA_DV, 1)
    Lc = kc.shape[1]
    da_x = _diff_attention(qx, k_all, vt_all, lam, gcol, tq=256, first=Lc, tk=512)
    outs_x = (gla_x, _hyena_branch(gx[1], p), hg_x, da_x)
    if last:
        return None, outs_x
    qc = (_da_prep(dqc, p['da_qnorm_g'], None) * scale).transpose(0, 2, 1)
    da_c = _diff_attention(qc, k_all[:, :Lc], vt_all[..., :Lc], lam, gcol, tq=Lc, first=Lc, tk=Lc)
    return (gla_c, _hyena_branch(gc[1], p), hg_c, da_c), outs_x


def _expert_choice_moe(x, m, h, router, w1, w3, w2):
    B, L, D = h.shape
    cap = CAPACITY_FACTOR * L // N_EXPERTS
    aff = jax.nn.softmax((h @ router).astype(F32), axis=-1)
    g, idx = lax.top_k(aff.transpose(0, 2, 1), cap)
    xg = jax.vmap(lambda hb, ib: hb[ib])(h, idx)
    if cap < FFN_MIN_ROWS:
        regroup = lambda t: t.transpose(1, 0, 2, 3).reshape(1, N_EXPERTS, B * cap, t.shape[-1])
        y = _expert_ffn(regroup(xg), regroup(g[..., None]), w1, w3, w2)
        y = y.reshape(N_EXPERTS, B, cap, D).transpose(1, 0, 2, 3)
    else:
        y = _expert_ffn(xg, g[..., None], w1, w3, w2)
    return _combine(idx.reshape(B, 1, N_EXPERTS * cap), y.reshape(B, N_EXPERTS * cap, D), x, m, tt=min(L, 512))


def kernel(x, c, ctx, c_ctx, ada_w, ada_b, norm1_g, norm2_g, w_in, gla_wa2, gla_ba, gla_norm_g,
           hy_conv_w, hy_w1, hy_b1, hy_w2, hy_b2, hy_w3, hy_freq, hy_decay, hy_bias,
           hg_lower, hg_norm_g, da_qnorm_g, da_knorm_g, da_lam, da_norm_g, w_branch, w_out,
           moe_router, moe_w1, moe_w3, moe_w2):
    B, L, D = x.shape
    P = jax.nn.softmax(hg_lower.astype(F32), axis=0)
    lower = jnp.cumsum(P, axis=0) - P[0]
    sc = jax.nn.silu(c)
    scc = jax.nn.silu(c_ctx)
    xc, xx = ctx, x
    for l in range(DEPTH):
        last = l == DEPTH - 1
        lam_init = 0.8 - 0.6 * math.exp(-0.3 * l)
        p = {'gla_wa2': gla_wa2[l], 'gla_ba': gla_ba[l], 'gla_norm_g': gla_norm_g[l],
             'hy_conv_w': hy_conv_w[l], 'hy_w1': hy_w1[l], 'hy_b1': hy_b1[l], 'hy_w2': hy_w2[l], 'hy_b2': hy_b2[l],
             'hy_w3': hy_w3[l], 'hy_freq': hy_freq[l], 'hy_decay': hy_decay[l], 'hy_bias': hy_bias[l],
             'hg_norm_g': hg_norm_g[l], 'da_qnorm_g': da_qnorm_g[l], 'da_knorm_g': da_knorm_g[l],
             'da_lam': da_lam[l], 'da_norm_g': da_norm_g[l]}
        mod_x = jnp.split((sc @ ada_w[l] + ada_b[l])[:, None, :], ADA_CHUNKS, axis=-1)
        mod_c1 = jnp.split((scc @ ada_w[l] + ada_b[l])[None, None, :], ADA_CHUNKS, axis=-1)
        mod_c = [jnp.broadcast_to(m, (B, 1, D)) for m in mod_c1]
        wg = _group_weights(w_in[l])
        wa, ba = _gate_weights(gla_wa2[l], gla_ba[l])
        lb = lower[l].reshape(1, 2 * HG_HEADS * HG_DK)
        wb = w_branch[l].astype(BF16)
        wo = w_out[l].astype(BF16)
        w1, w3, w2 = moe_w1[l], moe_w3[l], moe_w2[l]
        gx = _in_proj(xx, norm1_g[l], mod_x[0], mod_x[1], wg, wa, ba, lb, tm=256)
        gc = _in_proj(xc, norm1_g[l], mod_c[0], mod_c[1], wg, wa, ba, lb, tm=256)
        outs_c, outs_x = _token_mixer(gc, gx, p, lam_init, last)
        xx, hx = _merge_out(outs_x, gx[4], wb, wo, xx, mod_x[2], norm2_g[l], mod_x[3], mod_x[4], tm=256)
        xx = _expert_choice_moe(xx, mod_x[5], hx, moe_router[l], w1, w3, w2)
        if not last:
            xc, hc = _merge_out(outs_c, gc[4], wb, wo, xc, mod_c[2], norm2_g[l], mod_c[3], mod_c[4], tm=256)
            xc = _expert_choice_moe(xc, mod_c[5], hc, moe_router[l], w1, w3, w2)
    return xx
```

```python
import functools
import math

import jax
import jax.numpy as jnp
import numpy as np
from jax import lax
from jax.experimental import pallas as pl
from jax.experimental.pallas import tpu as pltpu

D_MODEL = 1024
DEPTH = 2
GRID_W = 64
N_BRANCH = 4
MIX_W = 256
GLA_HEADS = 4
GLA_DK = 32
GLA_DV = 64
GLA_RANK = 16
GLA_TAU = 16.0
HY_WIDTH = 256
HY_ORDER = 2
HY_BANDS = 16
HY_SHIFT = 0.05
HG_HEADS = 4
HG_DK = 64
HG_DV = 64
DA_HEADS = 4
DA_DK = 32
DA_DV = 64
ROPE_BASE = 10000.0
N_EXPERTS = 16
EXPERT_FF = 1024
CAPACITY_FACTOR = 2
ADA_CHUNKS = 6
EPS = 1e-6
F_TINY = 1e-20

IN_NAMES = ('gla_q', 'gla_k', 'gla_v', 'gla_af', 'gla_ab', 'gla_g', 'hy',
            'hg_q', 'hg_ff', 'hg_fb', 'hg_i', 'hg_g', 'da_q', 'da_k', 'da_v', 'merge')
IN_WIDTHS = (GLA_HEADS * GLA_DK, GLA_HEADS * GLA_DK, GLA_HEADS * GLA_DV, GLA_RANK, GLA_RANK, GLA_HEADS * GLA_DV,
             (1 + HY_ORDER) * HY_WIDTH,
             HG_HEADS * HG_DK, HG_HEADS * HG_DK, HG_HEADS * HG_DK, HG_HEADS * HG_DV, HG_HEADS * HG_DV,
             DA_HEADS * 2 * DA_DK, DA_HEADS * 2 * DA_DK, DA_HEADS * DA_DV,
             N_BRANCH * D_MODEL)
IN_OFFSETS = tuple(int(v) for v in np.cumsum((0,) + IN_WIDTHS)[:-1])

PROJ_GROUPS = (
    (('gla_q', 'gla_k', 'gla_v', 'gla_g', 'gla_af', 'gla_ab'), 896),
    (('hy',), 768),
    (('hg_q', 'hg_ff', 'hg_fb', 'hg_i', 'hg_g'), 1280),
    (('da_q', 'da_k', 'da_v'), 768),
    (('merge',), 4096),
)
PROJ_WIDTHS = tuple(w for _, w in PROJ_GROUPS)
LANES = 128
OUT_WIDTHS = (4 * GLA_HEADS * GLA_DK + 2 * GLA_HEADS * GLA_DV, PROJ_WIDTHS[1],
              5 * HG_HEADS * HG_DK + 2 * HG_HEADS * HG_DV, PROJ_WIDTHS[3], PROJ_WIDTHS[4])
OUT_DTYPES = (jnp.float32,) * 4 + (jnp.bfloat16,)
VMEM_LIMIT = 56 * 1024 * 1024

BF16 = jnp.bfloat16
F32 = jnp.float32
N_HEADS = 4
RC = 64
N_QK = 2 * DA_HEADS
VROWS = DA_DV + 8
LOG2E = 1.4426950408889634
FFN_MIN_ROWS = 256
HB = 256


def _split_bf16(a):
    hi = a.astype(BF16)
    return hi, (a - hi.astype(F32)).astype(BF16)


def _in_proj_kernel(x_ref, g_ref, sh_ref, sc_ref, w_ref, wa_ref, ba_ref, lb_ref,
                    gla_ref, hy_ref, hg_ref, da_ref, mg_ref):
    x = x_ref[0]
    ms = jnp.mean(x * x, axis=-1, keepdims=True)
    h = x * lax.rsqrt(ms + EPS) * g_ref[...]
    h = (h * (1.0 + sc_ref[0]) + sh_ref[0]).astype(BF16)
    offs = np.cumsum((0,) + PROJ_WIDTHS)

    def proj(k):
        return jnp.dot(h, w_ref[:, int(offs[k]):int(offs[k + 1])], preferred_element_type=F32)

    r = proj(0)
    qk = GLA_HEADS * GLA_DK
    vw = GLA_HEADS * GLA_DV
    gla_ref[0, :, 0:qk] = r[:, 0:qk] * (GLA_DK ** -0.5)
    gla_ref[0, :, qk:2 * qk] = r[:, qk:2 * qk]
    a_hi, a_lo = _split_bf16(r[:, 2 * qk + 2 * vw:])
    z = (jnp.dot(a_hi, wa_ref[0], preferred_element_type=F32) + jnp.dot(a_lo, wa_ref[0], preferred_element_type=F32)
         + jnp.dot(a_hi, wa_ref[1], preferred_element_type=F32) + ba_ref[...])
    gla_ref[0, :, 2 * qk:4 * qk] = jax.nn.log_sigmoid(z) * (1.0 / GLA_TAU)
    gla_ref[0, :, 4 * qk:4 * qk + 2 * vw] = r[:, 2 * qk:2 * qk + 2 * vw]

    hy_ref[0] = proj(1)

    r = proj(2)
    hw = HG_HEADS * HG_DK
    q = r[:, 0:hw]
    hg_ref[0, :, 0:hw] = q * jax.nn.sigmoid(q)
    zf = r[:, hw:3 * hw]
    lb = lb_ref[...]
    hg_ref[0, :, hw:3 * hw] = (1.0 - lb) * jax.nn.sigmoid(-zf)
    hg_ref[0, :, 3 * hw:5 * hw] = jnp.log(jnp.maximum(lb + (1.0 - lb) * jax.nn.sigmoid(zf), F_TINY))
    hg_ref[0, :, 5 * hw:7 * hw] = r[:, 3 * hw:5 * hw]

    da_ref[0] = proj(3)
    mg_ref[0] = (0.5 * proj(4)).astype(mg_ref.dtype)


def _in_proj(x, g, shift, scale, w_groups, wa, ba, lb, tm):
    B, L, D = x.shape
    nw = w_groups.shape[1]
    const2 = lambda b, i: (0, 0)
    return pl.pallas_call(
        _in_proj_kernel,
        grid=(B, L // tm),
        in_specs=[
            pl.BlockSpec((1, tm, D), lambda b, i: (b, i, 0)),
            pl.BlockSpec((1, D), const2),
            pl.BlockSpec((1, 1, D), lambda b, i: (b, 0, 0)),
            pl.BlockSpec((1, 1, D), lambda b, i: (b, 0, 0)),
            pl.BlockSpec((D, nw), const2, pipeline_mode=pl.Buffered(1)),
            pl.BlockSpec(wa.shape, lambda b, i: (0, 0, 0)),
            pl.BlockSpec(ba.shape, const2),
            pl.BlockSpec(lb.shape, const2),
        ],
        out_specs=[pl.BlockSpec((1, tm, w), lambda b, i: (b, i, 0)) for w in OUT_WIDTHS],
        out_shape=[jax.ShapeDtypeStruct((B, L, w), dt) for w, dt in zip(OUT_WIDTHS, OUT_DTYPES)],
        compiler_params=pltpu.CompilerParams(
            dimension_semantics=("parallel", "parallel"), vmem_limit_bytes=VMEM_LIMIT),
        name="in_proj",
    )(x, g.reshape(1, D), shift, scale, w_groups, wa, ba, lb)


def _merge_kernel(o0_ref, o1_ref, o2_ref, o3_ref, gate_ref, wb_ref, wo_ref, x_ref, m_ref, g_ref, sh_ref, sc_ref,
                  out_ref, h_ref):
    D = D_MODEL
    acc = None
    for i, o_ref in enumerate((o0_ref, o1_ref, o2_ref, o3_ref)):
        t = jnp.dot(o_ref[0].astype(BF16), wb_ref[i], preferred_element_type=F32)
        t = (jnp.tanh(gate_ref[0, :, i * D:(i + 1) * D].astype(F32)) + 1.0) * t
        acc = t if acc is None else acc + t
    mx = jnp.dot((0.5 * acc).astype(BF16), wo_ref[...], preferred_element_type=F32)
    out = x_ref[0] + m_ref[0] * mx
    out_ref[0] = out
    ms = jnp.mean(out * out, axis=-1, keepdims=True)
    h_ref[0] = out * lax.rsqrt(ms + EPS) * g_ref[...] * (1.0 + sc_ref[0]) + sh_ref[0]


def _merge_out(outs, gate_cols, w_branch, w_out, x, m, g, shift, scale, tm):
    B, L, D = x.shape
    tok = pl.BlockSpec((1, tm, D), lambda b, i: (b, i, 0))
    per_sample = pl.BlockSpec((1, 1, D), lambda b, i: (b, 0, 0))
    return pl.pallas_call(
        _merge_kernel,
        grid=(B, L // tm),
        in_specs=[pl.BlockSpec((1, tm, MIX_W), lambda b, i: (b, i, 0)) for _ in range(N_BRANCH)] + [
            pl.BlockSpec((1, tm, N_BRANCH * D), lambda b, i: (b, i, 0)),
            pl.BlockSpec((N_BRANCH, MIX_W, D), lambda b, i: (0, 0, 0)),
            pl.BlockSpec((D, D), lambda b, i: (0, 0)),
            tok, per_sample,
            pl.BlockSpec((1, D), lambda b, i: (0, 0)), per_sample, per_sample,
        ],
        out_specs=[tok, tok],
        out_shape=[jax.ShapeDtypeStruct((B, L, D), F32), jax.ShapeDtypeStruct((B, L, D), F32)],
        compiler_params=pltpu.CompilerParams(
            dimension_semantics=("parallel", "parallel"), vmem_limit_bytes=VMEM_LIMIT),
        name="merge_out",
    )(*outs, gate_cols, w_branch, w_out, x, m, g.reshape(1, D), shift, scale)


def _expert_ffn_kernel(x_ref, g_ref, w1_ref, w3_ref, w2_ref, out_ref, w1_scr, w3_scr, w2_scr):
    @pl.when(pl.program_id(1) == 0)
    def _():
        w1_scr[...] = w1_ref[0].astype(BF16)
        w3_scr[...] = w3_ref[0].astype(BF16)
        w2_scr[...] = w2_ref[0].astype(BF16)

    x = x_ref[0, 0].astype(BF16)
    a = jnp.dot(x, w1_scr[...], preferred_element_type=F32)
    b = jnp.dot(x, w3_scr[...], preferred_element_type=F32)
    h = (a * jax.nn.sigmoid(a) * b).astype(BF16)
    y = jnp.dot(h, w2_scr[...], preferred_element_type=F32)
    out_ref[0, 0] = (y * g_ref[0, 0]).astype(out_ref.dtype)


def _combine_kernel(idx_ref, y_ref, x_ref, m_ref, out_ref, *, ks):
    tt = out_ref.shape[1]
    S = y_ref.shape[1]
    base = pl.program_id(1) * tt
    acc = None
    for s0 in range(0, S, ks):
        tok = lax.broadcasted_iota(jnp.int32, (tt, ks), 0) + base
        onehot = jnp.where(tok == idx_ref[0, :, s0:s0 + ks], 1.0, 0.0).astype(BF16)
        part = jnp.dot(onehot, y_ref[0, s0:s0 + ks, :], preferred_element_type=F32)
        acc = part if acc is None else acc + part
    out_ref[0] = x_ref[0] + m_ref[0] * acc


def _combine(idx, y, x, m, tt):
    B, L, D = x.shape
    S = y.shape[1]
    return pl.pallas_call(
        functools.partial(_combine_kernel, ks=min(S, 2048)),
        grid=(B, L // tt),
        in_specs=[
            pl.BlockSpec((1, 1, S), lambda b, i: (b, 0, 0)),
            pl.BlockSpec((1, S, D), lambda b, i: (b, 0, 0)),
            pl.BlockSpec((1, tt, D), lambda b, i: (b, i, 0)),
            pl.BlockSpec((1, 1, D), lambda b, i: (b, 0, 0)),
        ],
        out_specs=pl.BlockSpec((1, tt, D), lambda b, i: (b, i, 0)),
        out_shape=jax.ShapeDtypeStruct((B, L, D), F32),
        compiler_params=pltpu.CompilerParams(
            dimension_semantics=("parallel", "parallel"), vmem_limit_bytes=VMEM_LIMIT),
        name="moe_combine",
    )(idx, y, x, m)


def _expert_ffn(xg, g, w1, w3, w2):
    B, E, cap, D = xg.shape
    F = w1.shape[-1]
    return pl.pallas_call(
        _expert_ffn_kernel,
        grid=(E, B),
        in_specs=[
            pl.BlockSpec((1, 1, cap, D), lambda e, b: (b, e, 0, 0)),
            pl.BlockSpec((1, 1, cap, 1), lambda e, b: (b, e, 0, 0)),
            pl.BlockSpec((1, D, F), lambda e, b: (e, 0, 0)),
            pl.BlockSpec((1, D, F), lambda e, b: (e, 0, 0)),
            pl.BlockSpec((1, F, D), lambda e, b: (e, 0, 0)),
        ],
        out_specs=pl.BlockSpec((1, 1, cap, D), lambda e, b: (b, e, 0, 0)),
        out_shape=jax.ShapeDtypeStruct((B, E, cap, D), BF16),
        scratch_shapes=[pltpu.VMEM((D, F), BF16), pltpu.VMEM((D, F), BF16), pltpu.VMEM((F, D), BF16)],
        compiler_params=pltpu.CompilerParams(
            dimension_semantics=("parallel", "arbitrary"), vmem_limit_bytes=VMEM_LIMIT),
        name="expert_ffn",
    )(xg, g, w1, w3, w2)


def _recur_tables(C, rev):
    pos = (C - 1 - np.arange(C)) if rev else np.arange(C)
    row_of = np.argsort(pos)
    levels = int(np.log2(C))
    diff = np.zeros((levels + 1, C, C), np.float32)
    mask = np.zeros((levels + 1, C, C), np.float32)
    for t in range(C):
        p = pos[t]
        diff[0, t, row_of[:p + 1]] = 1.0
        mask[0, t, t] = 1.0
        for l in range(1, levels + 1):
            n, m = 2 ** l, 2 ** (l - 1)
            off = p % n
            mid = p - off + m - 1
            if off >= m:
                diff[l, t, row_of[mid + 1:p + 1]] = 1.0
                mask[l, t, row_of[p - off:mid + 1]] = 1.0
            else:
                diff[l, t, row_of[p + 1:mid + 1]] = 1.0
    return (jnp.asarray(np.tile(diff.reshape((levels + 1) * C, C), (1, 3)), BF16),
            jnp.asarray(np.tile(mask, (1, 1, N_HEADS))))


def _recur_kernel(*refs, rev, C, nsub, dk, dv, finish):
    if finish:
        (q_ref, k_ref, v_ref, g_ref, s0_ref, d_ref, m_ref, prev_ref, gate_ref, gn_ref,
         o_ref, sfin_ref, s_scr, e_scr, a_scr) = refs
    else:
        q_ref, k_ref, v_ref, g_ref, s0_ref, d_ref, m_ref, o_ref, sfin_ref, s_scr, e_scr, a_scr = refs
    HK = N_HEADS * dk
    HV = N_HEADS * dv
    i = pl.program_id(1)
    if finish:
        head_mean = jnp.where(lax.broadcasted_iota(jnp.int32, (HV, HV), 0) // dv
                              == lax.broadcasted_iota(jnp.int32, (HV, HV), 1) // dv, 1.0 / dv, 0.0).astype(BF16)

    @pl.when(i == 0)
    def _():
        s_scr[...] = s0_ref[0]

    head_k = [(lax.broadcasted_iota(jnp.int32, (1, HK), 1) // dk == h).astype(BF16) for h in range(N_HEADS)]
    head_v = [(lax.broadcasted_iota(jnp.int32, (1, HV), 1) // dv == h).astype(BF16) for h in range(N_HEADS)]
    bd = (lax.broadcasted_iota(jnp.int32, (HV, HK), 0) // dv
          == lax.broadcasted_iota(jnp.int32, (HV, HK), 1) // dk)
    levels = m_ref.shape[0] - 1

    def stack_heads(x, head):
        xb = x.astype(BF16)
        return jnp.concatenate([xb * head[h] for h in range(N_HEADS)], axis=0)

    def nt(a, b):
        return lax.dot_general(a, b, (((1,), (1,)), ((), ())), preferred_element_type=F32)

    order = [(nsub - 1 - j) if rev else j for j in range(nsub)]
    for c in order:
        g = g_ref[0, c * C:(c + 1) * C, :]
        g_hi = g.astype(BF16)
        r1 = g - g_hi.astype(F32)
        g_mid = r1.astype(BF16)
        g_lo = (r1 - g_mid.astype(F32)).astype(BF16)
        e_scr[c] = jnp.dot(d_ref[...], jnp.concatenate([g_hi, g_mid, g_lo], axis=0), preferred_element_type=F32)
    for c in order:
        q = q_ref[0, c * C:(c + 1) * C, :]
        k = k_ref[0, c * C:(c + 1) * C, :]
        a = nt(q.astype(BF16), stack_heads(k, head_k)) * m_ref[0]
        for lvl in range(1, levels + 1):
            x = jnp.exp(e_scr[c, lvl * C:(lvl + 1) * C, :])
            a = a + nt((q * x).astype(BF16), stack_heads(k * x, head_k)) * m_ref[lvl]
        a_scr[c] = a.astype(BF16)
    for c in order:
        sl = slice(c * C, (c + 1) * C)
        q = q_ref[0, sl, :]
        k = k_ref[0, sl, :]
        v = v_ref[0, sl, :]
        b = e_scr[c, 0:C, :]
        st = s_scr[...]
        o = jnp.dot(a_scr[c], stack_heads(v, head_v), preferred_element_type=F32)
        o = o + nt((q * jnp.exp(b)).astype(BF16), st.astype(BF16))
        if finish:
            o = o + prev_ref[0, sl, :]
            sq_hi, sq_lo = _split_bf16(o * o)
            ms = (jnp.dot(sq_hi, head_mean, preferred_element_type=F32)
                  + jnp.dot(sq_lo, head_mean, preferred_element_type=F32))
            gate = gate_ref[0, sl, :]
            o = o * lax.rsqrt(ms + EPS) * gn_ref[...] * (gate * jax.nn.sigmoid(gate))
        o_ref[0, sl, :] = o
        b_end = b[0:1, :] if rev else b[C - 1:C, :]
        kend = (k * jnp.exp(b_end - b)).astype(BF16)
        upd = lax.dot_general(v.astype(BF16), kend, (((0,), (0,)), ((), ())), preferred_element_type=F32)
        s_scr[...] = st * jnp.exp(b_end) + jnp.where(bd, upd, 0.0)

    @pl.when(i == pl.num_programs(1) - 1)
    def _():
        sfin_ref[0] = s_scr[...]


def _recurrence(pack, cols, s0, prev, gnorm, *, rev, dk, dv, tb):
    B, L, _ = pack.shape
    HK, HV = N_HEADS * dk, N_HEADS * dv
    C = RC
    nblk = L // tb
    diff, masks = _recur_tables(C, rev)
    cq, ck, cg, cv, cgate = cols

    def tok(col):
        return (lambda b, i: (b, nblk - 1 - i, col)) if rev else (lambda b, i: (b, i, col))

    in_specs = [
        pl.BlockSpec((1, tb, HK), tok(cq)), pl.BlockSpec((1, tb, HK), tok(ck)),
        pl.BlockSpec((1, tb, HV), tok(cv)), pl.BlockSpec((1, tb, HK), tok(cg)),
        pl.BlockSpec((1, HV, HK), lambda b, i: (b, 0, 0)),
        pl.BlockSpec(diff.shape, lambda b, i: (0, 0)),
        pl.BlockSpec(masks.shape, lambda b, i: (0, 0, 0)),
    ]
    args = [pack, pack, pack, pack, s0, diff, masks]
    if prev is not None:
        in_specs += [pl.BlockSpec((1, tb, HV), tok(0)), pl.BlockSpec((1, tb, HV), tok(cgate)),
                     pl.BlockSpec((1, HV), lambda b, i: (0, 0))]
        args += [prev, pack, gnorm]
    kern = functools.partial(_recur_kernel, rev=rev, C=C, nsub=tb // C, dk=dk, dv=dv, finish=prev is not None)
    return pl.pallas_call(
        kern,
        grid=(B, nblk),
        in_specs=in_specs,
        out_specs=[pl.BlockSpec((1, tb, HV), tok(0)), pl.BlockSpec((1, HV, HK), lambda b, i: (b, 0, 0))],
        out_shape=[jax.ShapeDtypeStruct((B, L, HV), F32), jax.ShapeDtypeStruct((B, HV, HK), F32)],
        scratch_shapes=[pltpu.VMEM((HV, HK), F32), pltpu.VMEM((tb // C, diff.shape[0], HK), F32),
                        pltpu.VMEM((tb // C, C, N_HEADS * C), BF16)],
        compiler_params=pltpu.CompilerParams(
            dimension_semantics=("parallel", "arbitrary"), vmem_limit_bytes=VMEM_LIMIT),
        name="recur_rev" if rev else "recur_fwd",
    )(*args)


def _bidir(pack_c, pack_l, cols_f, cols_b, gnorm, *, dk, dv):
    B, Lc, _ = pack_c.shape
    s0 = jnp.zeros((B, N_HEADS * dv, N_HEADS * dk), F32)
    gn = jnp.tile(gnorm.astype(F32), N_HEADS).reshape(1, N_HEADS * dv)
    kw = dict(dk=dk, dv=dv)
    oc_f, sc_f = _recurrence(pack_c, cols_f, s0, None, None, rev=False, tb=Lc, **kw)
    ol_f, _ = _recurrence(pack_l, cols_f, sc_f, None, None, rev=False, tb=256, **kw)
    oc, sc_b = _recurrence(pack_c, cols_b, s0, oc_f, gn, rev=True, tb=Lc, **kw)
    ol, _ = _recurrence(pack_l, cols_b, sc_b, ol_f, gn, rev=True, tb=256, **kw)
    return oc, ol


def _attn_kernel(lam_ref, qt_ref, k_ref, vt_ref, g_ref, o_ref, qm_scr, s_scr, p_scr, m_scr, a_scr, acc_scr, *,
                 first, tk):
    W = qt_ref.shape[1]
    tq = qt_ref.shape[2]
    Lk = k_ref.shape[1]
    qt = qt_ref[0]
    row_pair = lax.broadcasted_iota(jnp.int32, (W, tq), 0) // DA_DK
    for j in range(N_QK):
        qm_scr[j] = jnp.where(row_pair == j, qt, 0.0).astype(BF16)
    m_scr[...] = jnp.full(m_scr.shape, -jnp.inf, F32)
    acc_scr[...] = jnp.zeros(acc_scr.shape, F32)

    def block(start, size):
        ks = pl.ds(start, size)
        kblk = k_ref[0, ks, :]
        for j in range(N_QK):
            s_scr[j, :size] = jnp.dot(kblk, qm_scr[j], preferred_element_type=F32)
        for j in range(N_QK):
            s = s_scr[j, :size]
            m_old = m_scr[j]
            m_new = jnp.maximum(m_old, jnp.max(s, axis=0, keepdims=True))
            a_scr[j] = jnp.exp2(m_old - m_new)
            p_scr[j, :size] = jnp.exp2(s - m_new).astype(BF16)
            m_scr[j] = m_new
        for j in range(N_QK):
            pv = jnp.dot(vt_ref[0, j // 2, :, ks], p_scr[j, :size], preferred_element_type=F32)
            acc_scr[j] = a_scr[j] * acc_scr[j] + pv

    block(0, first)

    def body(i, carry):
        block(pl.multiple_of(first + i * tk, math.gcd(first, tk)), tk)
        return carry

    lax.fori_loop(0, (Lk - first) // tk, body, 0)
    lam = lam_ref[0]
    outs = []
    for h in range(DA_HEADS):
        a1, a2 = acc_scr[2 * h], acc_scr[2 * h + 1]
        o = a1[:DA_DV] / a1[DA_DV:DA_DV + 1] - lam * (a2[:DA_DV] / a2[DA_DV:DA_DV + 1])
        ms = jnp.mean(o * o, axis=0, keepdims=True)
        outs.append(o * lax.rsqrt(ms + EPS) * g_ref[...])
    o_ref[0] = jnp.concatenate(outs, axis=0).T


def _diff_attention(qt, k, vt, lam, gcol, *, tq, first, tk):
    B, W, Lq = qt.shape
    Lk = k.shape[1]
    assert (Lk - first) % tk == 0
    rows = max(first, tk)
    HV = DA_HEADS * DA_DV
    return pl.pallas_call(
        functools.partial(_attn_kernel, first=first, tk=tk),
        grid=(B, Lq // tq),
        in_specs=[
            pl.BlockSpec(memory_space=pltpu.SMEM),
            pl.BlockSpec((1, W, tq), lambda b, i: (b, 0, i)),
            pl.BlockSpec((1, Lk, W), lambda b, i: (b, 0, 0)),
            pl.BlockSpec((1, DA_HEADS, VROWS, Lk), lambda b, i: (b, 0, 0, 0)),
            pl.BlockSpec((DA_DV, 1), lambda b, i: (0, 0)),
        ],
        out_specs=pl.BlockSpec((1, tq, HV), lambda b, i: (b, i, 0)),
        out_shape=jax.ShapeDtypeStruct((B, Lq, HV), F32),
        scratch_shapes=[
            pltpu.VMEM((N_QK, W, tq), BF16),
            pltpu.VMEM((N_QK, rows, tq), F32),
            pltpu.VMEM((N_QK, rows, tq), BF16),
            pltpu.VMEM((N_QK, 1, tq), F32),
            pltpu.VMEM((N_QK, 1, tq), F32),
            pltpu.VMEM((N_QK, VROWS, tq), F32),
        ],
        compiler_params=pltpu.CompilerParams(
            dimension_semantics=("parallel", "parallel"), vmem_limit_bytes=VMEM_LIMIT),
        name="diff_attn",
    )(lam, qt, k, vt, gcol)


def _hyena_kernel(w_ref, bias_ref, uv_ref, u1_ref, u2_ref, hp_ref, o_ref, acc_scr, *, nb, nbatch):
    c = pl.program_id(0)
    half = nb // 2
    rows = nb * nbatch
    lane = lax.broadcasted_iota(jnp.int32, (rows, HB), 1)
    row = lax.broadcasted_iota(jnp.int32, (rows, HB), 0)

    def short_conv(u_ref, ch):
        u = u_ref[0].reshape(rows, HB)
        prev = pltpu.roll(u, 1, axis=1)
        prev = jnp.where(lane == 0, jnp.where(row >= nbatch, pltpu.roll(prev, nbatch % rows, axis=0), 0.0), prev)
        nxt = pltpu.roll(u, HB - 1, axis=1)
        nxt = jnp.where(lane == HB - 1,
                        jnp.where(row < rows - nbatch, pltpu.roll(nxt, (rows - nbatch) % rows, axis=0), 0.0), nxt)
        return w_ref[0, ch] * prev + w_ref[1, ch] * u + w_ref[2, ch] * nxt

    def long_conv(z, o):
        zb = z.astype(BF16)
        hrow = hp_ref[o, 0]
        acc_scr[...] = jnp.zeros(acc_scr.shape, F32)
        for off in range(-half, half + 1):
            lo = (off + half) * HB
            r = jnp.broadcast_to(hrow[:, lo:lo + 2 * HB], (HB, 2 * HB))
            w = pltpu.roll(r, 1, 1, stride=1, stride_axis=0)[:, HB:].astype(BF16)
            s0, s1 = max(0, -off), nb - max(0, off)
            if s1 <= s0:
                continue
            src = slice(s0 * nbatch, s1 * nbatch)
            dst = slice((s0 + off) * nbatch, (s1 + off) * nbatch)
            acc_scr[dst, :] += jnp.dot(zb[src], w, preferred_element_type=F32)
        return acc_scr[...]

    z = short_conv(uv_ref, c)
    for o, u_ref in enumerate((u1_ref, u2_ref)):
        gate = short_conv(u_ref, (o + 1) * HY_WIDTH + c)
        z = gate * (long_conv(z, o) + bias_ref[o, c] * z)
    o_ref[0] = z.reshape(nb, nbatch, HB)


def _hyena_fused(ut, conv_w, hp, bias):
    _, nb, B, _ = ut.shape
    C = HY_WIDTH
    blk = lambda k: pl.BlockSpec((1, nb, B, HB), lambda c: (c + k * C, 0, 0, 0))
    return pl.pallas_call(
        functools.partial(_hyena_kernel, nb=nb, nbatch=B),
        grid=(C,),
        in_specs=[
            pl.BlockSpec(memory_space=pltpu.SMEM),
            pl.BlockSpec(memory_space=pltpu.SMEM),
            blk(0), blk(1), blk(2),
            pl.BlockSpec((HY_ORDER, 1, 1, hp.shape[-1]), lambda c: (0, c, 0, 0)),
        ],
        out_specs=pl.BlockSpec((1, nb, B, HB), lambda c: (c, 0, 0, 0)),
        out_shape=jax.ShapeDtypeStruct((C, nb, B, HB), F32),
        scratch_shapes=[pltpu.VMEM((nb * B, HB), F32)],
        compiler_params=pltpu.CompilerParams(
            dimension_semantics=("parallel",), vmem_limit_bytes=VMEM_LIMIT),
        name="hyena_branch",
    )(conv_w, bias, ut, ut, ut, hp)


def _rms_norm(x, g):
    xf = x.astype(F32)
    y = xf * lax.rsqrt(jnp.mean(xf * xf, axis=-1, keepdims=True) + EPS)
    return (y * g.astype(F32)).astype(x.dtype)


def _gate_weights(wa2, ba):
    qk = GLA_HEADS * GLA_DK
    w = jnp.zeros((LANES, 2 * qk), F32)
    w = w.at[:GLA_RANK, :qk].set(wa2[0].astype(F32)).at[GLA_RANK:2 * GLA_RANK, qk:].set(wa2[1].astype(F32))
    hi = w.astype(BF16)
    lo = (w - hi.astype(F32)).astype(BF16)
    return jnp.stack([hi, lo]), ba.astype(F32).reshape(1, 2 * qk)


def _hyena_filters(L, p):
    j = jnp.arange(L, dtype=F32)
    t = j / max(L - 1, 1)
    w = 2 * math.pi * j / L
    f = jnp.linspace(1e-4, HY_BANDS - 1, HY_BANDS, dtype=F32)
    feats = jnp.concatenate([t[:, None], jnp.cos(w[:, None] * f), -jnp.sin(w[:, None] * f)], axis=-1)
    h = jnp.sin(p['hy_freq'][0] * (feats @ p['hy_w1'] + p['hy_b1']))
    h = jnp.sin(p['hy_freq'][1] * (h @ p['hy_w2'] + p['hy_b2']))
    h = (h @ p['hy_w3']).astype(F32)
    dist = jnp.abs(j - L // 2) / (L // 2)
    h = h * (jnp.exp(-dist[:, None] * jnp.abs(p['hy_decay'].astype(F32))) + HY_SHIFT)
    h = h / jnp.sum(jnp.abs(h), axis=0, keepdims=True)
    return h.reshape(L, HY_ORDER, HY_WIDTH)


def _to_blocks(a):
    B, L, C = a.shape
    return a.reshape(B, L // HB, HB, C).transpose(3, 1, 0, 2)


def _from_blocks(a):
    C, nb, B, _ = a.shape
    return a.transpose(2, 1, 3, 0).reshape(B, nb * HB, C)


def _pad_filter(h):
    L = h.shape[0]
    nb = L // HB
    left = (HB - 1) - (L // 2 - (nb // 2) * HB)
    return jnp.pad(h.T, ((0, 0), (left, (nb + 2) * HB - L - left)))[:, None, :]


def _hyena_branch(u3, p):
    h = _hyena_filters(u3.shape[1], p)
    hp = jnp.stack([_pad_filter(h[:, o]) for o in range(HY_ORDER)])
    return _from_blocks(_hyena_fused(_to_blocks(u3), p['hy_conv_w'], hp, p['hy_bias']))


def _rope_tables(L):
    quarter = DA_DK // 4
    freqs = ROPE_BASE ** (-jnp.arange(quarter, dtype=F32) / quarter)
    row = (jnp.arange(L) // GRID_W).astype(F32)[:, None] * freqs
    col = (jnp.arange(L) % GRID_W).astype(F32)[:, None] * freqs
    return jnp.cos(row), jnp.sin(row), jnp.cos(col), jnp.sin(col)


def _da_prep(t, g, tables):
    B, L, W = t.shape
    t = _rms_norm(t.reshape(B, L, N_QK, DA_DK), g)
    if tables is not None:
        cr, sr, cc, sc = [a[None, :, None, :] for a in tables]
        e = DA_DK // 4
        a1, a2, b1, b2 = t[..., :e], t[..., e:2 * e], t[..., 2 * e:3 * e], t[..., 3 * e:]
        t = jnp.concatenate([a1 * cr - a2 * sr, a1 * sr + a2 * cr, b1 * cc - b2 * sc, b1 * sc + b2 * cc], axis=-1)
    return t.reshape(B, L, W)


def _value_rows(v):
    B, Lk, _ = v.shape
    vt = v.reshape(B, Lk, DA_HEADS, DA_DV).transpose(0, 2, 3, 1)
    extra = jnp.concatenate([jnp.ones((B, DA_HEADS, 1, Lk), v.dtype),
                             jnp.zeros((B, DA_HEADS, VROWS - DA_DV - 1, Lk), v.dtype)], axis=2)
    return jnp.concatenate([vt, extra], axis=2).astype(BF16)


def _da_parts(da):
    w = DA_HEADS * 2 * DA_DK
    return da[..., :w], da[..., w:2 * w], da[..., 2 * w:]


def _group_weights(w_in):
    cols = []
    for names, width in PROJ_GROUPS:
        used = 0
        for nm in names:
            i = IN_NAMES.index(nm)
            cols.append(w_in[:, IN_OFFSETS[i]:IN_OFFSETS[i] + IN_WIDTHS[i]])
            used += IN_WIDTHS[i]
        if width > used:
            cols.append(jnp.zeros((w_in.shape[0], width - used), w_in.dtype))
    return jnp.concatenate(cols, axis=1).astype(BF16)


GLA_COLS_F, GLA_COLS_B = (0, 1, 2, 2, 3), (0, 1, 3, 2, 3)
HG_COLS_F, HG_COLS_B = (0, 1, 3, 5, 6), (0, 2, 4, 5, 6)


def _token_mixer(gc, gx, p, lam_init, last):
    gla_c, gla_x = _bidir(gc[0], gx[0], GLA_COLS_F, GLA_COLS_B, p['gla_norm_g'], dk=GLA_DK, dv=GLA_DV)
    hg_c, hg_x = _bidir(gc[2], gx[2], HG_COLS_F, HG_COLS_B, p['hg_norm_g'], dk=HG_DK, dv=HG_DV)
    (dqc, dkc, dvc), (dqx, dkx, dvx) = _da_parts(gc[3]), _da_parts(gx[3])
    tables = _rope_tables(dqx.shape[1])
    scale = DA_DK ** -0.5 * LOG2E
    qx = (_da_prep(dqx, p['da_qnorm_g'], tables) * scale).transpose(0, 2, 1)
    kx = _da_prep(dkx, p['da_knorm_g'], tables)
    kc = _da_prep(dkc, p['da_knorm_g'], None)
    k_all = jnp.concatenate([kc, kx], axis=1).astype(BF16)
    vt_all = _value_rows(jnp.concatenate([dvc, dvx], axis=1))
    lp = p['da_lam'].astype(F32)
    lam = (jnp.exp(jnp.sum(lp[0] * lp[1])) - jnp.exp(jnp.sum(lp[2] * lp[3])) + lam_init).reshape(1)
    gcol = (p['da_norm_g'].astype(F32) * (1 - lam_init)).reshape(DA_DV, 1)
    Lc = kc.shape[1]
    da_x = _diff_attention(qx, k_all, vt_all, lam, gcol, tq=256, first=Lc, tk=512)
    outs_x = (gla_x, _hyena_branch(gx[1], p), hg_x, da_x)
    if last:
        return None, outs_x
    qc = (_da_prep(dqc, p['da_qnorm_g'], None) * scale).transpose(0, 2, 1)
    da_c = _diff_attention(qc, k_all[:, :Lc], vt_all[..., :Lc], lam, gcol, tq=Lc, first=Lc, tk=Lc)
    return (gla_c, _hyena_branch(gc[1], p), hg_c, da_c), outs_x


def _expert_choice_moe(x, m, h, router, w1, w3, w2):
    B, L, D = h.shape
    cap = CAPACITY_FACTOR * L // N_EXPERTS
    aff = jax.nn.softmax((h @ router).astype(F32), axis=-1)
    g, idx = lax.top_k(aff.transpose(0, 2, 1), cap)
    xg = jax.vmap(lambda hb, ib: hb[ib])(h, idx)
    if cap < FFN_MIN_ROWS:
        regroup = lambda t: t.transpose(1, 0, 2, 3).reshape(1, N_EXPERTS, B * cap, t.shape[-1])
        y = _expert_ffn(regroup(xg), regroup(g[..., None]), w1, w3, w2)
        y = y.reshape(N_EXPERTS, B, cap, D).transpose(1, 0, 2, 3)
    else:
        y = _expert_ffn(xg, g[..., None], w1, w3, w2)
    return _combine(idx.reshape(B, 1, N_EXPERTS * cap), y.reshape(B, N_EXPERTS * cap, D), x, m, tt=min(L, 512))


def kernel(x, c, ctx, c_ctx, ada_w, ada_b, norm1_g, norm2_g, w_in, gla_wa2, gla_ba, gla_norm_g,
           hy_conv_w, hy_w1, hy_b1, hy_w2, hy_b2, hy_w3, hy_freq, hy_decay, hy_bias,
           hg_lower, hg_norm_g, da_qnorm_g, da_knorm_g, da_lam, da_norm_g, w_branch, w_out,
           moe_router, moe_w1, moe_w3, moe_w2):
    B, L, D = x.shape
    P = jax.nn.softmax(hg_lower.astype(F32), axis=0)
    lower = jnp.cumsum(P, axis=0) - P[0]
    sc = jax.nn.silu(c)
    scc = jax.nn.silu(c_ctx)
    xc, xx = ctx, x
    for l in range(DEPTH):
        last = l == DEPTH - 1
        lam_init = 0.8 - 0.6 * math.exp(-0.3 * l)
        p = {'gla_wa2': gla_wa2[l], 'gla_ba': gla_ba[l], 'gla_norm_g': gla_norm_g[l],
             'hy_conv_w': hy_conv_w[l], 'hy_w1': hy_w1[l], 'hy_b1': hy_b1[l], 'hy_w2': hy_w2[l], 'hy_b2': hy_b2[l],
             'hy_w3': hy_w3[l], 'hy_freq': hy_freq[l], 'hy_decay': hy_decay[l], 'hy_bias': hy_bias[l],
             'hg_norm_g': hg_norm_g[l], 'da_qnorm_g': da_qnorm_g[l], 'da_knorm_g': da_knorm_g[l],
             'da_lam': da_lam[l], 'da_norm_g': da_norm_g[l]}
        mod_x = jnp.split((sc @ ada_w[l] + ada_b[l])[:, None, :], ADA_CHUNKS, axis=-1)
        mod_c1 = jnp.split((scc @ ada_w[l] + ada_b[l])[None, None, :], ADA_CHUNKS, axis=-1)
        mod_c = [jnp.broadcast_to(m, (B, 1, D)) for m in mod_c1]
        wg = _group_weights(w_in[l])
        wa, ba = _gate_weights(gla_wa2[l], gla_ba[l])
        lb = lower[l].reshape(1, 2 * HG_HEADS * HG_DK)
        wb = w_branch[l].astype(BF16)
        wo = w_out[l].astype(BF16)
        w1, w3, w2 = moe_w1[l], moe_w3[l], moe_w2[l]
        gx = _in_proj(xx, norm1_g[l], mod_x[0], mod_x[1], wg, wa, ba, lb, tm=256)
        gc = _in_proj(xc, norm1_g[l], mod_c[0], mod_c[1], wg, wa, ba, lb, tm=256)
        outs_c, outs_x = _token_mixer(gc, gx, p, lam_init, last)
        xx, hx = _merge_out(outs_x, gx[4], wb, wo, xx, mod_x[2], norm2_g[l], mod_x[3], mod_x[4], tm=512)
        xx = _expert_choice_moe(xx, mod_x[5], hx, moe_router[l], w1, w3, w2)
        if not last:
            xc, hc = _merge_out(outs_c, gc[4], wb, wo, xc, mod_c[2], norm2_g[l], mod_c[3], mod_c[4], tm=256)
            xc = _expert_choice_moe(xc, mod_c[5], hc, moe_router[l], w1, w3, w2)
    return xx
```

```python
import functools
import math

import jax
import jax.numpy as jnp
import numpy as np
from jax import lax
from jax.experimental import pallas as pl
from jax.experimental.pallas import tpu as pltpu

D_MODEL = 1024
DEPTH = 2
GRID_W = 64
N_BRANCH = 4
MIX_W = 256
GLA_HEADS = 4
GLA_DK = 32
GLA_DV = 64
GLA_RANK = 16
GLA_TAU = 16.0
HY_WIDTH = 256
HY_ORDER = 2
HY_BANDS = 16
HY_SHIFT = 0.05
HG_HEADS = 4
HG_DK = 64
HG_DV = 64
DA_HEADS = 4
DA_DK = 32
DA_DV = 64
ROPE_BASE = 10000.0
N_EXPERTS = 16
EXPERT_FF = 1024
CAPACITY_FACTOR = 2
ADA_CHUNKS = 6
EPS = 1e-6
F_TINY = 1e-20

IN_NAMES = ('gla_q', 'gla_k', 'gla_v', 'gla_af', 'gla_ab', 'gla_g', 'hy',
            'hg_q', 'hg_ff', 'hg_fb', 'hg_i', 'hg_g', 'da_q', 'da_k', 'da_v', 'merge')
IN_WIDTHS = (GLA_HEADS * GLA_DK, GLA_HEADS * GLA_DK, GLA_HEADS * GLA_DV, GLA_RANK, GLA_RANK, GLA_HEADS * GLA_DV,
             (1 + HY_ORDER) * HY_WIDTH,
             HG_HEADS * HG_DK, HG_HEADS * HG_DK, HG_HEADS * HG_DK, HG_HEADS * HG_DV, HG_HEADS * HG_DV,
             DA_HEADS * 2 * DA_DK, DA_HEADS * 2 * DA_DK, DA_HEADS * DA_DV,
             N_BRANCH * D_MODEL)
IN_OFFSETS = tuple(int(v) for v in np.cumsum((0,) + IN_WIDTHS)[:-1])

PROJ_GROUPS = (
    (('gla_q', 'gla_k', 'gla_v', 'gla_g', 'gla_af', 'gla_ab'), 896),
    (('hy',), 768),
    (('hg_q', 'hg_ff', 'hg_fb', 'hg_i', 'hg_g'), 1280),
    (('da_q', 'da_k', 'da_v'), 768),
    (('merge',), 4096),
)
PROJ_WIDTHS = tuple(w for _, w in PROJ_GROUPS)
LANES = 128
OUT_WIDTHS = (4 * GLA_HEADS * GLA_DK + 2 * GLA_HEADS * GLA_DV, PROJ_WIDTHS[1],
              5 * HG_HEADS * HG_DK + 2 * HG_HEADS * HG_DV, PROJ_WIDTHS[3], PROJ_WIDTHS[4])
OUT_DTYPES = (jnp.float32,) * 4 + (jnp.bfloat16,)
VMEM_LIMIT = 56 * 1024 * 1024

BF16 = jnp.bfloat16
F32 = jnp.float32
N_HEADS = 4
RC = 64
N_QK = 2 * DA_HEADS
VROWS = DA_DV + 8
LOG2E = 1.4426950408889634
FFN_MIN_ROWS = 256
HB = 256


def _split_bf16(a):
    hi = a.astype(BF16)
    return hi, (a - hi.astype(F32)).astype(BF16)


def _in_proj_kernel(x_ref, g_ref, sh_ref, sc_ref, w_ref, wa_ref, ba_ref, lb_ref,
                    gla_ref, hy_ref, hg_ref, da_ref, mg_ref):
    x = x_ref[0]
    ms = jnp.mean(x * x, axis=-1, keepdims=True)
    h = x * lax.rsqrt(ms + EPS) * g_ref[...]
    h = (h * (1.0 + sc_ref[0]) + sh_ref[0]).astype(BF16)
    offs = np.cumsum((0,) + PROJ_WIDTHS)

    def proj(k):
        return jnp.dot(h, w_ref[:, int(offs[k]):int(offs[k + 1])], preferred_element_type=F32)

    r = proj(0)
    qk = GLA_HEADS * GLA_DK
    vw = GLA_HEADS * GLA_DV
    gla_ref[0, :, 0:qk] = r[:, 0:qk] * (GLA_DK ** -0.5)
    gla_ref[0, :, qk:2 * qk] = r[:, qk:2 * qk]
    a_hi, a_lo = _split_bf16(r[:, 2 * qk + 2 * vw:])
    z = (jnp.dot(a_hi, wa_ref[0], preferred_element_type=F32) + jnp.dot(a_lo, wa_ref[0], preferred_element_type=F32)
         + jnp.dot(a_hi, wa_ref[1], preferred_element_type=F32) + ba_ref[...])
    gla_ref[0, :, 2 * qk:4 * qk] = jax.nn.log_sigmoid(z) * (1.0 / GLA_TAU)
    gla_ref[0, :, 4 * qk:4 * qk + 2 * vw] = r[:, 2 * qk:2 * qk + 2 * vw]

    hy_ref[0] = proj(1).T

    r = proj(2)
    hw = HG_HEADS * HG_DK
    q = r[:, 0:hw]
    hg_ref[0, :, 0:hw] = q * jax.nn.sigmoid(q)
    zf = r[:, hw:3 * hw]
    lb = lb_ref[...]
    hg_ref[0, :, hw:3 * hw] = (1.0 - lb) * jax.nn.sigmoid(-zf)
    hg_ref[0, :, 3 * hw:5 * hw] = jnp.log(jnp.maximum(lb + (1.0 - lb) * jax.nn.sigmoid(zf), F_TINY))
    hg_ref[0, :, 5 * hw:7 * hw] = r[:, 3 * hw:5 * hw]

    da_ref[0] = proj(3)
    mg_ref[0] = (0.5 * proj(4)).astype(mg_ref.dtype)


def _in_proj(x, g, shift, scale, w_groups, wa, ba, lb, tm):
    B, L, D = x.shape
    nw = w_groups.shape[1]
    const2 = lambda b, i: (0, 0)
    out_specs = [pl.BlockSpec((1, tm, w), lambda b, i: (b, i, 0)) for w in OUT_WIDTHS]
    out_shape = [jax.ShapeDtypeStruct((B, L, w), dt) for w, dt in zip(OUT_WIDTHS, OUT_DTYPES)]
    assert tm == HB
    out_specs[1] = pl.BlockSpec((1, OUT_WIDTHS[1], tm), lambda b, i: (i, 0, b))
    out_shape[1] = jax.ShapeDtypeStruct((L // tm, OUT_WIDTHS[1], B * tm), F32)
    return pl.pallas_call(
        _in_proj_kernel,
        grid=(B, L // tm),
        in_specs=[
            pl.BlockSpec((1, tm, D), lambda b, i: (b, i, 0)),
            pl.BlockSpec((1, D), const2),
            pl.BlockSpec((1, 1, D), lambda b, i: (b, 0, 0)),
            pl.BlockSpec((1, 1, D), lambda b, i: (b, 0, 0)),
            pl.BlockSpec((D, nw), const2, pipeline_mode=pl.Buffered(1)),
            pl.BlockSpec(wa.shape, lambda b, i: (0, 0, 0)),
            pl.BlockSpec(ba.shape, const2),
            pl.BlockSpec(lb.shape, const2),
        ],
        out_specs=out_specs,
        out_shape=out_shape,
        compiler_params=pltpu.CompilerParams(
            dimension_semantics=("parallel", "parallel"), vmem_limit_bytes=VMEM_LIMIT),
        name="in_proj",
    )(x, g.reshape(1, D), shift, scale, w_groups, wa, ba, lb)


def _merge_kernel(o0_ref, o1_ref, o2_ref, o3_ref, gate_ref, wb_ref, wo_ref, x_ref, m_ref, g_ref, sh_ref, sc_ref,
                  out_ref, h_ref):
    D = D_MODEL
    acc = None
    for i, o_ref in enumerate((o0_ref, o1_ref, o2_ref, o3_ref)):
        t = jnp.dot(o_ref[0].astype(BF16), wb_ref[i], preferred_element_type=F32)
        t = (jnp.tanh(gate_ref[0, :, i * D:(i + 1) * D].astype(F32)) + 1.0) * t
        acc = t if acc is None else acc + t
    mx = jnp.dot((0.5 * acc).astype(BF16), wo_ref[...], preferred_element_type=F32)
    out = x_ref[0] + m_ref[0] * mx
    out_ref[0] = out
    ms = jnp.mean(out * out, axis=-1, keepdims=True)
    h_ref[0] = out * lax.rsqrt(ms + EPS) * g_ref[...] * (1.0 + sc_ref[0]) + sh_ref[0]


def _merge_out(outs, gate_cols, w_branch, w_out, x, m, g, shift, scale, tm):
    B, L, D = x.shape
    tok = pl.BlockSpec((1, tm, D), lambda b, i: (b, i, 0))
    per_sample = pl.BlockSpec((1, 1, D), lambda b, i: (b, 0, 0))
    return pl.pallas_call(
        _merge_kernel,
        grid=(B, L // tm),
        in_specs=[pl.BlockSpec((1, tm, MIX_W), lambda b, i: (b, i, 0)) for _ in range(N_BRANCH)] + [
            pl.BlockSpec((1, tm, N_BRANCH * D), lambda b, i: (b, i, 0)),
            pl.BlockSpec((N_BRANCH, MIX_W, D), lambda b, i: (0, 0, 0)),
            pl.BlockSpec((D, D), lambda b, i: (0, 0)),
            tok, per_sample,
            pl.BlockSpec((1, D), lambda b, i: (0, 0)), per_sample, per_sample,
        ],
        out_specs=[tok, tok],
        out_shape=[jax.ShapeDtypeStruct((B, L, D), F32), jax.ShapeDtypeStruct((B, L, D), F32)],
        compiler_params=pltpu.CompilerParams(
            dimension_semantics=("parallel", "parallel"), vmem_limit_bytes=VMEM_LIMIT),
        name="merge_out",
    )(*outs, gate_cols, w_branch, w_out, x, m, g.reshape(1, D), shift, scale)


def _expert_ffn_kernel(x_ref, g_ref, w1_ref, w3_ref, w2_ref, out_ref, w1_scr, w3_scr, w2_scr):
    @pl.when(pl.program_id(1) == 0)
    def _():
        w1_scr[...] = w1_ref[0].astype(BF16)
        w3_scr[...] = w3_ref[0].astype(BF16)
        w2_scr[...] = w2_ref[0].astype(BF16)

    x = x_ref[0, 0].astype(BF16)
    a = jnp.dot(x, w1_scr[...], preferred_element_type=F32)
    b = jnp.dot(x, w3_scr[...], preferred_element_type=F32)
    h = (a * jax.nn.sigmoid(a) * b).astype(BF16)
    y = jnp.dot(h, w2_scr[...], preferred_element_type=F32)
    out_ref[0, 0] = (y * g_ref[0, 0]).astype(out_ref.dtype)


def _combine_kernel(idx_ref, y_ref, x_ref, m_ref, out_ref, *, ks):
    tt = out_ref.shape[1]
    S = y_ref.shape[1]
    base = pl.program_id(1) * tt
    acc = None
    for s0 in range(0, S, ks):
        tok = lax.broadcasted_iota(jnp.int32, (tt, ks), 0) + base
        onehot = jnp.where(tok == idx_ref[0, :, s0:s0 + ks], 1.0, 0.0).astype(BF16)
        part = jnp.dot(onehot, y_ref[0, s0:s0 + ks, :], preferred_element_type=F32)
        acc = part if acc is None else acc + part
    out_ref[0] = x_ref[0] + m_ref[0] * acc


def _combine(idx, y, x, m, tt):
    B, L, D = x.shape
    S = y.shape[1]
    return pl.pallas_call(
        functools.partial(_combine_kernel, ks=min(S, 2048)),
        grid=(B, L // tt),
        in_specs=[
            pl.BlockSpec((1, 1, S), lambda b, i: (b, 0, 0)),
            pl.BlockSpec((1, S, D), lambda b, i: (b, 0, 0)),
            pl.BlockSpec((1, tt, D), lambda b, i: (b, i, 0)),
            pl.BlockSpec((1, 1, D), lambda b, i: (b, 0, 0)),
        ],
        out_specs=pl.BlockSpec((1, tt, D), lambda b, i: (b, i, 0)),
        out_shape=jax.ShapeDtypeStruct((B, L, D), F32),
        compiler_params=pltpu.CompilerParams(
            dimension_semantics=("parallel", "parallel"), vmem_limit_bytes=VMEM_LIMIT),
        name="moe_combine",
    )(idx, y, x, m)


def _expert_ffn(xg, g, w1, w3, w2, layer):
    B, E, cap, D = xg.shape
    F = w1.shape[-1]
    return pl.pallas_call(
        _expert_ffn_kernel,
        grid=(E, B),
        in_specs=[
            pl.BlockSpec((1, 1, cap, D), lambda e, b: (b, e, 0, 0)),
            pl.BlockSpec((1, 1, cap, 1), lambda e, b: (b, e, 0, 0)),
            pl.BlockSpec((None, 1, D, F), lambda e, b: (layer, e, 0, 0)),
            pl.BlockSpec((None, 1, D, F), lambda e, b: (layer, e, 0, 0)),
            pl.BlockSpec((None, 1, F, D), lambda e, b: (layer, e, 0, 0)),
        ],
        out_specs=pl.BlockSpec((1, 1, cap, D), lambda e, b: (b, e, 0, 0)),
        out_shape=jax.ShapeDtypeStruct((B, E, cap, D), BF16),
        scratch_shapes=[pltpu.VMEM((D, F), BF16), pltpu.VMEM((D, F), BF16), pltpu.VMEM((F, D), BF16)],
        compiler_params=pltpu.CompilerParams(
            dimension_semantics=("parallel", "arbitrary"), vmem_limit_bytes=VMEM_LIMIT),
        name="expert_ffn",
    )(xg, g, w1, w3, w2)


def _recur_tables(C, rev):
    pos = (C - 1 - np.arange(C)) if rev else np.arange(C)
    row_of = np.argsort(pos)
    levels = int(np.log2(C))
    diff = np.zeros((levels + 1, C, C), np.float32)
    mask = np.zeros((levels + 1, C, C), np.float32)
    for t in range(C):
        p = pos[t]
        diff[0, t, row_of[:p + 1]] = 1.0
        mask[0, t, t] = 1.0
        for l in range(1, levels + 1):
            n, m = 2 ** l, 2 ** (l - 1)
            off = p % n
            mid = p - off + m - 1
            if off >= m:
                diff[l, t, row_of[mid + 1:p + 1]] = 1.0
                mask[l, t, row_of[p - off:mid + 1]] = 1.0
            else:
                diff[l, t, row_of[p + 1:mid + 1]] = 1.0
    return (jnp.asarray(np.tile(diff.reshape((levels + 1) * C, C), (1, 3)), BF16),
            jnp.asarray(np.tile(mask, (1, 1, N_HEADS))))


def _recur_kernel(*refs, rev, C, nsub, dk, dv, finish):
    if finish:
        (q_ref, k_ref, v_ref, g_ref, s0_ref, d_ref, m_ref, prev_ref, gate_ref, gn_ref,
         o_ref, sfin_ref, s_scr, e_scr, a_scr) = refs
    else:
        q_ref, k_ref, v_ref, g_ref, s0_ref, d_ref, m_ref, o_ref, sfin_ref, s_scr, e_scr, a_scr = refs
    HK = N_HEADS * dk
    HV = N_HEADS * dv
    i = pl.program_id(1)
    if finish:
        head_mean = jnp.where(lax.broadcasted_iota(jnp.int32, (HV, HV), 0) // dv
                              == lax.broadcasted_iota(jnp.int32, (HV, HV), 1) // dv, 1.0 / dv, 0.0).astype(BF16)

    @pl.when(i == 0)
    def _():
        s_scr[...] = s0_ref[0]

    head_k = [(lax.broadcasted_iota(jnp.int32, (1, HK), 1) // dk == h).astype(BF16) for h in range(N_HEADS)]
    head_v = [(lax.broadcasted_iota(jnp.int32, (1, HV), 1) // dv == h).astype(BF16) for h in range(N_HEADS)]
    bd = (lax.broadcasted_iota(jnp.int32, (HV, HK), 0) // dv
          == lax.broadcasted_iota(jnp.int32, (HV, HK), 1) // dk)
    levels = m_ref.shape[0] - 1

    def stack_heads(x, head):
        xb = x.astype(BF16)
        return jnp.concatenate([xb * head[h] for h in range(N_HEADS)], axis=0)

    def nt(a, b):
        return lax.dot_general(a, b, (((1,), (1,)), ((), ())), preferred_element_type=F32)

    order = [(nsub - 1 - j) if rev else j for j in range(nsub)]
    for c in order:
        g = g_ref[0, c * C:(c + 1) * C, :]
        g_hi = g.astype(BF16)
        r1 = g - g_hi.astype(F32)
        g_mid = r1.astype(BF16)
        g_lo = (r1 - g_mid.astype(F32)).astype(BF16)
        e_scr[c] = jnp.dot(d_ref[...], jnp.concatenate([g_hi, g_mid, g_lo], axis=0), preferred_element_type=F32)
    for c in order:
        q = q_ref[0, c * C:(c + 1) * C, :]
        k = k_ref[0, c * C:(c + 1) * C, :]
        a = nt(q.astype(BF16), stack_heads(k, head_k)) * m_ref[0]
        for lvl in range(1, levels + 1):
            x = jnp.exp(e_scr[c, lvl * C:(lvl + 1) * C, :])
            a = a + nt((q * x).astype(BF16), stack_heads(k * x, head_k)) * m_ref[lvl]
        a_scr[c] = a.astype(BF16)
    for c in order:
        sl = slice(c * C, (c + 1) * C)
        q = q_ref[0, sl, :]
        k = k_ref[0, sl, :]
        v = v_ref[0, sl, :]
        b = e_scr[c, 0:C, :]
        st = s_scr[...]
        o = jnp.dot(a_scr[c], stack_heads(v, head_v), preferred_element_type=F32)
        o = o + nt((q * jnp.exp(b)).astype(BF16), st.astype(BF16))
        if finish:
            o = o + prev_ref[0, sl, :]
            sq_hi, sq_lo = _split_bf16(o * o)
            ms = (jnp.dot(sq_hi, head_mean, preferred_element_type=F32)
                  + jnp.dot(sq_lo, head_mean, preferred_element_type=F32))
            gate = gate_ref[0, sl, :]
            o = o * lax.rsqrt(ms + EPS) * gn_ref[...] * (gate * jax.nn.sigmoid(gate))
        o_ref[0, sl, :] = o
        b_end = b[0:1, :] if rev else b[C - 1:C, :]
        kend = (k * jnp.exp(b_end - b)).astype(BF16)
        upd = lax.dot_general(v.astype(BF16), kend, (((0,), (0,)), ((), ())), preferred_element_type=F32)
        s_scr[...] = st * jnp.exp(b_end) + jnp.where(bd, upd, 0.0)

    @pl.when(i == pl.num_programs(1) - 1)
    def _():
        sfin_ref[0] = s_scr[...]


def _recurrence(pack, cols, s0, prev, gnorm, *, rev, dk, dv, tb):
    B, L, _ = pack.shape
    HK, HV = N_HEADS * dk, N_HEADS * dv
    C = RC
    nblk = L // tb
    diff, masks = _recur_tables(C, rev)
    cq, ck, cg, cv, cgate = cols

    def tok(col):
        return (lambda b, i: (b, nblk - 1 - i, col)) if rev else (lambda b, i: (b, i, col))

    in_specs = [
        pl.BlockSpec((1, tb, HK), tok(cq)), pl.BlockSpec((1, tb, HK), tok(ck)),
        pl.BlockSpec((1, tb, HV), tok(cv)), pl.BlockSpec((1, tb, HK), tok(cg)),
        pl.BlockSpec((1, HV, HK), lambda b, i: (b, 0, 0)),
        pl.BlockSpec(diff.shape, lambda b, i: (0, 0)),
        pl.BlockSpec(masks.shape, lambda b, i: (0, 0, 0)),
    ]
    args = [pack, pack, pack, pack, s0, diff, masks]
    if prev is not None:
        in_specs += [pl.BlockSpec((1, tb, HV), tok(0)), pl.BlockSpec((1, tb, HV), tok(cgate)),
                     pl.BlockSpec((1, HV), lambda b, i: (0, 0))]
        args += [prev, pack, gnorm]
    kern = functools.partial(_recur_kernel, rev=rev, C=C, nsub=tb // C, dk=dk, dv=dv, finish=prev is not None)
    return pl.pallas_call(
        kern,
        grid=(B, nblk),
        in_specs=in_specs,
        out_specs=[pl.BlockSpec((1, tb, HV), tok(0)), pl.BlockSpec((1, HV, HK), lambda b, i: (b, 0, 0))],
        out_shape=[jax.ShapeDtypeStruct((B, L, HV), F32), jax.ShapeDtypeStruct((B, HV, HK), F32)],
        scratch_shapes=[pltpu.VMEM((HV, HK), F32), pltpu.VMEM((tb // C, diff.shape[0], HK), F32),
                        pltpu.VMEM((tb // C, C, N_HEADS * C), BF16)],
        compiler_params=pltpu.CompilerParams(
            dimension_semantics=("parallel", "arbitrary"), vmem_limit_bytes=VMEM_LIMIT),
        name="recur_rev" if rev else "recur_fwd",
    )(*args)


def _bidir(pack_c, pack_l, cols_f, cols_b, gnorm, *, dk, dv):
    B, Lc, _ = pack_c.shape
    s0 = jnp.zeros((B, N_HEADS * dv, N_HEADS * dk), F32)
    gn = jnp.tile(gnorm.astype(F32), N_HEADS).reshape(1, N_HEADS * dv)
    kw = dict(dk=dk, dv=dv)
    oc_f, sc_f = _recurrence(pack_c, cols_f, s0, None, None, rev=False, tb=Lc, **kw)
    ol_f, _ = _recurrence(pack_l, cols_f, sc_f, None, None, rev=False, tb=256, **kw)
    oc, sc_b = _recurrence(pack_c, cols_b, s0, oc_f, gn, rev=True, tb=Lc, **kw)
    ol, _ = _recurrence(pack_l, cols_b, sc_b, ol_f, gn, rev=True, tb=256, **kw)
    return oc, ol


def _attn_kernel(lam_ref, qt_ref, k_ref, vt_ref, g_ref, o_ref, qm_scr, s_scr, p_scr, m_scr, a_scr, acc_scr, *,
                 first, tk):
    W = qt_ref.shape[1]
    tq = qt_ref.shape[2]
    Lk = k_ref.shape[1]
    qt = qt_ref[0]
    row_pair = lax.broadcasted_iota(jnp.int32, (W, tq), 0) // DA_DK
    for j in range(N_QK):
        qm_scr[j] = jnp.where(row_pair == j, qt, 0.0).astype(BF16)
    m_scr[...] = jnp.full(m_scr.shape, -jnp.inf, F32)
    acc_scr[...] = jnp.zeros(acc_scr.shape, F32)

    def block(start, size):
        ks = pl.ds(start, size)
        kblk = k_ref[0, ks, :]
        for j in range(N_QK):
            s_scr[j, :size] = jnp.dot(kblk, qm_scr[j], preferred_element_type=F32)
        for j in range(N_QK):
            s = s_scr[j, :size]
            m_old = m_scr[j]
            m_new = jnp.maximum(m_old, jnp.max(s, axis=0, keepdims=True))
            a_scr[j] = jnp.exp2(m_old - m_new)
            p_scr[j, :size] = jnp.exp2(s - m_new).astype(BF16)
            m_scr[j] = m_new
        for j in range(N_QK):
            pv = jnp.dot(vt_ref[0, j // 2, :, ks], p_scr[j, :size], preferred_element_type=F32)
            acc_scr[j] = a_scr[j] * acc_scr[j] + pv

    block(0, first)

    def body(i, carry):
        block(pl.multiple_of(first + i * tk, math.gcd(first, tk)), tk)
        return carry

    lax.fori_loop(0, (Lk - first) // tk, body, 0)
    lam = lam_ref[0]
    outs = []
    for h in range(DA_HEADS):
        a1, a2 = acc_scr[2 * h], acc_scr[2 * h + 1]
        o = a1[:DA_DV] / a1[DA_DV:DA_DV + 1] - lam * (a2[:DA_DV] / a2[DA_DV:DA_DV + 1])
        ms = jnp.mean(o * o, axis=0, keepdims=True)
        outs.append(o * lax.rsqrt(ms + EPS) * g_ref[...])
    o_ref[0] = jnp.concatenate(outs, axis=0).T


def _diff_attention(qt, k, vt, lam, gcol, *, tq, first, tk):
    B, W, Lq = qt.shape
    Lk = k.shape[1]
    assert (Lk - first) % tk == 0
    rows = max(first, tk)
    HV = DA_HEADS * DA_DV
    return pl.pallas_call(
        functools.partial(_attn_kernel, first=first, tk=tk),
        grid=(B, Lq // tq),
        in_specs=[
            pl.BlockSpec(memory_space=pltpu.SMEM),
            pl.BlockSpec((1, W, tq), lambda b, i: (b, 0, i)),
            pl.BlockSpec((1, Lk, W), lambda b, i: (b, 0, 0)),
            pl.BlockSpec((1, DA_HEADS, VROWS, Lk), lambda b, i: (b, 0, 0, 0)),
            pl.BlockSpec((DA_DV, 1), lambda b, i: (0, 0)),
        ],
        out_specs=pl.BlockSpec((1, tq, HV), lambda b, i: (b, i, 0)),
        out_shape=jax.ShapeDtypeStruct((B, Lq, HV), F32),
        scratch_shapes=[
            pltpu.VMEM((N_QK, W, tq), BF16),
            pltpu.VMEM((N_QK, rows, tq), F32),
            pltpu.VMEM((N_QK, rows, tq), BF16),
            pltpu.VMEM((N_QK, 1, tq), F32),
            pltpu.VMEM((N_QK, 1, tq), F32),
            pltpu.VMEM((N_QK, VROWS, tq), F32),
        ],
        compiler_params=pltpu.CompilerParams(
            dimension_semantics=("parallel", "parallel"), vmem_limit_bytes=VMEM_LIMIT),
        name="diff_attn",
    )(lam, qt, k, vt, gcol)


def _hyena_kernel(w_ref, bias_ref, uv_ref, u1_ref, u2_ref, hp_ref, o_ref, acc_scr, *, nb, nbatch):
    c = pl.program_id(0)
    half = nb // 2
    rows = nb * nbatch
    lane = lax.broadcasted_iota(jnp.int32, (rows, HB), 1)
    row = lax.broadcasted_iota(jnp.int32, (rows, HB), 0)

    def short_conv(u_ref, ch):
        u = u_ref[:, 0].reshape(rows, HB)
        prev = pltpu.roll(u, 1, axis=1)
        prev = jnp.where(lane == 0, jnp.where(row >= nbatch, pltpu.roll(prev, nbatch % rows, axis=0), 0.0), prev)
        nxt = pltpu.roll(u, HB - 1, axis=1)
        nxt = jnp.where(lane == HB - 1,
                        jnp.where(row < rows - nbatch, pltpu.roll(nxt, (rows - nbatch) % rows, axis=0), 0.0), nxt)
        return w_ref[0, ch] * prev + w_ref[1, ch] * u + w_ref[2, ch] * nxt

    def long_conv(z, o):
        zb = z.astype(BF16)
        hrow = hp_ref[o, 0]
        acc_scr[...] = jnp.zeros(acc_scr.shape, F32)
        for off in range(-half, half + 1):
            lo = (off + half) * HB
            r = jnp.broadcast_to(hrow[:, lo:lo + 2 * HB], (HB, 2 * HB))
            w = pltpu.roll(r, 1, 1, stride=1, stride_axis=0)[:, HB:].astype(BF16)
            s0, s1 = max(0, -off), nb - max(0, off)
            if s1 <= s0:
                continue
            src = slice(s0 * nbatch, s1 * nbatch)
            dst = slice((s0 + off) * nbatch, (s1 + off) * nbatch)
            acc_scr[dst, :] += jnp.dot(zb[src], w, preferred_element_type=F32)
        return acc_scr[...]

    z = short_conv(uv_ref, c)
    for o, u_ref in enumerate((u1_ref, u2_ref)):
        gate = short_conv(u_ref, (o + 1) * HY_WIDTH + c)
        z = gate * (long_conv(z, o) + bias_ref[o, c] * z)
    o_ref[0] = z.reshape(nb, nbatch, HB)


def _hyena_fused(ut, conv_w, hp, bias):
    nb, _, B, _ = ut.shape
    C = HY_WIDTH
    blk = lambda k: pl.BlockSpec((nb, 1, B, HB), lambda c: (0, c + k * C, 0, 0))
    return pl.pallas_call(
        functools.partial(_hyena_kernel, nb=nb, nbatch=B),
        grid=(C,),
        in_specs=[
            pl.BlockSpec(memory_space=pltpu.SMEM),
            pl.BlockSpec(memory_space=pltpu.SMEM),
            blk(0), blk(1), blk(2),
            pl.BlockSpec((HY_ORDER, 1, 1, hp.shape[-1]), lambda c: (0, c, 0, 0)),
        ],
        out_specs=pl.BlockSpec((1, nb, B, HB), lambda c: (c, 0, 0, 0)),
        out_shape=jax.ShapeDtypeStruct((C, nb, B, HB), F32),
        scratch_shapes=[pltpu.VMEM((nb * B, HB), F32)],
        compiler_params=pltpu.CompilerParams(
            dimension_semantics=("parallel",), vmem_limit_bytes=VMEM_LIMIT),
        name="hyena_branch",
    )(conv_w, bias, ut, ut, ut, hp)


def _rms_norm(x, g):
    xf = x.astype(F32)
    y = xf * lax.rsqrt(jnp.mean(xf * xf, axis=-1, keepdims=True) + EPS)
    return (y * g.astype(F32)).astype(x.dtype)


def _gate_weights(wa2, ba):
    qk = GLA_HEADS * GLA_DK
    w = jnp.zeros((LANES, 2 * qk), F32)
    w = w.at[:GLA_RANK, :qk].set(wa2[0].astype(F32)).at[GLA_RANK:2 * GLA_RANK, qk:].set(wa2[1].astype(F32))
    hi = w.astype(BF16)
    lo = (w - hi.astype(F32)).astype(BF16)
    return jnp.stack([hi, lo]), ba.astype(F32).reshape(1, 2 * qk)


def _hyena_filters(L, p):
    j = jnp.arange(L, dtype=F32)
    t = j / max(L - 1, 1)
    w = 2 * math.pi * j / L
    f = jnp.linspace(1e-4, HY_BANDS - 1, HY_BANDS, dtype=F32)
    feats = jnp.concatenate([t[:, None], jnp.cos(w[:, None] * f), -jnp.sin(w[:, None] * f)], axis=-1)
    h = jnp.sin(p['hy_freq'][0] * (feats @ p['hy_w1'] + p['hy_b1']))
    h = jnp.sin(p['hy_freq'][1] * (h @ p['hy_w2'] + p['hy_b2']))
    h = (h @ p['hy_w3']).astype(F32)
    dist = jnp.abs(j - L // 2) / (L // 2)
    h = h * (jnp.exp(-dist[:, None] * jnp.abs(p['hy_decay'].astype(F32))) + HY_SHIFT)
    h = h / jnp.sum(jnp.abs(h), axis=0, keepdims=True)
    return h.reshape(L, HY_ORDER, HY_WIDTH)


def _from_blocks(a):
    C, nb, B, _ = a.shape
    return a.transpose(2, 1, 3, 0).reshape(B, nb * HB, C)


def _pad_filter(h):
    L = h.shape[0]
    nb = L // HB
    left = (HB - 1) - (L // 2 - (nb // 2) * HB)
    return jnp.pad(h.T, ((0, 0), (left, (nb + 2) * HB - L - left)))[:, None, :]


def _hyena_branch(ut3, p):
    nb, C3, BH = ut3.shape
    h = _hyena_filters(nb * HB, p)
    hp = jnp.stack([_pad_filter(h[:, o]) for o in range(HY_ORDER)])
    ut = ut3.reshape(nb, C3, BH // HB, HB)
    return _from_blocks(_hyena_fused(ut, p['hy_conv_w'], hp, p['hy_bias']))


def _rope_tables(L):
    quarter = DA_DK // 4
    freqs = ROPE_BASE ** (-jnp.arange(quarter, dtype=F32) / quarter)
    row = (jnp.arange(L) // GRID_W).astype(F32)[:, None] * freqs
    col = (jnp.arange(L) % GRID_W).astype(F32)[:, None] * freqs
    return jnp.cos(row), jnp.sin(row), jnp.cos(col), jnp.sin(col)


def _da_prep(t, g, tables):
    B, L, W = t.shape
    t = _rms_norm(t.reshape(B, L, N_QK, DA_DK), g)
    if tables is not None:
        cr, sr, cc, sc = [a[None, :, None, :] for a in tables]
        e = DA_DK // 4
        a1, a2, b1, b2 = t[..., :e], t[..., e:2 * e], t[..., 2 * e:3 * e], t[..., 3 * e:]
        t = jnp.concatenate([a1 * cr - a2 * sr, a1 * sr + a2 * cr, b1 * cc - b2 * sc, b1 * sc + b2 * cc], axis=-1)
    return t.reshape(B, L, W)


def _value_rows(v):
    B, Lk, _ = v.shape
    vt = v.reshape(B, Lk, DA_HEADS, DA_DV).transpose(0, 2, 3, 1)
    extra = jnp.concatenate([jnp.ones((B, DA_HEADS, 1, Lk), v.dtype),
                             jnp.zeros((B, DA_HEADS, VROWS - DA_DV - 1, Lk), v.dtype)], axis=2)
    return jnp.concatenate([vt, extra], axis=2).astype(BF16)


def _da_parts(da):
    w = DA_HEADS * 2 * DA_DK
    return da[..., :w], da[..., w:2 * w], da[..., 2 * w:]


def _group_weights(w_in):
    cols = []
    for names, width in PROJ_GROUPS:
        used = 0
        for nm in names:
            i = IN_NAMES.index(nm)
            cols.append(w_in[:, IN_OFFSETS[i]:IN_OFFSETS[i] + IN_WIDTHS[i]])
            used += IN_WIDTHS[i]
        if width > used:
            cols.append(jnp.zeros((w_in.shape[0], width - used), w_in.dtype))
    return jnp.concatenate(cols, axis=1).astype(BF16)


GLA_COLS_F, GLA_COLS_B = (0, 1, 2, 2, 3), (0, 1, 3, 2, 3)
HG_COLS_F, HG_COLS_B = (0, 1, 3, 5, 6), (0, 2, 4, 5, 6)


def _token_mixer(gc, gx, p, lam_init, last):
    gla_c, gla_x = _bidir(gc[0], gx[0], GLA_COLS_F, GLA_COLS_B, p['gla_norm_g'], dk=GLA_DK, dv=GLA_DV)
    hg_c, hg_x = _bidir(gc[2], gx[2], HG_COLS_F, HG_COLS_B, p['hg_norm_g'], dk=HG_DK, dv=HG_DV)
    (dqc, dkc, dvc), (dqx, dkx, dvx) = _da_parts(gc[3]), _da_parts(gx[3])
    tables = _rope_tables(dqx.shape[1])
    scale = DA_DK ** -0.5 * LOG2E
    qx = (_da_prep(dqx, p['da_qnorm_g'], tables) * scale).transpose(0, 2, 1)
    kx = _da_prep(dkx, p['da_knorm_g'], tables)
    kc = _da_prep(dkc, p['da_knorm_g'], None)
    k_all = jnp.concatenate([kc, kx], axis=1).astype(BF16)
    vt_all = _value_rows(jnp.concatenate([dvc, dvx], axis=1))
    lp = p['da_lam'].astype(F32)
    lam = (jnp.exp(jnp.sum(lp[0] * lp[1])) - jnp.exp(jnp.sum(lp[2] * lp[3])) + lam_init).reshape(1)
    gcol = (p['da_norm_g'].astype(F32) * (1 - lam_init)).reshape(DA_DV, 1)
    Lc = kc.shape[1]
    da_x = _diff_attention(qx, k_all, vt_all, lam, gcol, tq=256, first=Lc, tk=512)
    outs_x = (gla_x, _hyena_branch(gx[1], p), hg_x, da_x)
    if last:
        return None, outs_x
    qc = (_da_prep(dqc, p['da_qnorm_g'], None) * scale).transpose(0, 2, 1)
    da_c = _diff_attention(qc, k_all[:, :Lc], vt_all[..., :Lc], lam, gcol, tq=Lc, first=Lc, tk=Lc)
    return (gla_c, _hyena_branch(gc[1], p), hg_c, da_c), outs_x


def _expert_choice_moe(x, m, h, router, w1, w3, w2, layer):
    B, L, D = h.shape
    cap = CAPACITY_FACTOR * L // N_EXPERTS
    aff = jax.nn.softmax((h @ router).astype(F32), axis=-1)
    g, idx = lax.top_k(aff.transpose(0, 2, 1), cap)
    xg = jax.vmap(lambda hb, ib: hb[ib])(h, idx)
    if cap < FFN_MIN_ROWS:
        regroup = lambda t: t.transpose(1, 0, 2, 3).reshape(1, N_EXPERTS, B * cap, t.shape[-1])
        y = _expert_ffn(regroup(xg), regroup(g[..., None]), w1, w3, w2, layer)
        y = y.reshape(N_EXPERTS, B, cap, D).transpose(1, 0, 2, 3)
    else:
        y = _expert_ffn(xg, g[..., None], w1, w3, w2, layer)
    return _combine(idx.reshape(B, 1, N_EXPERTS * cap), y.reshape(B, N_EXPERTS * cap, D), x, m, tt=min(L, 512))


def kernel(x, c, ctx, c_ctx, ada_w, ada_b, norm1_g, norm2_g, w_in, gla_wa2, gla_ba, gla_norm_g,
           hy_conv_w, hy_w1, hy_b1, hy_w2, hy_b2, hy_w3, hy_freq, hy_decay, hy_bias,
           hg_lower, hg_norm_g, da_qnorm_g, da_knorm_g, da_lam, da_norm_g, w_branch, w_out,
           moe_router, moe_w1, moe_w3, moe_w2):
    B, L, D = x.shape
    P = jax.nn.softmax(hg_lower.astype(F32), axis=0)
    lower = jnp.cumsum(P, axis=0) - P[0]
    sc = jax.nn.silu(c)
    scc = jax.nn.silu(c_ctx)
    xc, xx = ctx, x
    for l in range(DEPTH):
        last = l == DEPTH - 1
        lam_init = 0.8 - 0.6 * math.exp(-0.3 * l)
        p = {'gla_wa2': gla_wa2[l], 'gla_ba': gla_ba[l], 'gla_norm_g': gla_norm_g[l],
             'hy_conv_w': hy_conv_w[l], 'hy_w1': hy_w1[l], 'hy_b1': hy_b1[l], 'hy_w2': hy_w2[l], 'hy_b2': hy_b2[l],
             'hy_w3': hy_w3[l], 'hy_freq': hy_freq[l], 'hy_decay': hy_decay[l], 'hy_bias': hy_bias[l],
             'hg_norm_g': hg_norm_g[l], 'da_qnorm_g': da_qnorm_g[l], 'da_knorm_g': da_knorm_g[l],
             'da_lam': da_lam[l], 'da_norm_g': da_norm_g[l]}
        mod_x = jnp.split((sc @ ada_w[l] + ada_b[l])[:, None, :], ADA_CHUNKS, axis=-1)
        mod_c1 = jnp.split((scc @ ada_w[l] + ada_b[l])[None, None, :], ADA_CHUNKS, axis=-1)
        mod_c = [jnp.broadcast_to(m, (B, 1, D)) for m in mod_c1]
        wg = _group_weights(w_in[l])
        wa, ba = _gate_weights(gla_wa2[l], gla_ba[l])
        lb = lower[l].reshape(1, 2 * HG_HEADS * HG_DK)
        wb = w_branch[l].astype(BF16)
        wo = w_out[l].astype(BF16)
        gx = _in_proj(xx, norm1_g[l], mod_x[0], mod_x[1], wg, wa, ba, lb, tm=256)
        gc = _in_proj(xc, norm1_g[l], mod_c[0], mod_c[1], wg, wa, ba, lb, tm=256)
        outs_c, outs_x = _token_mixer(gc, gx, p, lam_init, last)
        xx, hx = _merge_out(outs_x, gx[4], wb, wo, xx, mod_x[2], norm2_g[l], mod_x[3], mod_x[4], tm=512)
        xx = _expert_choice_moe(xx, mod_x[5], hx, moe_router[l], moe_w1, moe_w3, moe_w2, l)
        if not last:
            xc, hc = _merge_out(outs_c, gc[4], wb, wo, xc, mod_c[2], norm2_g[l], mod_c[3], mod_c[4], tm=256)
            xc = _expert_choice_moe(xc, mod_c[5], hc, moe_router[l], moe_w1, moe_w3, moe_w2, l)
    return xx
```

```python
import functools
import math

import jax
import jax.numpy as jnp
import numpy as np
from jax import lax
from jax.experimental import pallas as pl
from jax.experimental.pallas import tpu as pltpu

D_MODEL = 1024
DEPTH = 2
GRID_W = 64
N_BRANCH = 4
MIX_W = 256
GLA_HEADS = 4
GLA_DK = 32
GLA_DV = 64
GLA_RANK = 16
GLA_TAU = 16.0
HY_WIDTH = 256
HY_ORDER = 2
HY_BANDS = 16
HY_SHIFT = 0.05
HG_HEADS = 4
HG_DK = 64
HG_DV = 64
DA_HEADS = 4
DA_DK = 32
DA_DV = 64
ROPE_BASE = 10000.0
N_EXPERTS = 16
EXPERT_FF = 1024
CAPACITY_FACTOR = 2
ADA_CHUNKS = 6
EPS = 1e-6
F_TINY = 1e-20

IN_NAMES = ('gla_q', 'gla_k', 'gla_v', 'gla_af', 'gla_ab', 'gla_g', 'hy',
            'hg_q', 'hg_ff', 'hg_fb', 'hg_i', 'hg_g', 'da_q', 'da_k', 'da_v', 'merge')
IN_WIDTHS = (GLA_HEADS * GLA_DK, GLA_HEADS * GLA_DK, GLA_HEADS * GLA_DV, GLA_RANK, GLA_RANK, GLA_HEADS * GLA_DV,
             (1 + HY_ORDER) * HY_WIDTH,
             HG_HEADS * HG_DK, HG_HEADS * HG_DK, HG_HEADS * HG_DK, HG_HEADS * HG_DV, HG_HEADS * HG_DV,
             DA_HEADS * 2 * DA_DK, DA_HEADS * 2 * DA_DK, DA_HEADS * DA_DV,
             N_BRANCH * D_MODEL)
IN_OFFSETS = tuple(int(v) for v in np.cumsum((0,) + IN_WIDTHS)[:-1])

PROJ_GROUPS = (
    (('gla_q', 'gla_k', 'gla_v', 'gla_g', 'gla_af', 'gla_ab'), 896),
    (('hy',), 768),
    (('hg_q', 'hg_ff', 'hg_fb', 'hg_i', 'hg_g'), 1280),
    (('da_q', 'da_k', 'da_v'), 768),
    (('merge',), 4096),
)
PROJ_WIDTHS = tuple(w for _, w in PROJ_GROUPS)
LANES = 128
OUT_WIDTHS = (4 * GLA_HEADS * GLA_DK + 2 * GLA_HEADS * GLA_DV, PROJ_WIDTHS[1],
              5 * HG_HEADS * HG_DK + 2 * HG_HEADS * HG_DV, PROJ_WIDTHS[3], PROJ_WIDTHS[4])
OUT_DTYPES = (jnp.float32,) * 4 + (jnp.bfloat16,)
VMEM_LIMIT = 56 * 1024 * 1024

BF16 = jnp.bfloat16
F32 = jnp.float32
N_HEADS = 4
RC = 64
N_QK = 2 * DA_HEADS
VROWS = DA_DV + 8
LOG2E = 1.4426950408889634
FFN_MIN_ROWS = 256
HB = 256


def _split_bf16(a):
    hi = a.astype(BF16)
    return hi, (a - hi.astype(F32)).astype(BF16)


def _in_proj_kernel(x_ref, g_ref, sh_ref, sc_ref, w_ref, wa_ref, ba_ref, lb_ref,
                    gla_ref, hy_ref, hg_ref, da_ref, mg_ref):
    x = x_ref[0]
    ms = jnp.mean(x * x, axis=-1, keepdims=True)
    h = x * lax.rsqrt(ms + EPS) * g_ref[...]
    h = (h * (1.0 + sc_ref[0]) + sh_ref[0]).astype(BF16)
    offs = np.cumsum((0,) + PROJ_WIDTHS)

    def proj(k):
        return jnp.dot(h, w_ref[:, int(offs[k]):int(offs[k + 1])], preferred_element_type=F32)

    r = proj(0)
    qk = GLA_HEADS * GLA_DK
    vw = GLA_HEADS * GLA_DV
    gla_ref[0, :, 0:qk] = r[:, 0:qk] * (GLA_DK ** -0.5)
    gla_ref[0, :, qk:2 * qk] = r[:, qk:2 * qk]
    a_hi, a_lo = _split_bf16(r[:, 2 * qk + 2 * vw:])
    z = (jnp.dot(a_hi, wa_ref[0], preferred_element_type=F32) + jnp.dot(a_lo, wa_ref[0], preferred_element_type=F32)
         + jnp.dot(a_hi, wa_ref[1], preferred_element_type=F32) + ba_ref[...])
    gla_ref[0, :, 2 * qk:4 * qk] = jax.nn.log_sigmoid(z) * (1.0 / GLA_TAU)
    gla_ref[0, :, 4 * qk:4 * qk + 2 * vw] = r[:, 2 * qk:2 * qk + 2 * vw]

    hy_ref[0] = proj(1).T

    r = proj(2)
    hw = HG_HEADS * HG_DK
    q = r[:, 0:hw]
    hg_ref[0, :, 0:hw] = q * jax.nn.sigmoid(q)
    zf = r[:, hw:3 * hw]
    lb = lb_ref[...]
    hg_ref[0, :, hw:3 * hw] = (1.0 - lb) * jax.nn.sigmoid(-zf)
    hg_ref[0, :, 3 * hw:5 * hw] = jnp.log(jnp.maximum(lb + (1.0 - lb) * jax.nn.sigmoid(zf), F_TINY))
    hg_ref[0, :, 5 * hw:7 * hw] = r[:, 3 * hw:5 * hw]

    da_ref[0] = proj(3)
    mg_ref[0] = (0.5 * proj(4)).astype(mg_ref.dtype)


def _in_proj(x, g, shift, scale, w_groups, wa, ba, lb, tm):
    B, L, D = x.shape
    nw = w_groups.shape[1]
    const2 = lambda b, i: (0, 0)
    out_specs = [pl.BlockSpec((1, tm, w), lambda b, i: (b, i, 0)) for w in OUT_WIDTHS]
    out_shape = [jax.ShapeDtypeStruct((B, L, w), dt) for w, dt in zip(OUT_WIDTHS, OUT_DTYPES)]
    assert tm == HB
    out_specs[1] = pl.BlockSpec((1, OUT_WIDTHS[1], tm), lambda b, i: (i, 0, b))
    out_shape[1] = jax.ShapeDtypeStruct((L // tm, OUT_WIDTHS[1], B * tm), F32)
    return pl.pallas_call(
        _in_proj_kernel,
        grid=(B, L // tm),
        in_specs=[
            pl.BlockSpec((1, tm, D), lambda b, i: (b, i, 0)),
            pl.BlockSpec((1, D), const2),
            pl.BlockSpec((1, 1, D), lambda b, i: (b, 0, 0)),
            pl.BlockSpec((1, 1, D), lambda b, i: (b, 0, 0)),
            pl.BlockSpec((D, nw), const2, pipeline_mode=pl.Buffered(1)),
            pl.BlockSpec(wa.shape, lambda b, i: (0, 0, 0)),
            pl.BlockSpec(ba.shape, const2),
            pl.BlockSpec(lb.shape, const2),
        ],
        out_specs=out_specs,
        out_shape=out_shape,
        compiler_params=pltpu.CompilerParams(
            dimension_semantics=("parallel", "parallel"), vmem_limit_bytes=VMEM_LIMIT),
        name="in_proj",
    )(x, g.reshape(1, D), shift, scale, w_groups, wa, ba, lb)


def _merge_kernel(o0_ref, o1_ref, o2_ref, o3_ref, gate_ref, wb_ref, wo_ref, x_ref, m_ref, g_ref, sh_ref, sc_ref,
                  out_ref, h_ref, hb_ref):
    D = D_MODEL
    acc = None
    for i, o_ref in enumerate((o0_ref, o1_ref, o2_ref, o3_ref)):
        t = jnp.dot(o_ref[0].astype(BF16), wb_ref[i], preferred_element_type=F32)
        t = (jnp.tanh(gate_ref[0, :, i * D:(i + 1) * D].astype(F32)) + 1.0) * t
        acc = t if acc is None else acc + t
    mx = jnp.dot((0.5 * acc).astype(BF16), wo_ref[...], preferred_element_type=F32)
    out = x_ref[0] + m_ref[0] * mx
    out_ref[0] = out
    ms = jnp.mean(out * out, axis=-1, keepdims=True)
    h = out * lax.rsqrt(ms + EPS) * g_ref[...] * (1.0 + sc_ref[0]) + sh_ref[0]
    h_ref[0] = h
    hb_ref[0] = h.astype(BF16)


def _merge_out(outs, gate_cols, w_branch, w_out, x, m, g, shift, scale, tm):
    B, L, D = x.shape
    tok = pl.BlockSpec((1, tm, D), lambda b, i: (b, i, 0))
    per_sample = pl.BlockSpec((1, 1, D), lambda b, i: (b, 0, 0))
    return pl.pallas_call(
        _merge_kernel,
        grid=(B, L // tm),
        in_specs=[pl.BlockSpec((1, tm, MIX_W), lambda b, i: (b, i, 0)) for _ in range(N_BRANCH)] + [
            pl.BlockSpec((1, tm, N_BRANCH * D), lambda b, i: (b, i, 0)),
            pl.BlockSpec((N_BRANCH, MIX_W, D), lambda b, i: (0, 0, 0)),
            pl.BlockSpec((D, D), lambda b, i: (0, 0)),
            tok, per_sample,
            pl.BlockSpec((1, D), lambda b, i: (0, 0)), per_sample, per_sample,
        ],
        out_specs=[tok, tok, tok],
        out_shape=[jax.ShapeDtypeStruct((B, L, D), F32), jax.ShapeDtypeStruct((B, L, D), F32),
                   jax.ShapeDtypeStruct((B, L, D), BF16)],
        compiler_params=pltpu.CompilerParams(
            dimension_semantics=("parallel", "parallel"), vmem_limit_bytes=VMEM_LIMIT),
        name="merge_out",
    )(*outs, gate_cols, w_branch, w_out, x, m, g.reshape(1, D), shift, scale)


def _expert_ffn_kernel(x_ref, g_ref, w1_ref, w3_ref, w2_ref, out_ref, w1_scr, w3_scr, w2_scr):
    @pl.when(pl.program_id(1) == 0)
    def _():
        w1_scr[...] = w1_ref[0].astype(BF16)
        w3_scr[...] = w3_ref[0].astype(BF16)
        w2_scr[...] = w2_ref[0].astype(BF16)

    x = x_ref[0, 0].astype(BF16)
    a = jnp.dot(x, w1_scr[...], preferred_element_type=F32)
    b = jnp.dot(x, w3_scr[...], preferred_element_type=F32)
    h = (a * jax.nn.sigmoid(a) * b).astype(BF16)
    y = jnp.dot(h, w2_scr[...], preferred_element_type=F32)
    out_ref[0, 0] = (y * g_ref[0, 0]).astype(out_ref.dtype)


def _combine_kernel(idx_ref, y_ref, x_ref, m_ref, out_ref, *, ks):
    tt = out_ref.shape[1]
    S = y_ref.shape[1]
    base = pl.program_id(1) * tt
    acc = None
    for s0 in range(0, S, ks):
        tok = lax.broadcasted_iota(jnp.int32, (tt, ks), 0) + base
        onehot = jnp.where(tok == idx_ref[0, :, s0:s0 + ks], 1.0, 0.0).astype(BF16)
        part = jnp.dot(onehot, y_ref[0, s0:s0 + ks, :], preferred_element_type=F32)
        acc = part if acc is None else acc + part
    out_ref[0] = x_ref[0] + m_ref[0] * acc


def _combine(idx, y, x, m, tt):
    B, L, D = x.shape
    S = y.shape[1]
    return pl.pallas_call(
        functools.partial(_combine_kernel, ks=min(S, 2048)),
        grid=(B, L // tt),
        in_specs=[
            pl.BlockSpec((1, 1, S), lambda b, i: (b, 0, 0)),
            pl.BlockSpec((1, S, D), lambda b, i: (b, 0, 0)),
            pl.BlockSpec((1, tt, D), lambda b, i: (b, i, 0)),
            pl.BlockSpec((1, 1, D), lambda b, i: (b, 0, 0)),
        ],
        out_specs=pl.BlockSpec((1, tt, D), lambda b, i: (b, i, 0)),
        out_shape=jax.ShapeDtypeStruct((B, L, D), F32),
        compiler_params=pltpu.CompilerParams(
            dimension_semantics=("parallel", "parallel"), vmem_limit_bytes=VMEM_LIMIT),
        name="moe_combine",
    )(idx, y, x, m)


def _expert_ffn(xg, g, w1, w3, w2, layer):
    B, E, cap, D = xg.shape
    F = w1.shape[-1]
    return pl.pallas_call(
        _expert_ffn_kernel,
        grid=(E, B),
        in_specs=[
            pl.BlockSpec((1, 1, cap, D), lambda e, b: (b, e, 0, 0)),
            pl.BlockSpec((1, 1, cap, 1), lambda e, b: (b, e, 0, 0)),
            pl.BlockSpec((None, 1, D, F), lambda e, b: (layer, e, 0, 0)),
            pl.BlockSpec((None, 1, D, F), lambda e, b: (layer, e, 0, 0)),
            pl.BlockSpec((None, 1, F, D), lambda e, b: (layer, e, 0, 0)),
        ],
        out_specs=pl.BlockSpec((1, 1, cap, D), lambda e, b: (b, e, 0, 0)),
        out_shape=jax.ShapeDtypeStruct((B, E, cap, D), BF16),
        scratch_shapes=[pltpu.VMEM((D, F), BF16), pltpu.VMEM((D, F), BF16), pltpu.VMEM((F, D), BF16)],
        compiler_params=pltpu.CompilerParams(
            dimension_semantics=("parallel", "arbitrary"), vmem_limit_bytes=VMEM_LIMIT),
        name="expert_ffn",
    )(xg, g, w1, w3, w2)


def _recur_tables(C, rev):
    pos = (C - 1 - np.arange(C)) if rev else np.arange(C)
    row_of = np.argsort(pos)
    levels = int(np.log2(C))
    diff = np.zeros((levels + 1, C, C), np.float32)
    mask = np.zeros((levels + 1, C, C), np.float32)
    for t in range(C):
        p = pos[t]
        diff[0, t, row_of[:p + 1]] = 1.0
        mask[0, t, t] = 1.0
        for l in range(1, levels + 1):
            n, m = 2 ** l, 2 ** (l - 1)
            off = p % n
            mid = p - off + m - 1
            if off >= m:
                diff[l, t, row_of[mid + 1:p + 1]] = 1.0
                mask[l, t, row_of[p - off:mid + 1]] = 1.0
            else:
                diff[l, t, row_of[p + 1:mid + 1]] = 1.0
    return (jnp.asarray(np.tile(diff.reshape((levels + 1) * C, C), (1, 3)), BF16),
            jnp.asarray(np.tile(mask, (1, 1, N_HEADS))))


def _recur_kernel(*refs, rev, C, nsub, dk, dv, finish):
    if finish:
        (q_ref, k_ref, v_ref, g_ref, s0_ref, d_ref, m_ref, prev_ref, gate_ref, gn_ref,
         o_ref, sfin_ref, s_scr, e_scr, a_scr) = refs
    else:
        q_ref, k_ref, v_ref, g_ref, s0_ref, d_ref, m_ref, o_ref, sfin_ref, s_scr, e_scr, a_scr = refs
    HK = N_HEADS * dk
    HV = N_HEADS * dv
    i = pl.program_id(1)
    if finish:
        head_mean = jnp.where(lax.broadcasted_iota(jnp.int32, (HV, HV), 0) // dv
                              == lax.broadcasted_iota(jnp.int32, (HV, HV), 1) // dv, 1.0 / dv, 0.0).astype(BF16)

    @pl.when(i == 0)
    def _():
        s_scr[...] = s0_ref[0]

    head_k = [(lax.broadcasted_iota(jnp.int32, (1, HK), 1) // dk == h).astype(BF16) for h in range(N_HEADS)]
    head_v = [(lax.broadcasted_iota(jnp.int32, (1, HV), 1) // dv == h).astype(BF16) for h in range(N_HEADS)]
    bd = (lax.broadcasted_iota(jnp.int32, (HV, HK), 0) // dv
          == lax.broadcasted_iota(jnp.int32, (HV, HK), 1) // dk)
    levels = m_ref.shape[0] - 1

    def stack_heads(x, head):
        xb = x.astype(BF16)
        return jnp.concatenate([xb * head[h] for h in range(N_HEADS)], axis=0)

    def nt(a, b):
        return lax.dot_general(a, b, (((1,), (1,)), ((), ())), preferred_element_type=F32)

    order = [(nsub - 1 - j) if rev else j for j in range(nsub)]
    for c in order:
        g = g_ref[0, c * C:(c + 1) * C, :]
        g_hi = g.astype(BF16)
        r1 = g - g_hi.astype(F32)
        g_mid = r1.astype(BF16)
        g_lo = (r1 - g_mid.astype(F32)).astype(BF16)
        e_scr[c] = jnp.dot(d_ref[...], jnp.concatenate([g_hi, g_mid, g_lo], axis=0), preferred_element_type=F32)
    for c in order:
        q = q_ref[0, c * C:(c + 1) * C, :]
        k = k_ref[0, c * C:(c + 1) * C, :]
        a = nt(q.astype(BF16), stack_heads(k, head_k)) * m_ref[0]
        for lvl in range(1, levels + 1):
            x = jnp.exp(e_scr[c, lvl * C:(lvl + 1) * C, :])
            a = a + nt((q * x).astype(BF16), stack_heads(k * x, head_k)) * m_ref[lvl]
        a_scr[c] = a.astype(BF16)
    for c in order:
        sl = slice(c * C, (c + 1) * C)
        q = q_ref[0, sl, :]
        k = k_ref[0, sl, :]
        v = v_ref[0, sl, :]
        b = e_scr[c, 0:C, :]
        st = s_scr[...]
        o = jnp.dot(a_scr[c], stack_heads(v, head_v), preferred_element_type=F32)
        o = o + nt((q * jnp.exp(b)).astype(BF16), st.astype(BF16))
        if finish:
            o = o + prev_ref[0, sl, :]
            sq_hi, sq_lo = _split_bf16(o * o)
            ms = (jnp.dot(sq_hi, head_mean, preferred_element_type=F32)
                  + jnp.dot(sq_lo, head_mean, preferred_element_type=F32))
            gate = gate_ref[0, sl, :]
            o = o * lax.rsqrt(ms + EPS) * gn_ref[...] * (gate * jax.nn.sigmoid(gate))
        o_ref[0, sl, :] = o
        b_end = b[0:1, :] if rev else b[C - 1:C, :]
        kend = (k * jnp.exp(b_end - b)).astype(BF16)
        upd = lax.dot_general(v.astype(BF16), kend, (((0,), (0,)), ((), ())), preferred_element_type=F32)
        s_scr[...] = st * jnp.exp(b_end) + jnp.where(bd, upd, 0.0)

    @pl.when(i == pl.num_programs(1) - 1)
    def _():
        sfin_ref[0] = s_scr[...]


def _recurrence(pack, cols, s0, prev, gnorm, *, rev, dk, dv, tb):
    B, L, _ = pack.shape
    HK, HV = N_HEADS * dk, N_HEADS * dv
    C = RC
    nblk = L // tb
    diff, masks = _recur_tables(C, rev)
    cq, ck, cg, cv, cgate = cols

    def tok(col):
        return (lambda b, i: (b, nblk - 1 - i, col)) if rev else (lambda b, i: (b, i, col))

    in_specs = [
        pl.BlockSpec((1, tb, HK), tok(cq)), pl.BlockSpec((1, tb, HK), tok(ck)),
        pl.BlockSpec((1, tb, HV), tok(cv)), pl.BlockSpec((1, tb, HK), tok(cg)),
        pl.BlockSpec((1, HV, HK), lambda b, i: (b, 0, 0)),
        pl.BlockSpec(diff.shape, lambda b, i: (0, 0)),
        pl.BlockSpec(masks.shape, lambda b, i: (0, 0, 0)),
    ]
    args = [pack, pack, pack, pack, s0, diff, masks]
    if prev is not None:
        in_specs += [pl.BlockSpec((1, tb, HV), tok(0)), pl.BlockSpec((1, tb, HV), tok(cgate)),
                     pl.BlockSpec((1, HV), lambda b, i: (0, 0))]
        args += [prev, pack, gnorm]
    kern = functools.partial(_recur_kernel, rev=rev, C=C, nsub=tb // C, dk=dk, dv=dv, finish=prev is not None)
    return pl.pallas_call(
        kern,
        grid=(B, nblk),
        in_specs=in_specs,
        out_specs=[pl.BlockSpec((1, tb, HV), tok(0)), pl.BlockSpec((1, HV, HK), lambda b, i: (b, 0, 0))],
        out_shape=[jax.ShapeDtypeStruct((B, L, HV), F32), jax.ShapeDtypeStruct((B, HV, HK), F32)],
        scratch_shapes=[pltpu.VMEM((HV, HK), F32), pltpu.VMEM((tb // C, diff.shape[0], HK), F32),
                        pltpu.VMEM((tb // C, C, N_HEADS * C), BF16)],
        compiler_params=pltpu.CompilerParams(
            dimension_semantics=("parallel", "arbitrary"), vmem_limit_bytes=VMEM_LIMIT),
        name="recur_rev" if rev else "recur_fwd",
    )(*args)


def _bidir(pack_c, pack_l, cols_f, cols_b, gnorm, *, dk, dv):
    B, Lc, _ = pack_c.shape
    s0 = jnp.zeros((B, N_HEADS * dv, N_HEADS * dk), F32)
    gn = jnp.tile(gnorm.astype(F32), N_HEADS).reshape(1, N_HEADS * dv)
    kw = dict(dk=dk, dv=dv)
    oc_f, sc_f = _recurrence(pack_c, cols_f, s0, None, None, rev=False, tb=Lc, **kw)
    ol_f, _ = _recurrence(pack_l, cols_f, sc_f, None, None, rev=False, tb=512, **kw)
    oc, sc_b = _recurrence(pack_c, cols_b, s0, oc_f, gn, rev=True, tb=Lc, **kw)
    ol, _ = _recurrence(pack_l, cols_b, sc_b, ol_f, gn, rev=True, tb=512, **kw)
    return oc, ol


def _attn_kernel(lam_ref, qt_ref, k_ref, vt_ref, g_ref, o_ref, qm_scr, s_scr, p_scr, m_scr, a_scr, acc_scr, *,
                 first, tk):
    W = qt_ref.shape[1]
    tq = qt_ref.shape[2]
    Lk = k_ref.shape[1]
    qt = qt_ref[0]
    row_pair = lax.broadcasted_iota(jnp.int32, (W, tq), 0) // DA_DK
    for j in range(N_QK):
        qm_scr[j] = jnp.where(row_pair == j, qt, 0.0).astype(BF16)
    m_scr[...] = jnp.full(m_scr.shape, -jnp.inf, F32)
    acc_scr[...] = jnp.zeros(acc_scr.shape, F32)

    def block(start, size):
        ks = pl.ds(start, size)
        kblk = k_ref[0, ks, :]
        for j in range(N_QK):
            s_scr[j, :size] = jnp.dot(kblk, qm_scr[j], preferred_element_type=F32)
        for j in range(N_QK):
            s = s_scr[j, :size]
            m_old = m_scr[j]
            m_new = jnp.maximum(m_old, jnp.max(s, axis=0, keepdims=True))
            a_scr[j] = jnp.exp2(m_old - m_new)
            p_scr[j, :size] = jnp.exp2(s - m_new).astype(BF16)
            m_scr[j] = m_new
        for j in range(N_QK):
            pv = jnp.dot(vt_ref[0, j // 2, :, ks], p_scr[j, :size], preferred_element_type=F32)
            acc_scr[j] = a_scr[j] * acc_scr[j] + pv

    block(0, first)

    def body(i, carry):
        block(pl.multiple_of(first + i * tk, math.gcd(first, tk)), tk)
        return carry

    lax.fori_loop(0, (Lk - first) // tk, body, 0)
    lam = lam_ref[0]
    outs = []
    for h in range(DA_HEADS):
        a1, a2 = acc_scr[2 * h], acc_scr[2 * h + 1]
        o = a1[:DA_DV] / a1[DA_DV:DA_DV + 1] - lam * (a2[:DA_DV] / a2[DA_DV:DA_DV + 1])
        ms = jnp.mean(o * o, axis=0, keepdims=True)
        outs.append(o * lax.rsqrt(ms + EPS) * g_ref[...])
    o_ref[0] = jnp.concatenate(outs, axis=0).T


def _diff_attention(qt, k, vt, lam, gcol, *, tq, first, tk):
    B, W, Lq = qt.shape
    Lk = k.shape[1]
    assert (Lk - first) % tk == 0
    rows = max(first, tk)
    HV = DA_HEADS * DA_DV
    return pl.pallas_call(
        functools.partial(_attn_kernel, first=first, tk=tk),
        grid=(B, Lq // tq),
        in_specs=[
            pl.BlockSpec(memory_space=pltpu.SMEM),
            pl.BlockSpec((1, W, tq), lambda b, i: (b, 0, i)),
            pl.BlockSpec((1, Lk, W), lambda b, i: (b, 0, 0)),
            pl.BlockSpec((1, DA_HEADS, VROWS, Lk), lambda b, i: (b, 0, 0, 0)),
            pl.BlockSpec((DA_DV, 1), lambda b, i: (0, 0)),
        ],
        out_specs=pl.BlockSpec((1, tq, HV), lambda b, i: (b, i, 0)),
        out_shape=jax.ShapeDtypeStruct((B, Lq, HV), F32),
        scratch_shapes=[
            pltpu.VMEM((N_QK, W, tq), BF16),
            pltpu.VMEM((N_QK, rows, tq), F32),
            pltpu.VMEM((N_QK, rows, tq), BF16),
            pltpu.VMEM((N_QK, 1, tq), F32),
            pltpu.VMEM((N_QK, 1, tq), F32),
            pltpu.VMEM((N_QK, VROWS, tq), F32),
        ],
        compiler_params=pltpu.CompilerParams(
            dimension_semantics=("parallel", "parallel"), vmem_limit_bytes=VMEM_LIMIT),
        name="diff_attn",
    )(lam, qt, k, vt, gcol)


def _hyena_kernel(w_ref, bias_ref, uv_ref, u1_ref, u2_ref, hp_ref, o_ref, acc_scr, *, nb, nbatch):
    c = pl.program_id(0)
    half = nb // 2
    rows = nb * nbatch
    lane = lax.broadcasted_iota(jnp.int32, (rows, HB), 1)
    row = lax.broadcasted_iota(jnp.int32, (rows, HB), 0)

    def short_conv(u_ref, ch):
        u = u_ref[:, 0].reshape(rows, HB)
        prev = pltpu.roll(u, 1, axis=1)
        prev = jnp.where(lane == 0, jnp.where(row >= nbatch, pltpu.roll(prev, nbatch % rows, axis=0), 0.0), prev)
        nxt = pltpu.roll(u, HB - 1, axis=1)
        nxt = jnp.where(lane == HB - 1,
                        jnp.where(row < rows - nbatch, pltpu.roll(nxt, (rows - nbatch) % rows, axis=0), 0.0), nxt)
        return w_ref[0, ch] * prev + w_ref[1, ch] * u + w_ref[2, ch] * nxt

    def long_conv(z, o):
        zb = z.astype(BF16)
        hrow = hp_ref[o, 0]
        acc_scr[...] = jnp.zeros(acc_scr.shape, F32)
        for off in range(-half, half + 1):
            lo = (off + half) * HB
            r = jnp.broadcast_to(hrow[:, lo:lo + 2 * HB], (HB, 2 * HB))
            w = pltpu.roll(r, 1, 1, stride=1, stride_axis=0)[:, HB:].astype(BF16)
            s0, s1 = max(0, -off), nb - max(0, off)
            if s1 <= s0:
                continue
            src = slice(s0 * nbatch, s1 * nbatch)
            dst = slice((s0 + off) * nbatch, (s1 + off) * nbatch)
            acc_scr[dst, :] += jnp.dot(zb[src], w, preferred_element_type=F32)
        return acc_scr[...]

    z = short_conv(uv_ref, c)
    for o, u_ref in enumerate((u1_ref, u2_ref)):
        gate = short_conv(u_ref, (o + 1) * HY_WIDTH + c)
        z = gate * (long_conv(z, o) + bias_ref[o, c] * z)
    o_ref[0] = z.reshape(nb, nbatch, HB)


def _hyena_fused(ut, conv_w, hp, bias):
    nb, _, B, _ = ut.shape
    C = HY_WIDTH
    blk = lambda k: pl.BlockSpec((nb, 1, B, HB), lambda c: (0, c + k * C, 0, 0))
    return pl.pallas_call(
        functools.partial(_hyena_kernel, nb=nb, nbatch=B),
        grid=(C,),
        in_specs=[
            pl.BlockSpec(memory_space=pltpu.SMEM),
            pl.BlockSpec(memory_space=pltpu.SMEM),
            blk(0), blk(1), blk(2),
            pl.BlockSpec((HY_ORDER, 1, 1, hp.shape[-1]), lambda c: (0, c, 0, 0)),
        ],
        out_specs=pl.BlockSpec((1, nb, B, HB), lambda c: (c, 0, 0, 0)),
        out_shape=jax.ShapeDtypeStruct((C, nb, B, HB), F32),
        scratch_shapes=[pltpu.VMEM((nb * B, HB), F32)],
        compiler_params=pltpu.CompilerParams(
            dimension_semantics=("parallel",), vmem_limit_bytes=VMEM_LIMIT),
        name="hyena_branch",
    )(conv_w, bias, ut, ut, ut, hp)


def _rms_norm(x, g):
    xf = x.astype(F32)
    y = xf * lax.rsqrt(jnp.mean(xf * xf, axis=-1, keepdims=True) + EPS)
    return (y * g.astype(F32)).astype(x.dtype)


def _gate_weights(wa2, ba):
    qk = GLA_HEADS * GLA_DK
    w = jnp.zeros((LANES, 2 * qk), F32)
    w = w.at[:GLA_RANK, :qk].set(wa2[0].astype(F32)).at[GLA_RANK:2 * GLA_RANK, qk:].set(wa2[1].astype(F32))
    hi = w.astype(BF16)
    lo = (w - hi.astype(F32)).astype(BF16)
    return jnp.stack([hi, lo]), ba.astype(F32).reshape(1, 2 * qk)


def _hyena_filters(L, p):
    j = jnp.arange(L, dtype=F32)
    t = j / max(L - 1, 1)
    w = 2 * math.pi * j / L
    f = jnp.linspace(1e-4, HY_BANDS - 1, HY_BANDS, dtype=F32)
    feats = jnp.concatenate([t[:, None], jnp.cos(w[:, None] * f), -jnp.sin(w[:, None] * f)], axis=-1)
    h = jnp.sin(p['hy_freq'][0] * (feats @ p['hy_w1'] + p['hy_b1']))
    h = jnp.sin(p['hy_freq'][1] * (h @ p['hy_w2'] + p['hy_b2']))
    h = (h @ p['hy_w3']).astype(F32)
    dist = jnp.abs(j - L // 2) / (L // 2)
    h = h * (jnp.exp(-dist[:, None] * jnp.abs(p['hy_decay'].astype(F32))) + HY_SHIFT)
    h = h / jnp.sum(jnp.abs(h), axis=0, keepdims=True)
    return h.reshape(L, HY_ORDER, HY_WIDTH)


def _from_blocks(a):
    C, nb, B, _ = a.shape
    return a.transpose(2, 1, 3, 0).reshape(B, nb * HB, C)


def _pad_filter(h):
    L = h.shape[0]
    nb = L // HB
    left = (HB - 1) - (L // 2 - (nb // 2) * HB)
    return jnp.pad(h.T, ((0, 0), (left, (nb + 2) * HB - L - left)))[:, None, :]


def _hyena_branch(ut3, p):
    nb, C3, BH = ut3.shape
    h = _hyena_filters(nb * HB, p)
    hp = jnp.stack([_pad_filter(h[:, o]) for o in range(HY_ORDER)])
    ut = ut3.reshape(nb, C3, BH // HB, HB)
    return _from_blocks(_hyena_fused(ut, p['hy_conv_w'], hp, p['hy_bias']))


def _rope_tables(L):
    quarter = DA_DK // 4
    freqs = ROPE_BASE ** (-jnp.arange(quarter, dtype=F32) / quarter)
    row = (jnp.arange(L) // GRID_W).astype(F32)[:, None] * freqs
    col = (jnp.arange(L) % GRID_W).astype(F32)[:, None] * freqs
    return jnp.cos(row), jnp.sin(row), jnp.cos(col), jnp.sin(col)


def _da_prep(t, g, tables):
    B, L, W = t.shape
    t = _rms_norm(t.reshape(B, L, N_QK, DA_DK), g)
    if tables is not None:
        cr, sr, cc, sc = [a[None, :, None, :] for a in tables]
        e = DA_DK // 4
        a1, a2, b1, b2 = t[..., :e], t[..., e:2 * e], t[..., 2 * e:3 * e], t[..., 3 * e:]
        t = jnp.concatenate([a1 * cr - a2 * sr, a1 * sr + a2 * cr, b1 * cc - b2 * sc, b1 * sc + b2 * cc], axis=-1)
    return t.reshape(B, L, W)


def _value_rows(v):
    B, Lk, _ = v.shape
    vt = v.reshape(B, Lk, DA_HEADS, DA_DV).transpose(0, 2, 3, 1)
    extra = jnp.concatenate([jnp.ones((B, DA_HEADS, 1, Lk), v.dtype),
                             jnp.zeros((B, DA_HEADS, VROWS - DA_DV - 1, Lk), v.dtype)], axis=2)
    return jnp.concatenate([vt, extra], axis=2).astype(BF16)


def _da_parts(da):
    w = DA_HEADS * 2 * DA_DK
    return da[..., :w], da[..., w:2 * w], da[..., 2 * w:]


def _group_weights(w_in):
    cols = []
    for names, width in PROJ_GROUPS:
        used = 0
        for nm in names:
            i = IN_NAMES.index(nm)
            cols.append(w_in[:, IN_OFFSETS[i]:IN_OFFSETS[i] + IN_WIDTHS[i]])
            used += IN_WIDTHS[i]
        if width > used:
            cols.append(jnp.zeros((w_in.shape[0], width - used), w_in.dtype))
    return jnp.concatenate(cols, axis=1).astype(BF16)


GLA_COLS_F, GLA_COLS_B = (0, 1, 2, 2, 3), (0, 1, 3, 2, 3)
HG_COLS_F, HG_COLS_B = (0, 1, 3, 5, 6), (0, 2, 4, 5, 6)


def _token_mixer(gc, gx, p, lam_init, last):
    gla_c, gla_x = _bidir(gc[0], gx[0], GLA_COLS_F, GLA_COLS_B, p['gla_norm_g'], dk=GLA_DK, dv=GLA_DV)
    hg_c, hg_x = _bidir(gc[2], gx[2], HG_COLS_F, HG_COLS_B, p['hg_norm_g'], dk=HG_DK, dv=HG_DV)
    (dqc, dkc, dvc), (dqx, dkx, dvx) = _da_parts(gc[3]), _da_parts(gx[3])
    tables = _rope_tables(dqx.shape[1])
    scale = DA_DK ** -0.5 * LOG2E
    qx = (_da_prep(dqx, p['da_qnorm_g'], tables) * scale).transpose(0, 2, 1)
    kx = _da_prep(dkx, p['da_knorm_g'], tables)
    kc = _da_prep(dkc, p['da_knorm_g'], None)
    k_all = jnp.concatenate([kc, kx], axis=1).astype(BF16)
    vt_all = _value_rows(jnp.concatenate([dvc, dvx], axis=1))
    lp = p['da_lam'].astype(F32)
    lam = (jnp.exp(jnp.sum(lp[0] * lp[1])) - jnp.exp(jnp.sum(lp[2] * lp[3])) + lam_init).reshape(1)
    gcol = (p['da_norm_g'].astype(F32) * (1 - lam_init)).reshape(DA_DV, 1)
    Lc = kc.shape[1]
    da_x = _diff_attention(qx, k_all, vt_all, lam, gcol, tq=256, first=Lc, tk=512)
    outs_x = (gla_x, _hyena_branch(gx[1], p), hg_x, da_x)
    if last:
        return None, outs_x
    qc = (_da_prep(dqc, p['da_qnorm_g'], None) * scale).transpose(0, 2, 1)
    da_c = _diff_attention(qc, k_all[:, :Lc], vt_all[..., :Lc], lam, gcol, tq=Lc, first=Lc, tk=Lc)
    return (gla_c, _hyena_branch(gc[1], p), hg_c, da_c), outs_x


def _expert_choice_moe(x, m, h, hb, router, w1, w3, w2, layer):
    B, L, D = h.shape
    cap = CAPACITY_FACTOR * L // N_EXPERTS
    aff = jax.nn.softmax((h @ router).astype(F32), axis=-1)
    g, idx = lax.top_k(aff.transpose(0, 2, 1), cap)
    xg = jax.vmap(lambda rows, ib: rows[ib])(hb, idx)
    if cap < FFN_MIN_ROWS:
        regroup = lambda t: t.transpose(1, 0, 2, 3).reshape(1, N_EXPERTS, B * cap, t.shape[-1])
        y = _expert_ffn(regroup(xg), regroup(g[..., None]), w1, w3, w2, layer)
        y = y.reshape(N_EXPERTS, B, cap, D).transpose(1, 0, 2, 3)
    else:
        y = _expert_ffn(xg, g[..., None], w1, w3, w2, layer)
    return _combine(idx.reshape(B, 1, N_EXPERTS * cap), y.reshape(B, N_EXPERTS * cap, D), x, m, tt=min(L, 512))


def kernel(x, c, ctx, c_ctx, ada_w, ada_b, norm1_g, norm2_g, w_in, gla_wa2, gla_ba, gla_norm_g,
           hy_conv_w, hy_w1, hy_b1, hy_w2, hy_b2, hy_w3, hy_freq, hy_decay, hy_bias,
           hg_lower, hg_norm_g, da_qnorm_g, da_knorm_g, da_lam, da_norm_g, w_branch, w_out,
           moe_router, moe_w1, moe_w3, moe_w2):
    B, L, D = x.shape
    P = jax.nn.softmax(hg_lower.astype(F32), axis=0)
    lower = jnp.cumsum(P, axis=0) - P[0]
    sc = jax.nn.silu(c)
    scc = jax.nn.silu(c_ctx)
    xc, xx = ctx, x
    for l in range(DEPTH):
        last = l == DEPTH - 1
        lam_init = 0.8 - 0.6 * math.exp(-0.3 * l)
        p = {'gla_wa2': gla_wa2[l], 'gla_ba': gla_ba[l], 'gla_norm_g': gla_norm_g[l],
             'hy_conv_w': hy_conv_w[l], 'hy_w1': hy_w1[l], 'hy_b1': hy_b1[l], 'hy_w2': hy_w2[l], 'hy_b2': hy_b2[l],
             'hy_w3': hy_w3[l], 'hy_freq': hy_freq[l], 'hy_decay': hy_decay[l], 'hy_bias': hy_bias[l],
             'hg_norm_g': hg_norm_g[l], 'da_qnorm_g': da_qnorm_g[l], 'da_knorm_g': da_knorm_g[l],
             'da_lam': da_lam[l], 'da_norm_g': da_norm_g[l]}
        mod_x = jnp.split((sc @ ada_w[l] + ada_b[l])[:, None, :], ADA_CHUNKS, axis=-1)
        mod_c1 = jnp.split((scc @ ada_w[l] + ada_b[l])[None, None, :], ADA_CHUNKS, axis=-1)
        mod_c = [jnp.broadcast_to(m, (B, 1, D)) for m in mod_c1]
        wg = _group_weights(w_in[l])
        wa, ba = _gate_weights(gla_wa2[l], gla_ba[l])
        lb = lower[l].reshape(1, 2 * HG_HEADS * HG_DK)
        wb = w_branch[l].astype(BF16)
        wo = w_out[l].astype(BF16)
        gx = _in_proj(xx, norm1_g[l], mod_x[0], mod_x[1], wg, wa, ba, lb, tm=256)
        gc = _in_proj(xc, norm1_g[l], mod_c[0], mod_c[1], wg, wa, ba, lb, tm=256)
        outs_c, outs_x = _token_mixer(gc, gx, p, lam_init, last)
        xx, hx, hxb = _merge_out(outs_x, gx[4], wb, wo, xx, mod_x[2], norm2_g[l], mod_x[3], mod_x[4], tm=512)
        xx = _expert_choice_moe(xx, mod_x[5], hx, hxb, moe_router[l], moe_w1, moe_w3, moe_w2, l)
        if not last:
            xc, hc, hcb = _merge_out(outs_c, gc[4], wb, wo, xc, mod_c[2], norm2_g[l], mod_c[3], mod_c[4], tm=256)
            xc = _expert_choice_moe(xc, mod_c[5], hc, hcb, moe_router[l], moe_w1, moe_w3, moe_w2, l)
    return xx
```

```python
import functools
import math

import jax
import jax.numpy as jnp
import numpy as np
from jax import lax
from jax.experimental import pallas as pl
from jax.experimental.pallas import tpu as pltpu

D_MODEL = 1024
DEPTH = 2
GRID_W = 64
N_BRANCH = 4
MIX_W = 256
GLA_HEADS = 4
GLA_DK = 32
GLA_DV = 64
GLA_RANK = 16
GLA_TAU = 16.0
HY_WIDTH = 256
HY_ORDER = 2
HY_BANDS = 16
HY_SHIFT = 0.05
HG_HEADS = 4
HG_DK = 64
HG_DV = 64
DA_HEADS = 4
DA_DK = 32
DA_DV = 64
ROPE_BASE = 10000.0
N_EXPERTS = 16
EXPERT_FF = 1024
CAPACITY_FACTOR = 2
ADA_CHUNKS = 6
EPS = 1e-6
F_TINY = 1e-20

IN_NAMES = ('gla_q', 'gla_k', 'gla_v', 'gla_af', 'gla_ab', 'gla_g', 'hy',
            'hg_q', 'hg_ff', 'hg_fb', 'hg_i', 'hg_g', 'da_q', 'da_k', 'da_v', 'merge')
IN_WIDTHS = (GLA_HEADS * GLA_DK, GLA_HEADS * GLA_DK, GLA_HEADS * GLA_DV, GLA_RANK, GLA_RANK, GLA_HEADS * GLA_DV,
             (1 + HY_ORDER) * HY_WIDTH,
             HG_HEADS * HG_DK, HG_HEADS * HG_DK, HG_HEADS * HG_DK, HG_HEADS * HG_DV, HG_HEADS * HG_DV,
             DA_HEADS * 2 * DA_DK, DA_HEADS * 2 * DA_DK, DA_HEADS * DA_DV,
             N_BRANCH * D_MODEL)
IN_OFFSETS = tuple(int(v) for v in np.cumsum((0,) + IN_WIDTHS)[:-1])

PROJ_GROUPS = (
    (('gla_q', 'gla_k', 'gla_v', 'gla_g', 'gla_af', 'gla_ab'), 896),
    (('hy',), 768),
    (('hg_q', 'hg_ff', 'hg_fb', 'hg_i', 'hg_g'), 1280),
    (('da_q', 'da_k', 'da_v'), 768),
    (('merge',), 4096),
)
PROJ_WIDTHS = tuple(w for _, w in PROJ_GROUPS)
LANES = 128
OUT_WIDTHS = (4 * GLA_HEADS * GLA_DK + 2 * GLA_HEADS * GLA_DV, PROJ_WIDTHS[1],
              5 * HG_HEADS * HG_DK + 2 * HG_HEADS * HG_DV, PROJ_WIDTHS[3], PROJ_WIDTHS[4])
OUT_DTYPES = (jnp.float32,) * 4 + (jnp.bfloat16,)
VMEM_LIMIT = 56 * 1024 * 1024

BF16 = jnp.bfloat16
F32 = jnp.float32
N_HEADS = 4
RC = 64
N_QK = 2 * DA_HEADS
VROWS = DA_DV + 8
LOG2E = 1.4426950408889634
CW = 256
FFN_MIN_ROWS = 256
HB = 256


def _split_bf16(a):
    hi = a.astype(BF16)
    return hi, (a - hi.astype(F32)).astype(BF16)


def _in_proj_kernel(x_ref, g_ref, sh_ref, sc_ref, w_ref, wa_ref, ba_ref, lb_ref,
                    gla_ref, hy_ref, hg_ref, da_ref, mg_ref):
    x = x_ref[0]
    ms = jnp.mean(x * x, axis=-1, keepdims=True)
    h = x * lax.rsqrt(ms + EPS) * g_ref[...]
    h = (h * (1.0 + sc_ref[0]) + sh_ref[0]).astype(BF16)
    offs = np.cumsum((0,) + PROJ_WIDTHS)

    def proj(k):
        return jnp.dot(h, w_ref[:, int(offs[k]):int(offs[k + 1])], preferred_element_type=F32)

    r = proj(0)
    qk = GLA_HEADS * GLA_DK
    vw = GLA_HEADS * GLA_DV
    gla_ref[0, :, 0:qk] = r[:, 0:qk] * (GLA_DK ** -0.5)
    gla_ref[0, :, qk:2 * qk] = r[:, qk:2 * qk]
    a_hi, a_lo = _split_bf16(r[:, 2 * qk + 2 * vw:])
    z = (jnp.dot(a_hi, wa_ref[0], preferred_element_type=F32) + jnp.dot(a_lo, wa_ref[0], preferred_element_type=F32)
         + jnp.dot(a_hi, wa_ref[1], preferred_element_type=F32) + ba_ref[...])
    gla_ref[0, :, 2 * qk:4 * qk] = jax.nn.log_sigmoid(z) * (1.0 / GLA_TAU)
    gla_ref[0, :, 4 * qk:4 * qk + 2 * vw] = r[:, 2 * qk:2 * qk + 2 * vw]

    hy_ref[0] = proj(1).T

    r = proj(2)
    hw = HG_HEADS * HG_DK
    q = r[:, 0:hw]
    hg_ref[0, :, 0:hw] = q * jax.nn.sigmoid(q)
    zf = r[:, hw:3 * hw]
    lb = lb_ref[...]
    hg_ref[0, :, hw:3 * hw] = (1.0 - lb) * jax.nn.sigmoid(-zf)
    hg_ref[0, :, 3 * hw:5 * hw] = jnp.log(jnp.maximum(lb + (1.0 - lb) * jax.nn.sigmoid(zf), F_TINY))
    hg_ref[0, :, 5 * hw:7 * hw] = r[:, 3 * hw:5 * hw]

    da_ref[0] = proj(3)
    mg_ref[0] = (0.5 * proj(4)).astype(mg_ref.dtype)


def _in_proj(x, g, shift, scale, w_groups, wa, ba, lb, tm):
    B, L, D = x.shape
    nw = w_groups.shape[1]
    const2 = lambda b, i: (0, 0)
    out_specs = [pl.BlockSpec((1, tm, w), lambda b, i: (b, i, 0)) for w in OUT_WIDTHS]
    out_shape = [jax.ShapeDtypeStruct((B, L, w), dt) for w, dt in zip(OUT_WIDTHS, OUT_DTYPES)]
    assert tm == HB
    out_specs[1] = pl.BlockSpec((1, OUT_WIDTHS[1], tm), lambda b, i: (i, 0, b))
    out_shape[1] = jax.ShapeDtypeStruct((L // tm, OUT_WIDTHS[1], B * tm), F32)
    return pl.pallas_call(
        _in_proj_kernel,
        grid=(B, L // tm),
        in_specs=[
            pl.BlockSpec((1, tm, D), lambda b, i: (b, i, 0)),
            pl.BlockSpec((1, D), const2),
            pl.BlockSpec((1, 1, D), lambda b, i: (b, 0, 0)),
            pl.BlockSpec((1, 1, D), lambda b, i: (b, 0, 0)),
            pl.BlockSpec((D, nw), const2, pipeline_mode=pl.Buffered(1)),
            pl.BlockSpec(wa.shape, lambda b, i: (0, 0, 0)),
            pl.BlockSpec(ba.shape, const2),
            pl.BlockSpec(lb.shape, const2),
        ],
        out_specs=out_specs,
        out_shape=out_shape,
        compiler_params=pltpu.CompilerParams(
            dimension_semantics=("parallel", "parallel"), vmem_limit_bytes=VMEM_LIMIT),
        name="in_proj",
    )(x, g.reshape(1, D), shift, scale, w_groups, wa, ba, lb)


def _merge_kernel(o0_ref, o1_ref, o2_ref, o3_ref, gate_ref, wb_ref, wo_ref, x_ref, m_ref, g_ref, sh_ref, sc_ref,
                  out_ref, h_ref, hb_ref):
    D = D_MODEL
    acc = None
    for i, o_ref in enumerate((o0_ref, o1_ref, o2_ref, o3_ref)):
        t = jnp.dot(o_ref[0].astype(BF16), wb_ref[i], preferred_element_type=F32)
        t = (jnp.tanh(gate_ref[0, :, i * D:(i + 1) * D].astype(F32)) + 1.0) * t
        acc = t if acc is None else acc + t
    mx = jnp.dot((0.5 * acc).astype(BF16), wo_ref[...], preferred_element_type=F32)
    out = x_ref[0] + m_ref[0] * mx
    out_ref[0] = out
    ms = jnp.mean(out * out, axis=-1, keepdims=True)
    h = out * lax.rsqrt(ms + EPS) * g_ref[...] * (1.0 + sc_ref[0]) + sh_ref[0]
    h_ref[0] = h
    hb_ref[0] = h.astype(BF16)


def _merge_out(outs, gate_cols, w_branch, w_out, x, m, g, shift, scale, tm):
    B, L, D = x.shape
    tok = pl.BlockSpec((1, tm, D), lambda b, i: (b, i, 0))
    per_sample = pl.BlockSpec((1, 1, D), lambda b, i: (b, 0, 0))
    return pl.pallas_call(
        _merge_kernel,
        grid=(B, L // tm),
        in_specs=[pl.BlockSpec((1, tm, MIX_W), lambda b, i: (b, i, 0)) for _ in range(N_BRANCH)] + [
            pl.BlockSpec((1, tm, N_BRANCH * D), lambda b, i: (b, i, 0)),
            pl.BlockSpec((N_BRANCH, MIX_W, D), lambda b, i: (0, 0, 0)),
            pl.BlockSpec((D, D), lambda b, i: (0, 0)),
            tok, per_sample,
            pl.BlockSpec((1, D), lambda b, i: (0, 0)), per_sample, per_sample,
        ],
        out_specs=[tok, tok, tok],
        out_shape=[jax.ShapeDtypeStruct((B, L, D), F32), jax.ShapeDtypeStruct((B, L, D), F32),
                   jax.ShapeDtypeStruct((B, L, D), BF16)],
        compiler_params=pltpu.CompilerParams(
            dimension_semantics=("parallel", "parallel"), vmem_limit_bytes=VMEM_LIMIT),
        name="merge_out",
    )(*outs, gate_cols, w_branch, w_out, x, m, g.reshape(1, D), shift, scale)


def _expert_ffn_kernel(x_ref, g_ref, w1_ref, w3_ref, w2_ref, out_ref, w1_scr, w3_scr, w2_scr):
    @pl.when(pl.program_id(1) == 0)
    def _():
        w1_scr[...] = w1_ref[0].astype(BF16)
        w3_scr[...] = w3_ref[0].astype(BF16)
        w2_scr[...] = w2_ref[0].astype(BF16)

    x = x_ref[0, 0].astype(BF16)
    a = jnp.dot(x, w1_scr[...], preferred_element_type=F32)
    b = jnp.dot(x, w3_scr[...], preferred_element_type=F32)
    h = (a * jax.nn.sigmoid(a) * b).astype(BF16)
    y = jnp.dot(h, w2_scr[...], preferred_element_type=F32)
    out_ref[0, 0] = (y * g_ref[0, 0]).astype(out_ref.dtype)


def _combine_kernel(idx_ref, y_ref, x_ref, m_ref, out_ref, *, ks):
    tt = out_ref.shape[1]
    S = y_ref.shape[1]
    base = pl.program_id(1) * tt
    acc = None
    for s0 in range(0, S, ks):
        tok = lax.broadcasted_iota(jnp.int32, (tt, ks), 0) + base
        onehot = jnp.where(tok == idx_ref[0, :, s0:s0 + ks], 1.0, 0.0).astype(BF16)
        part = jnp.dot(onehot, y_ref[0, s0:s0 + ks, :], preferred_element_type=F32)
        acc = part if acc is None else acc + part
    out_ref[0] = x_ref[0] + m_ref[0] * acc


def _combine_sorted_kernel(w0_ref, nw_ref, idx_ref, y_ref, x_ref, m_ref, out_ref, acc_scr, iw_scr, yw_scr, *, cap):
    tt = out_ref.shape[1]
    n_exp = y_ref.shape[1] // cap
    b, i = pl.program_id(0), pl.program_id(1)
    base = i * tt

    def start(e, w):
        entry = (b * pl.num_programs(1) + i) * n_exp + e
        return pl.multiple_of(e * cap + (w0_ref[entry] + w) * CW, CW), nw_ref[entry]

    for e in range(n_exp):
        s0, _ = start(e, 0)
        iw_scr[:, e * CW:(e + 1) * CW] = idx_ref[0, :, pl.ds(s0, CW)]
        yw_scr[e * CW:(e + 1) * CW, :] = y_ref[0, pl.ds(s0, CW), :]
    acc = None
    ks = 8 * CW
    for k0 in range(0, n_exp * CW, ks):
        tok = lax.broadcasted_iota(jnp.int32, (tt, ks), 0) + base
        onehot = jnp.where(tok == iw_scr[:, k0:k0 + ks], 1.0, 0.0).astype(BF16)
        part = jnp.dot(onehot, yw_scr[k0:k0 + ks, :], preferred_element_type=F32)
        acc = part if acc is None else acc + part
    acc_scr[...] = acc
    tok = lax.broadcasted_iota(jnp.int32, (tt, CW), 0) + base
    for e in range(n_exp):
        def window(w, carry):
            s0, _ = start(e, w)
            onehot = jnp.where(tok == idx_ref[0, :, pl.ds(s0, CW)], 1.0, 0.0).astype(BF16)
            acc_scr[...] += jnp.dot(onehot, y_ref[0, pl.ds(s0, CW), :], preferred_element_type=F32)
            return carry

        lax.fori_loop(1, start(e, 0)[1], window, 0)
    out_ref[0] = x_ref[0] + m_ref[0] * acc_scr[...]


def _combine_sorted(idx, y, x, m, tt):
    B, L, D = x.shape
    E, cap = idx.shape[1:]
    nt = L // tt
    below = jnp.sum(idx[:, :, None, :] < (jnp.arange(nt + 1, dtype=jnp.int32) * tt)[None, None, :, None],
                    axis=-1, dtype=jnp.int32)
    lo, hi = below[..., :-1], below[..., 1:]
    w0 = jnp.minimum(lo // CW, cap // CW - 1)
    nw = jnp.where(hi > lo, (hi - 1) // CW - w0 + 1, 0)
    flat = lambda a: a.transpose(0, 2, 1).reshape(-1).astype(jnp.int32)
    grid_spec = pltpu.PrefetchScalarGridSpec(
        num_scalar_prefetch=2,
        grid=(B, nt),
        in_specs=[
            pl.BlockSpec((1, 1, E * cap), lambda b, i, *_: (b, 0, 0)),
            pl.BlockSpec((1, E * cap, D), lambda b, i, *_: (b, 0, 0), pipeline_mode=pl.Buffered(1)),
            pl.BlockSpec((1, tt, D), lambda b, i, *_: (b, i, 0)),
            pl.BlockSpec((1, 1, D), lambda b, i, *_: (b, 0, 0)),
        ],
        out_specs=pl.BlockSpec((1, tt, D), lambda b, i, *_: (b, i, 0)),
        scratch_shapes=[pltpu.VMEM((tt, D), F32), pltpu.VMEM((1, E * CW), jnp.int32),
                        pltpu.VMEM((E * CW, D), BF16)],
    )
    return pl.pallas_call(
        functools.partial(_combine_sorted_kernel, cap=cap),
        grid_spec=grid_spec,
        out_shape=jax.ShapeDtypeStruct((B, L, D), F32),
        compiler_params=pltpu.CompilerParams(
            dimension_semantics=("parallel", "parallel"), vmem_limit_bytes=VMEM_LIMIT),
        name="moe_combine",
    )(flat(w0), flat(nw), idx.reshape(B, 1, E * cap), y, x, m)


def _combine(idx, y, x, m, tt):
    B, L, D = x.shape
    S = y.shape[1]
    return pl.pallas_call(
        functools.partial(_combine_kernel, ks=min(S, 2048)),
        grid=(B, L // tt),
        in_specs=[
            pl.BlockSpec((1, 1, S), lambda b, i: (b, 0, 0)),
            pl.BlockSpec((1, S, D), lambda b, i: (b, 0, 0)),
            pl.BlockSpec((1, tt, D), lambda b, i: (b, i, 0)),
            pl.BlockSpec((1, 1, D), lambda b, i: (b, 0, 0)),
        ],
        out_specs=pl.BlockSpec((1, tt, D), lambda b, i: (b, i, 0)),
        out_shape=jax.ShapeDtypeStruct((B, L, D), F32),
        compiler_params=pltpu.CompilerParams(
            dimension_semantics=("parallel", "parallel"), vmem_limit_bytes=VMEM_LIMIT),
        name="moe_combine",
    )(idx, y, x, m)


def _expert_ffn(xg, g, w1, w3, w2, layer):
    B, E, cap, D = xg.shape
    F = w1.shape[-1]
    return pl.pallas_call(
        _expert_ffn_kernel,
        grid=(E, B),
        in_specs=[
            pl.BlockSpec((1, 1, cap, D), lambda e, b: (b, e, 0, 0)),
            pl.BlockSpec((1, 1, cap, 1), lambda e, b: (b, e, 0, 0)),
            pl.BlockSpec((None, 1, D, F), lambda e, b: (layer, e, 0, 0)),
            pl.BlockSpec((None, 1, D, F), lambda e, b: (layer, e, 0, 0)),
            pl.BlockSpec((None, 1, F, D), lambda e, b: (layer, e, 0, 0)),
        ],
        out_specs=pl.BlockSpec((1, 1, cap, D), lambda e, b: (b, e, 0, 0)),
        out_shape=jax.ShapeDtypeStruct((B, E, cap, D), BF16),
        scratch_shapes=[pltpu.VMEM((D, F), BF16), pltpu.VMEM((D, F), BF16), pltpu.VMEM((F, D), BF16)],
        compiler_params=pltpu.CompilerParams(
            dimension_semantics=("parallel", "arbitrary"), vmem_limit_bytes=VMEM_LIMIT),
        name="expert_ffn",
    )(xg, g, w1, w3, w2)


def _recur_tables(C, rev):
    pos = (C - 1 - np.arange(C)) if rev else np.arange(C)
    row_of = np.argsort(pos)
    levels = int(np.log2(C))
    diff = np.zeros((levels + 1, C, C), np.float32)
    mask = np.zeros((levels + 1, C, C), np.float32)
    for t in range(C):
        p = pos[t]
        diff[0, t, row_of[:p + 1]] = 1.0
        mask[0, t, t] = 1.0
        for l in range(1, levels + 1):
            n, m = 2 ** l, 2 ** (l - 1)
            off = p % n
            mid = p - off + m - 1
            if off >= m:
                diff[l, t, row_of[mid + 1:p + 1]] = 1.0
                mask[l, t, row_of[p - off:mid + 1]] = 1.0
            else:
                diff[l, t, row_of[p + 1:mid + 1]] = 1.0
    return (jnp.asarray(np.tile(diff.reshape((levels + 1) * C, C), (1, 3)), BF16),
            jnp.asarray(np.tile(mask, (1, 1, N_HEADS))))


def _recur_kernel(*refs, rev, C, nsub, dk, dv, finish):
    if finish:
        (q_ref, k_ref, v_ref, g_ref, s0_ref, d_ref, m_ref, prev_ref, gate_ref, gn_ref,
         o_ref, sfin_ref, s_scr, e_scr, a_scr) = refs
    else:
        q_ref, k_ref, v_ref, g_ref, s0_ref, d_ref, m_ref, o_ref, sfin_ref, s_scr, e_scr, a_scr = refs
    HK = N_HEADS * dk
    HV = N_HEADS * dv
    i = pl.program_id(1)
    if finish:
        head_mean = jnp.where(lax.broadcasted_iota(jnp.int32, (HV, HV), 0) // dv
                              == lax.broadcasted_iota(jnp.int32, (HV, HV), 1) // dv, 1.0 / dv, 0.0).astype(BF16)

    @pl.when(i == 0)
    def _():
        s_scr[...] = s0_ref[0]

    head_k = [(lax.broadcasted_iota(jnp.int32, (1, HK), 1) // dk == h).astype(BF16) for h in range(N_HEADS)]
    head_v = [(lax.broadcasted_iota(jnp.int32, (1, HV), 1) // dv == h).astype(BF16) for h in range(N_HEADS)]
    bd = (lax.broadcasted_iota(jnp.int32, (HV, HK), 0) // dv
          == lax.broadcasted_iota(jnp.int32, (HV, HK), 1) // dk)
    levels = m_ref.shape[0] - 1

    def stack_heads(x, head):
        xb = x.astype(BF16)
        return jnp.concatenate([xb * head[h] for h in range(N_HEADS)], axis=0)

    def nt(a, b):
        return lax.dot_general(a, b, (((1,), (1,)), ((), ())), preferred_element_type=F32)

    order = [(nsub - 1 - j) if rev else j for j in range(nsub)]
    for c in order:
        g = g_ref[0, c * C:(c + 1) * C, :]
        g_hi = g.astype(BF16)
        r1 = g - g_hi.astype(F32)
        g_mid = r1.astype(BF16)
        g_lo = (r1 - g_mid.astype(F32)).astype(BF16)
        e_scr[c] = jnp.dot(d_ref[...], jnp.concatenate([g_hi, g_mid, g_lo], axis=0), preferred_element_type=F32)
    for c in order:
        q = q_ref[0, c * C:(c + 1) * C, :]
        k = k_ref[0, c * C:(c + 1) * C, :]
        a = nt(q.astype(BF16), stack_heads(k, head_k)) * m_ref[0]
        for lvl in range(1, levels + 1):
            x = jnp.exp(e_scr[c, lvl * C:(lvl + 1) * C, :])
            a = a + nt((q * x).astype(BF16), stack_heads(k * x, head_k)) * m_ref[lvl]
        a_scr[c] = a.astype(BF16)
    for c in order:
        sl = slice(c * C, (c + 1) * C)
        q = q_ref[0, sl, :]
        k = k_ref[0, sl, :]
        v = v_ref[0, sl, :]
        b = e_scr[c, 0:C, :]
        st = s_scr[...]
        o = jnp.dot(a_scr[c], stack_heads(v, head_v), preferred_element_type=F32)
        o = o + nt((q * jnp.exp(b)).astype(BF16), st.astype(BF16))
        if finish:
            o = o + prev_ref[0, sl, :]
            sq_hi, sq_lo = _split_bf16(o * o)
            ms = (jnp.dot(sq_hi, head_mean, preferred_element_type=F32)
                  + jnp.dot(sq_lo, head_mean, preferred_element_type=F32))
            gate = gate_ref[0, sl, :]
            o = o * lax.rsqrt(ms + EPS) * gn_ref[...] * (gate * jax.nn.sigmoid(gate))
        o_ref[0, sl, :] = o
        b_end = b[0:1, :] if rev else b[C - 1:C, :]
        kend = (k * jnp.exp(b_end - b)).astype(BF16)
        upd = lax.dot_general(v.astype(BF16), kend, (((0,), (0,)), ((), ())), preferred_element_type=F32)
        s_scr[...] = st * jnp.exp(b_end) + jnp.where(bd, upd, 0.0)

    @pl.when(i == pl.num_programs(1) - 1)
    def _():
        sfin_ref[0] = s_scr[...]


def _recurrence(pack, cols, s0, prev, gnorm, *, rev, dk, dv, tb):
    B, L, _ = pack.shape
    HK, HV = N_HEADS * dk, N_HEADS * dv
    C = RC
    nblk = L // tb
    diff, masks = _recur_tables(C, rev)
    cq, ck, cg, cv, cgate = cols

    def tok(col):
        return (lambda b, i: (b, nblk - 1 - i, col)) if rev else (lambda b, i: (b, i, col))

    in_specs = [
        pl.BlockSpec((1, tb, HK), tok(cq)), pl.BlockSpec((1, tb, HK), tok(ck)),
        pl.BlockSpec((1, tb, HV), tok(cv)), pl.BlockSpec((1, tb, HK), tok(cg)),
        pl.BlockSpec((1, HV, HK), lambda b, i: (b, 0, 0)),
        pl.BlockSpec(diff.shape, lambda b, i: (0, 0)),
        pl.BlockSpec(masks.shape, lambda b, i: (0, 0, 0)),
    ]
    args = [pack, pack, pack, pack, s0, diff, masks]
    if prev is not None:
        in_specs += [pl.BlockSpec((1, tb, HV), tok(0)), pl.BlockSpec((1, tb, HV), tok(cgate)),
                     pl.BlockSpec((1, HV), lambda b, i: (0, 0))]
        args += [prev, pack, gnorm]
    kern = functools.partial(_recur_kernel, rev=rev, C=C, nsub=tb // C, dk=dk, dv=dv, finish=prev is not None)
    return pl.pallas_call(
        kern,
        grid=(B, nblk),
        in_specs=in_specs,
        out_specs=[pl.BlockSpec((1, tb, HV), tok(0)), pl.BlockSpec((1, HV, HK), lambda b, i: (b, 0, 0))],
        out_shape=[jax.ShapeDtypeStruct((B, L, HV), F32), jax.ShapeDtypeStruct((B, HV, HK), F32)],
        scratch_shapes=[pltpu.VMEM((HV, HK), F32), pltpu.VMEM((tb // C, diff.shape[0], HK), F32),
                        pltpu.VMEM((tb // C, C, N_HEADS * C), BF16)],
        compiler_params=pltpu.CompilerParams(
            dimension_semantics=("parallel", "arbitrary"), vmem_limit_bytes=VMEM_LIMIT),
        name="recur_rev" if rev else "recur_fwd",
    )(*args)


def _bidir(pack_c, pack_l, cols_f, cols_b, gnorm, *, dk, dv):
    B, Lc, _ = pack_c.shape
    s0 = jnp.zeros((B, N_HEADS * dv, N_HEADS * dk), F32)
    gn = jnp.tile(gnorm.astype(F32), N_HEADS).reshape(1, N_HEADS * dv)
    kw = dict(dk=dk, dv=dv)
    oc_f, sc_f = _recurrence(pack_c, cols_f, s0, None, None, rev=False, tb=Lc, **kw)
    ol_f, _ = _recurrence(pack_l, cols_f, sc_f, None, None, rev=False, tb=512, **kw)
    oc, sc_b = _recurrence(pack_c, cols_b, s0, oc_f, gn, rev=True, tb=Lc, **kw)
    ol, _ = _recurrence(pack_l, cols_b, sc_b, ol_f, gn, rev=True, tb=512, **kw)
    return oc, ol


def _attn_kernel(lam_ref, qt_ref, k_ref, vt_ref, g_ref, o_ref, qm_scr, s_scr, p_scr, m_scr, a_scr, acc_scr, *,
                 first, tk):
    W = qt_ref.shape[1]
    tq = qt_ref.shape[2]
    Lk = k_ref.shape[1]
    qt = qt_ref[0]
    row_pair = lax.broadcasted_iota(jnp.int32, (W, tq), 0) // DA_DK
    for j in range(N_QK):
        qm_scr[j] = jnp.where(row_pair == j, qt, 0.0).astype(BF16)
    m_scr[...] = jnp.full(m_scr.shape, -jnp.inf, F32)
    acc_scr[...] = jnp.zeros(acc_scr.shape, F32)

    def block(start, size):
        ks = pl.ds(start, size)
        kblk = k_ref[0, ks, :]
        for j in range(N_QK):
            s_scr[j, :size] = jnp.dot(kblk, qm_scr[j], preferred_element_type=F32)
        for j in range(N_QK):
            s = s_scr[j, :size]
            m_old = m_scr[j]
            m_new = jnp.maximum(m_old, jnp.max(s, axis=0, keepdims=True))
            a_scr[j] = jnp.exp2(m_old - m_new)
            p_scr[j, :size] = jnp.exp2(s - m_new).astype(BF16)
            m_scr[j] = m_new
        for j in range(N_QK):
            pv = jnp.dot(vt_ref[0, j // 2, :, ks], p_scr[j, :size], preferred_element_type=F32)
            acc_scr[j] = a_scr[j] * acc_scr[j] + pv

    block(0, first)

    def body(i, carry):
        block(pl.multiple_of(first + i * tk, math.gcd(first, tk)), tk)
        return carry

    lax.fori_loop(0, (Lk - first) // tk, body, 0)
    lam = lam_ref[0]
    outs = []
    for h in range(DA_HEADS):
        a1, a2 = acc_scr[2 * h], acc_scr[2 * h + 1]
        o = a1[:DA_DV] / a1[DA_DV:DA_DV + 1] - lam * (a2[:DA_DV] / a2[DA_DV:DA_DV + 1])
        ms = jnp.mean(o * o, axis=0, keepdims=True)
        outs.append(o * lax.rsqrt(ms + EPS) * g_ref[...])
    o_ref[0] = jnp.concatenate(outs, axis=0).T


def _diff_attention(qt, k, vt, lam, gcol, *, tq, first, tk):
    B, W, Lq = qt.shape
    Lk = k.shape[1]
    assert (Lk - first) % tk == 0
    rows = max(first, tk)
    HV = DA_HEADS * DA_DV
    return pl.pallas_call(
        functools.partial(_attn_kernel, first=first, tk=tk),
        grid=(B, Lq // tq),
        in_specs=[
            pl.BlockSpec(memory_space=pltpu.SMEM),
            pl.BlockSpec((1, W, tq), lambda b, i: (b, 0, i)),
            pl.BlockSpec((1, Lk, W), lambda b, i: (b, 0, 0)),
            pl.BlockSpec((1, DA_HEADS, VROWS, Lk), lambda b, i: (b, 0, 0, 0)),
            pl.BlockSpec((DA_DV, 1), lambda b, i: (0, 0)),
        ],
        out_specs=pl.BlockSpec((1, tq, HV), lambda b, i: (b, i, 0)),
        out_shape=jax.ShapeDtypeStruct((B, Lq, HV), F32),
        scratch_shapes=[
            pltpu.VMEM((N_QK, W, tq), BF16),
            pltpu.VMEM((N_QK, rows, tq), F32),
            pltpu.VMEM((N_QK, rows, tq), BF16),
            pltpu.VMEM((N_QK, 1, tq), F32),
            pltpu.VMEM((N_QK, 1, tq), F32),
            pltpu.VMEM((N_QK, VROWS, tq), F32),
        ],
        compiler_params=pltpu.CompilerParams(
            dimension_semantics=("parallel", "parallel"), vmem_limit_bytes=VMEM_LIMIT),
        name="diff_attn",
    )(lam, qt, k, vt, gcol)


def _hyena_kernel(w_ref, bias_ref, uv_ref, u1_ref, u2_ref, hp_ref, o_ref, acc_scr, *, nb, nbatch):
    c = pl.program_id(0)
    half = nb // 2
    rows = nb * nbatch
    lane = lax.broadcasted_iota(jnp.int32, (rows, HB), 1)
    row = lax.broadcasted_iota(jnp.int32, (rows, HB), 0)

    def short_conv(u_ref, ch):
        u = u_ref[:, 0].reshape(rows, HB)
        prev = pltpu.roll(u, 1, axis=1)
        prev = jnp.where(lane == 0, jnp.where(row >= nbatch, pltpu.roll(prev, nbatch % rows, axis=0), 0.0), prev)
        nxt = pltpu.roll(u, HB - 1, axis=1)
        nxt = jnp.where(lane == HB - 1,
                        jnp.where(row < rows - nbatch, pltpu.roll(nxt, (rows - nbatch) % rows, axis=0), 0.0), nxt)
        return w_ref[0, ch] * prev + w_ref[1, ch] * u + w_ref[2, ch] * nxt

    def long_conv(z, o):
        zb = z.astype(BF16)
        hrow = hp_ref[o, 0]
        acc_scr[...] = jnp.zeros(acc_scr.shape, F32)
        for off in range(-half, half + 1):
            lo = (off + half) * HB
            r = jnp.broadcast_to(hrow[:, lo:lo + 2 * HB], (HB, 2 * HB))
            w = pltpu.roll(r, 1, 1, stride=1, stride_axis=0)[:, HB:].astype(BF16)
            s0, s1 = max(0, -off), nb - max(0, off)
            if s1 <= s0:
                continue
            src = slice(s0 * nbatch, s1 * nbatch)
            dst = slice((s0 + off) * nbatch, (s1 + off) * nbatch)
            acc_scr[dst, :] += jnp.dot(zb[src], w, preferred_element_type=F32)
        return acc_scr[...]

    z = short_conv(uv_ref, c)
    for o, u_ref in enumerate((u1_ref, u2_ref)):
        gate = short_conv(u_ref, (o + 1) * HY_WIDTH + c)
        z = gate * (long_conv(z, o) + bias_ref[o, c] * z)
    o_ref[0] = z.reshape(nb, nbatch, HB)


def _hyena_fused(ut, conv_w, hp, bias):
    nb, _, B, _ = ut.shape
    C = HY_WIDTH
    blk = lambda k: pl.BlockSpec((nb, 1, B, HB), lambda c: (0, c + k * C, 0, 0))
    return pl.pallas_call(
        functools.partial(_hyena_kernel, nb=nb, nbatch=B),
        grid=(C,),
        in_specs=[
            pl.BlockSpec(memory_space=pltpu.SMEM),
            pl.BlockSpec(memory_space=pltpu.SMEM),
            blk(0), blk(1), blk(2),
            pl.BlockSpec((HY_ORDER, 1, 1, hp.shape[-1]), lambda c: (0, c, 0, 0)),
        ],
        out_specs=pl.BlockSpec((1, nb, B, HB), lambda c: (c, 0, 0, 0)),
        out_shape=jax.ShapeDtypeStruct((C, nb, B, HB), F32),
        scratch_shapes=[pltpu.VMEM((nb * B, HB), F32)],
        compiler_params=pltpu.CompilerParams(
            dimension_semantics=("parallel",), vmem_limit_bytes=VMEM_LIMIT),
        name="hyena_branch",
    )(conv_w, bias, ut, ut, ut, hp)


def _rms_norm(x, g):
    xf = x.astype(F32)
    y = xf * lax.rsqrt(jnp.mean(xf * xf, axis=-1, keepdims=True) + EPS)
    return (y * g.astype(F32)).astype(x.dtype)


def _gate_weights(wa2, ba):
    qk = GLA_HEADS * GLA_DK
    w = jnp.zeros((LANES, 2 * qk), F32)
    w = w.at[:GLA_RANK, :qk].set(wa2[0].astype(F32)).at[GLA_RANK:2 * GLA_RANK, qk:].set(wa2[1].astype(F32))
    hi = w.astype(BF16)
    lo = (w - hi.astype(F32)).astype(BF16)
    return jnp.stack([hi, lo]), ba.astype(F32).reshape(1, 2 * qk)


def _hyena_filters(L, p):
    j = jnp.arange(L, dtype=F32)
    t = j / max(L - 1, 1)
    w = 2 * math.pi * j / L
    f = jnp.linspace(1e-4, HY_BANDS - 1, HY_BANDS, dtype=F32)
    feats = jnp.concatenate([t[:, None], jnp.cos(w[:, None] * f), -jnp.sin(w[:, None] * f)], axis=-1)
    h = jnp.sin(p['hy_freq'][0] * (feats @ p['hy_w1'] + p['hy_b1']))
    h = jnp.sin(p['hy_freq'][1] * (h @ p['hy_w2'] + p['hy_b2']))
    h = (h @ p['hy_w3']).astype(F32)
    dist = jnp.abs(j - L // 2) / (L // 2)
    h = h * (jnp.exp(-dist[:, None] * jnp.abs(p['hy_decay'].astype(F32))) + HY_SHIFT)
    h = h / jnp.sum(jnp.abs(h), axis=0, keepdims=True)
    return h.reshape(L, HY_ORDER, HY_WIDTH)


def _from_blocks(a):
    C, nb, B, _ = a.shape
    return a.transpose(2, 1, 3, 0).reshape(B, nb * HB, C)


def _pad_filter(h):
    L = h.shape[0]
    nb = L // HB
    left = (HB - 1) - (L // 2 - (nb // 2) * HB)
    return jnp.pad(h.T, ((0, 0), (left, (nb + 2) * HB - L - left)))[:, None, :]


def _hyena_branch(ut3, p):
    nb, C3, BH = ut3.shape
    h = _hyena_filters(nb * HB, p)
    hp = jnp.stack([_pad_filter(h[:, o]) for o in range(HY_ORDER)])
    ut = ut3.reshape(nb, C3, BH // HB, HB)
    return _from_blocks(_hyena_fused(ut, p['hy_conv_w'], hp, p['hy_bias']))


def _rope_tables(L):
    quarter = DA_DK // 4
    freqs = ROPE_BASE ** (-jnp.arange(quarter, dtype=F32) / quarter)
    row = (jnp.arange(L) // GRID_W).astype(F32)[:, None] * freqs
    col = (jnp.arange(L) % GRID_W).astype(F32)[:, None] * freqs
    return jnp.cos(row), jnp.sin(row), jnp.cos(col), jnp.sin(col)


def _da_prep(t, g, tables):
    B, L, W = t.shape
    t = _rms_norm(t.reshape(B, L, N_QK, DA_DK), g)
    if tables is not None:
        cr, sr, cc, sc = [a[None, :, None, :] for a in tables]
        e = DA_DK // 4
        a1, a2, b1, b2 = t[..., :e], t[..., e:2 * e], t[..., 2 * e:3 * e], t[..., 3 * e:]
        t = jnp.concatenate([a1 * cr - a2 * sr, a1 * sr + a2 * cr, b1 * cc - b2 * sc, b1 * sc + b2 * cc], axis=-1)
    return t.reshape(B, L, W)


def _value_rows(v):
    B, Lk, _ = v.shape
    vt = v.reshape(B, Lk, DA_HEADS, DA_DV).transpose(0, 2, 3, 1)
    extra = jnp.concatenate([jnp.ones((B, DA_HEADS, 1, Lk), v.dtype),
                             jnp.zeros((B, DA_HEADS, VROWS - DA_DV - 1, Lk), v.dtype)], axis=2)
    return jnp.concatenate([vt, extra], axis=2).astype(BF16)


def _da_parts(da):
    w = DA_HEADS * 2 * DA_DK
    return da[..., :w], da[..., w:2 * w], da[..., 2 * w:]


def _group_weights(w_in):
    cols = []
    for names, width in PROJ_GROUPS:
        used = 0
        for nm in names:
            i = IN_NAMES.index(nm)
            cols.append(w_in[:, IN_OFFSETS[i]:IN_OFFSETS[i] + IN_WIDTHS[i]])
            used += IN_WIDTHS[i]
        if width > used:
            cols.append(jnp.zeros((w_in.shape[0], width - used), w_in.dtype))
    return jnp.concatenate(cols, axis=1).astype(BF16)


GLA_COLS_F, GLA_COLS_B = (0, 1, 2, 2, 3), (0, 1, 3, 2, 3)
HG_COLS_F, HG_COLS_B = (0, 1, 3, 5, 6), (0, 2, 4, 5, 6)


def _token_mixer(gc, gx, p, lam_init, last):
    gla_c, gla_x = _bidir(gc[0], gx[0], GLA_COLS_F, GLA_COLS_B, p['gla_norm_g'], dk=GLA_DK, dv=GLA_DV)
    hg_c, hg_x = _bidir(gc[2], gx[2], HG_COLS_F, HG_COLS_B, p['hg_norm_g'], dk=HG_DK, dv=HG_DV)
    (dqc, dkc, dvc), (dqx, dkx, dvx) = _da_parts(gc[3]), _da_parts(gx[3])
    tables = _rope_tables(dqx.shape[1])
    scale = DA_DK ** -0.5 * LOG2E
    qx = (_da_prep(dqx, p['da_qnorm_g'], tables) * scale).transpose(0, 2, 1)
    kx = _da_prep(dkx, p['da_knorm_g'], tables)
    kc = _da_prep(dkc, p['da_knorm_g'], None)
    k_all = jnp.concatenate([kc, kx], axis=1).astype(BF16)
    vt_all = _value_rows(jnp.concatenate([dvc, dvx], axis=1))
    lp = p['da_lam'].astype(F32)
    lam = (jnp.exp(jnp.sum(lp[0] * lp[1])) - jnp.exp(jnp.sum(lp[2] * lp[3])) + lam_init).reshape(1)
    gcol = (p['da_norm_g'].astype(F32) * (1 - lam_init)).reshape(DA_DV, 1)
    Lc = kc.shape[1]
    da_x = _diff_attention(qx, k_all, vt_all, lam, gcol, tq=256, first=Lc, tk=512)
    outs_x = (gla_x, _hyena_branch(gx[1], p), hg_x, da_x)
    if last:
        return None, outs_x
    qc = (_da_prep(dqc, p['da_qnorm_g'], None) * scale).transpose(0, 2, 1)
    da_c = _diff_attention(qc, k_all[:, :Lc], vt_all[..., :Lc], lam, gcol, tq=Lc, first=Lc, tk=Lc)
    return (gla_c, _hyena_branch(gc[1], p), hg_c, da_c), outs_x


def _expert_choice_moe(x, m, h, hb, router, w1, w3, w2, layer):
    B, L, D = h.shape
    cap = CAPACITY_FACTOR * L // N_EXPERTS
    aff = jax.nn.softmax((h @ router).astype(F32), axis=-1)
    g, idx = lax.top_k(aff.transpose(0, 2, 1), cap)
    windowed = cap % CW == 0
    if windowed:
        idx, g = lax.sort((idx, g), dimension=-1, num_keys=1)
    xg = jax.vmap(lambda rows, ib: rows[ib])(hb, idx)
    if cap < FFN_MIN_ROWS:
        regroup = lambda t: t.transpose(1, 0, 2, 3).reshape(1, N_EXPERTS, B * cap, t.shape[-1])
        y = _expert_ffn(regroup(xg), regroup(g[..., None]), w1, w3, w2, layer)
        y = y.reshape(N_EXPERTS, B, cap, D).transpose(1, 0, 2, 3)
    else:
        y = _expert_ffn(xg, g[..., None], w1, w3, w2, layer)
    y = y.reshape(B, N_EXPERTS * cap, D)
    if windowed:
        return _combine_sorted(idx, y, x, m, tt=512)
    return _combine(idx.reshape(B, 1, N_EXPERTS * cap), y, x, m, tt=min(L, 512))


def kernel(x, c, ctx, c_ctx, ada_w, ada_b, norm1_g, norm2_g, w_in, gla_wa2, gla_ba, gla_norm_g,
           hy_conv_w, hy_w1, hy_b1, hy_w2, hy_b2, hy_w3, hy_freq, hy_decay, hy_bias,
           hg_lower, hg_norm_g, da_qnorm_g, da_knorm_g, da_lam, da_norm_g, w_branch, w_out,
           moe_router, moe_w1, moe_w3, moe_w2):
    B, L, D = x.shape
    P = jax.nn.softmax(hg_lower.astype(F32), axis=0)
    lower = jnp.cumsum(P, axis=0) - P[0]
    sc = jax.nn.silu(c)
    scc = jax.nn.silu(c_ctx)
    xc, xx = ctx, x
    for l in range(DEPTH):
        last = l == DEPTH - 1
        lam_init = 0.8 - 0.6 * math.exp(-0.3 * l)
        p = {'gla_wa2': gla_wa2[l], 'gla_ba': gla_ba[l], 'gla_norm_g': gla_norm_g[l],
             'hy_conv_w': hy_conv_w[l], 'hy_w1': hy_w1[l], 'hy_b1': hy_b1[l], 'hy_w2': hy_w2[l], 'hy_b2': hy_b2[l],
             'hy_w3': hy_w3[l], 'hy_freq': hy_freq[l], 'hy_decay': hy_decay[l], 'hy_bias': hy_bias[l],
             'hg_norm_g': hg_norm_g[l], 'da_qnorm_g': da_qnorm_g[l], 'da_knorm_g': da_knorm_g[l],
             'da_lam': da_lam[l], 'da_norm_g': da_norm_g[l]}
        mod_x = jnp.split((sc @ ada_w[l] + ada_b[l])[:, None, :], ADA_CHUNKS, axis=-1)
        mod_c1 = jnp.split((scc @ ada_w[l] + ada_b[l])[None, None, :], ADA_CHUNKS, axis=-1)
        mod_c = [jnp.broadcast_to(m, (B, 1, D)) for m in mod_c1]
        wg = _group_weights(w_in[l])
        wa, ba = _gate_weights(gla_wa2[l], gla_ba[l])
        lb = lower[l].reshape(1, 2 * HG_HEADS * HG_DK)
        wb = w_branch[l].astype(BF16)
        wo = w_out[l].astype(BF16)
        gx = _in_proj(xx, norm1_g[l], mod_x[0], mod_x[1], wg, wa, ba, lb, tm=256)
        gc = _in_proj(xc, norm1_g[l], mod_c[0], mod_c[1], wg, wa, ba, lb, tm=256)
        outs_c, outs_x = _token_mixer(gc, gx, p, lam_init, last)
        xx, hx, hxb = _merge_out(outs_x, gx[4], wb, wo, xx, mod_x[2], norm2_g[l], mod_x[3], mod_x[4], tm=512)
        xx = _expert_choice_moe(xx, mod_x[5], hx, hxb, moe_router[l], moe_w1, moe_w3, moe_w2, l)
        if not last:
            xc, hc, hcb = _merge_out(outs_c, gc[4], wb, wo, xc, mod_c[2], norm2_g[l], mod_c[3], mod_c[4], tm=256)
            xc = _expert_choice_moe(xc, mod_c[5], hc, hcb, moe_router[l], moe_w1, moe_w3, moe_w2, l)
    return xx
```

```python
import functools
import math

import jax
import jax.numpy as jnp
import numpy as np
from jax import lax
from jax.experimental import pallas as pl
from jax.experimental.pallas import tpu as pltpu

D_MODEL = 1024
DEPTH = 2
GRID_W = 64
N_BRANCH = 4
MIX_W = 256
GLA_HEADS = 4
GLA_DK = 32
GLA_DV = 64
GLA_RANK = 16
GLA_TAU = 16.0
HY_WIDTH = 256
HY_ORDER = 2
HY_BANDS = 16
HY_SHIFT = 0.05
HG_HEADS = 4
HG_DK = 64
HG_DV = 64
DA_HEADS = 4
DA_DK = 32
DA_DV = 64
ROPE_BASE = 10000.0
N_EXPERTS = 16
EXPERT_FF = 1024
CAPACITY_FACTOR = 2
ADA_CHUNKS = 6
EPS = 1e-6
F_TINY = 1e-20

IN_NAMES = ('gla_q', 'gla_k', 'gla_v', 'gla_af', 'gla_ab', 'gla_g', 'hy',
            'hg_q', 'hg_ff', 'hg_fb', 'hg_i', 'hg_g', 'da_q', 'da_k', 'da_v', 'merge')
IN_WIDTHS = (GLA_HEADS * GLA_DK, GLA_HEADS * GLA_DK, GLA_HEADS * GLA_DV, GLA_RANK, GLA_RANK, GLA_HEADS * GLA_DV,
             (1 + HY_ORDER) * HY_WIDTH,
             HG_HEADS * HG_DK, HG_HEADS * HG_DK, HG_HEADS * HG_DK, HG_HEADS * HG_DV, HG_HEADS * HG_DV,
             DA_HEADS * 2 * DA_DK, DA_HEADS * 2 * DA_DK, DA_HEADS * DA_DV,
             N_BRANCH * D_MODEL)
IN_OFFSETS = tuple(int(v) for v in np.cumsum((0,) + IN_WIDTHS)[:-1])

PROJ_GROUPS = (
    (('gla_q', 'gla_k', 'gla_v', 'gla_g', 'gla_af', 'gla_ab'), 896),
    (('hy',), 768),
    (('hg_q', 'hg_ff', 'hg_fb', 'hg_i', 'hg_g'), 1280),
    (('da_q', 'da_k', 'da_v'), 768),
    (('merge',), 4096),
)
PROJ_WIDTHS = tuple(w for _, w in PROJ_GROUPS)
LANES = 128
OUT_WIDTHS = (4 * GLA_HEADS * GLA_DK + 2 * GLA_HEADS * GLA_DV, PROJ_WIDTHS[1],
              5 * HG_HEADS * HG_DK + 2 * HG_HEADS * HG_DV, PROJ_WIDTHS[3], PROJ_WIDTHS[4])
OUT_DTYPES = (jnp.float32,) * 4 + (jnp.bfloat16,)
VMEM_LIMIT = 56 * 1024 * 1024

BF16 = jnp.bfloat16
F32 = jnp.float32
N_HEADS = 4
RC = 64
N_QK = 2 * DA_HEADS
VROWS = DA_DV + 8
LOG2E = 1.4426950408889634
CW = 256
FFN_MIN_ROWS = 256
HB = 256


def _split_bf16(a):
    hi = a.astype(BF16)
    return hi, (a - hi.astype(F32)).astype(BF16)


def _in_proj_kernel(x_ref, g_ref, sh_ref, sc_ref, w_ref, wa_ref, ba_ref, lb_ref,
                    gla_ref, hy_ref, hg_ref, da_ref, mg_ref):
    x = x_ref[0]
    ms = jnp.mean(x * x, axis=-1, keepdims=True)
    h = x * lax.rsqrt(ms + EPS) * g_ref[...]
    h = (h * (1.0 + sc_ref[0]) + sh_ref[0]).astype(BF16)
    offs = np.cumsum((0,) + PROJ_WIDTHS)

    def proj(k):
        return jnp.dot(h, w_ref[:, int(offs[k]):int(offs[k + 1])], preferred_element_type=F32)

    r = proj(0)
    qk = GLA_HEADS * GLA_DK
    vw = GLA_HEADS * GLA_DV
    gla_ref[0, :, 0:qk] = r[:, 0:qk] * (GLA_DK ** -0.5)
    gla_ref[0, :, qk:2 * qk] = r[:, qk:2 * qk]
    a_hi, a_lo = _split_bf16(r[:, 2 * qk + 2 * vw:])
    z = (jnp.dot(a_hi, wa_ref[0], preferred_element_type=F32) + jnp.dot(a_lo, wa_ref[0], preferred_element_type=F32)
         + jnp.dot(a_hi, wa_ref[1], preferred_element_type=F32) + ba_ref[...])
    gla_ref[0, :, 2 * qk:4 * qk] = jax.nn.log_sigmoid(z) * (1.0 / GLA_TAU)
    gla_ref[0, :, 4 * qk:4 * qk + 2 * vw] = r[:, 2 * qk:2 * qk + 2 * vw]

    hy_ref[0] = proj(1).T

    r = proj(2)
    hw = HG_HEADS * HG_DK
    q = r[:, 0:hw]
    hg_ref[0, :, 0:hw] = q * jax.nn.sigmoid(q)
    zf = r[:, hw:3 * hw]
    lb = lb_ref[...]
    hg_ref[0, :, hw:3 * hw] = (1.0 - lb) * jax.nn.sigmoid(-zf)
    hg_ref[0, :, 3 * hw:5 * hw] = jnp.log(jnp.maximum(lb + (1.0 - lb) * jax.nn.sigmoid(zf), F_TINY))
    hg_ref[0, :, 5 * hw:7 * hw] = r[:, 3 * hw:5 * hw]

    da_ref[0] = proj(3)
    mg_ref[0] = (0.5 * proj(4)).astype(mg_ref.dtype)


def _in_proj(x, g, shift, scale, w_groups, wa, ba, lb, tm):
    B, L, D = x.shape
    nw = w_groups.shape[1]
    const2 = lambda b, i: (0, 0)
    out_specs = [pl.BlockSpec((1, tm, w), lambda b, i: (b, i, 0)) for w in OUT_WIDTHS]
    out_shape = [jax.ShapeDtypeStruct((B, L, w), dt) for w, dt in zip(OUT_WIDTHS, OUT_DTYPES)]
    assert tm == HB
    out_specs[1] = pl.BlockSpec((1, OUT_WIDTHS[1], tm), lambda b, i: (i, 0, b))
    out_shape[1] = jax.ShapeDtypeStruct((L // tm, OUT_WIDTHS[1], B * tm), F32)
    return pl.pallas_call(
        _in_proj_kernel,
        grid=(B, L // tm),
        in_specs=[
            pl.BlockSpec((1, tm, D), lambda b, i: (b, i, 0)),
            pl.BlockSpec((1, D), const2),
            pl.BlockSpec((1, 1, D), lambda b, i: (b, 0, 0)),
            pl.BlockSpec((1, 1, D), lambda b, i: (b, 0, 0)),
            pl.BlockSpec((D, nw), const2, pipeline_mode=pl.Buffered(1)),
            pl.BlockSpec(wa.shape, lambda b, i: (0, 0, 0)),
            pl.BlockSpec(ba.shape, const2),
            pl.BlockSpec(lb.shape, const2),
        ],
        out_specs=out_specs,
        out_shape=out_shape,
        compiler_params=pltpu.CompilerParams(
            dimension_semantics=("parallel", "parallel"), vmem_limit_bytes=VMEM_LIMIT),
        name="in_proj",
    )(x, g.reshape(1, D), shift, scale, w_groups, wa, ba, lb)


def _merge_kernel(o0_ref, o1_ref, o2_ref, o3_ref, gate_ref, wb_ref, wo_ref, x_ref, m_ref, g_ref, sh_ref, sc_ref,
                  out_ref, h_ref, hb_ref):
    D = D_MODEL
    acc = None
    for i, o_ref in enumerate((o0_ref, o1_ref, o2_ref, o3_ref)):
        t = jnp.dot(o_ref[0].astype(BF16), wb_ref[i], preferred_element_type=F32)
        t = (jnp.tanh(gate_ref[0, :, i * D:(i + 1) * D].astype(F32)) + 1.0) * t
        acc = t if acc is None else acc + t
    mx = jnp.dot((0.5 * acc).astype(BF16), wo_ref[...], preferred_element_type=F32)
    out = x_ref[0] + m_ref[0] * mx
    out_ref[0] = out
    ms = jnp.mean(out * out, axis=-1, keepdims=True)
    h = out * lax.rsqrt(ms + EPS) * g_ref[...] * (1.0 + sc_ref[0]) + sh_ref[0]
    h_ref[0] = h
    hb_ref[0] = h.astype(BF16)


def _merge_out(outs, gate_cols, w_branch, w_out, x, m, g, shift, scale, tm):
    B, L, D = x.shape
    tok = pl.BlockSpec((1, tm, D), lambda b, i: (b, i, 0))
    per_sample = pl.BlockSpec((1, 1, D), lambda b, i: (b, 0, 0))
    return pl.pallas_call(
        _merge_kernel,
        grid=(B, L // tm),
        in_specs=[pl.BlockSpec((1, tm, MIX_W), lambda b, i: (b, i, 0)) for _ in range(N_BRANCH)] + [
            pl.BlockSpec((1, tm, N_BRANCH * D), lambda b, i: (b, i, 0)),
            pl.BlockSpec((N_BRANCH, MIX_W, D), lambda b, i: (0, 0, 0)),
            pl.BlockSpec((D, D), lambda b, i: (0, 0)),
            tok, per_sample,
            pl.BlockSpec((1, D), lambda b, i: (0, 0)), per_sample, per_sample,
        ],
        out_specs=[tok, tok, tok],
        out_shape=[jax.ShapeDtypeStruct((B, L, D), F32), jax.ShapeDtypeStruct((B, L, D), F32),
                   jax.ShapeDtypeStruct((B, L, D), BF16)],
        compiler_params=pltpu.CompilerParams(
            dimension_semantics=("parallel", "parallel"), vmem_limit_bytes=VMEM_LIMIT),
        name="merge_out",
    )(*outs, gate_cols, w_branch, w_out, x, m, g.reshape(1, D), shift, scale)


def _expert_ffn_kernel(x_ref, g_ref, w1_ref, w3_ref, w2_ref, out_ref, w1_scr, w3_scr, w2_scr):
    @pl.when(pl.program_id(1) == 0)
    def _():
        w1_scr[...] = w1_ref[0].astype(BF16)
        w3_scr[...] = w3_ref[0].astype(BF16)
        w2_scr[...] = w2_ref[0].astype(BF16)

    x = x_ref[0, 0].astype(BF16)
    a = jnp.dot(x, w1_scr[...], preferred_element_type=F32)
    b = jnp.dot(x, w3_scr[...], preferred_element_type=F32)
    h = (a * jax.nn.sigmoid(a) * b).astype(BF16)
    y = jnp.dot(h, w2_scr[...], preferred_element_type=F32)
    out_ref[0, 0] = (y * g_ref[0, 0]).astype(out_ref.dtype)


def _combine_kernel(idx_ref, y_ref, x_ref, m_ref, out_ref, *, ks):
    tt = out_ref.shape[1]
    S = y_ref.shape[1]
    base = pl.program_id(1) * tt
    acc = None
    for s0 in range(0, S, ks):
        tok = lax.broadcasted_iota(jnp.int32, (tt, ks), 0) + base
        onehot = jnp.where(tok == idx_ref[0, :, s0:s0 + ks], 1.0, 0.0).astype(BF16)
        part = jnp.dot(onehot, y_ref[0, s0:s0 + ks, :], preferred_element_type=F32)
        acc = part if acc is None else acc + part
    out_ref[0] = x_ref[0] + m_ref[0] * acc


def _combine_sorted_kernel(w0_ref, nw_ref, idx_ref, y_ref, x_ref, m_ref, out_ref, acc_scr, iw_scr, yw_scr, *, cap):
    tt = out_ref.shape[1]
    n_exp = y_ref.shape[1] // cap
    b, i = pl.program_id(0), pl.program_id(1)
    base = i * tt

    def entry(e):
        return (b * pl.num_programs(1) + i) * n_exp + e

    for e in range(n_exp):
        s0 = pl.multiple_of(e * cap + w0_ref[entry(e)] * LANES, LANES)
        iw_scr[:, e * CW:(e + 1) * CW] = idx_ref[0, :, pl.ds(s0, CW)]
        yw_scr[e * CW:(e + 1) * CW, :] = y_ref[0, pl.ds(s0, CW), :]
    acc = None
    ks = 8 * CW
    for k0 in range(0, n_exp * CW, ks):
        tok = lax.broadcasted_iota(jnp.int32, (tt, ks), 0) + base
        onehot = jnp.where(tok == iw_scr[:, k0:k0 + ks], 1.0, 0.0).astype(BF16)
        part = jnp.dot(onehot, yw_scr[k0:k0 + ks, :], preferred_element_type=F32)
        acc = part if acc is None else acc + part
    acc_scr[...] = acc
    tok = lax.broadcasted_iota(jnp.int32, (tt, LANES), 0) + base
    for e in range(n_exp):
        def block(w, carry):
            s0 = pl.multiple_of(e * cap + (w0_ref[entry(e)] + CW // LANES + w) * LANES, LANES)
            onehot = jnp.where(tok == idx_ref[0, :, pl.ds(s0, LANES)], 1.0, 0.0).astype(BF16)
            acc_scr[...] += jnp.dot(onehot, y_ref[0, pl.ds(s0, LANES), :], preferred_element_type=F32)
            return carry

        lax.fori_loop(0, nw_ref[entry(e)], block, 0)
    out_ref[0] = x_ref[0] + m_ref[0] * acc_scr[...]


def _combine_sorted(idx, y, x, m, tt):
    B, L, D = x.shape
    E, cap = idx.shape[1:]
    nt = L // tt
    below = jnp.sum(idx[:, :, None, :] < (jnp.arange(nt + 1, dtype=jnp.int32) * tt)[None, None, :, None],
                    axis=-1, dtype=jnp.int32)
    lo, hi = below[..., :-1], below[..., 1:]
    w0 = jnp.minimum(lo // LANES, (cap - CW) // LANES)
    nw = jnp.where(hi > lo, jnp.maximum((hi - 1) // LANES - (w0 + CW // LANES - 1), 0), 0)
    flat = lambda a: a.transpose(0, 2, 1).reshape(-1).astype(jnp.int32)
    grid_spec = pltpu.PrefetchScalarGridSpec(
        num_scalar_prefetch=2,
        grid=(B, nt),
        in_specs=[
            pl.BlockSpec((1, 1, E * cap), lambda b, i, *_: (b, 0, 0)),
            pl.BlockSpec((1, E * cap, D), lambda b, i, *_: (b, 0, 0), pipeline_mode=pl.Buffered(1)),
            pl.BlockSpec((1, tt, D), lambda b, i, *_: (b, i, 0)),
            pl.BlockSpec((1, 1, D), lambda b, i, *_: (b, 0, 0)),
        ],
        out_specs=pl.BlockSpec((1, tt, D), lambda b, i, *_: (b, i, 0)),
        scratch_shapes=[pltpu.VMEM((tt, D), F32), pltpu.VMEM((1, E * CW), jnp.int32),
                        pltpu.VMEM((E * CW, D), BF16)],
    )
    return pl.pallas_call(
        functools.partial(_combine_sorted_kernel, cap=cap),
        grid_spec=grid_spec,
        out_shape=jax.ShapeDtypeStruct((B, L, D), F32),
        compiler_params=pltpu.CompilerParams(
            dimension_semantics=("parallel", "parallel"), vmem_limit_bytes=VMEM_LIMIT),
        name="moe_combine",
    )(flat(w0), flat(nw), idx.reshape(B, 1, E * cap), y, x, m)


def _combine(idx, y, x, m, tt):
    B, L, D = x.shape
    S = y.shape[1]
    return pl.pallas_call(
        functools.partial(_combine_kernel, ks=min(S, 2048)),
        grid=(B, L // tt),
        in_specs=[
            pl.BlockSpec((1, 1, S), lambda b, i: (b, 0, 0)),
            pl.BlockSpec((1, S, D), lambda b, i: (b, 0, 0)),
            pl.BlockSpec((1, tt, D), lambda b, i: (b, i, 0)),
            pl.BlockSpec((1, 1, D), lambda b, i: (b, 0, 0)),
        ],
        out_specs=pl.BlockSpec((1, tt, D), lambda b, i: (b, i, 0)),
        out_shape=jax.ShapeDtypeStruct((B, L, D), F32),
        compiler_params=pltpu.CompilerParams(
            dimension_semantics=("parallel", "parallel"), vmem_limit_bytes=VMEM_LIMIT),
        name="moe_combine",
    )(idx, y, x, m)


def _expert_ffn(xg, g, w1, w3, w2, layer):
    B, E, cap, D = xg.shape
    F = w1.shape[-1]
    return pl.pallas_call(
        _expert_ffn_kernel,
        grid=(E, B),
        in_specs=[
            pl.BlockSpec((1, 1, cap, D), lambda e, b: (b, e, 0, 0)),
            pl.BlockSpec((1, 1, cap, 1), lambda e, b: (b, e, 0, 0)),
            pl.BlockSpec((None, 1, D, F), lambda e, b: (layer, e, 0, 0)),
            pl.BlockSpec((None, 1, D, F), lambda e, b: (layer, e, 0, 0)),
            pl.BlockSpec((None, 1, F, D), lambda e, b: (layer, e, 0, 0)),
        ],
        out_specs=pl.BlockSpec((1, 1, cap, D), lambda e, b: (b, e, 0, 0)),
        out_shape=jax.ShapeDtypeStruct((B, E, cap, D), BF16),
        scratch_shapes=[pltpu.VMEM((D, F), BF16), pltpu.VMEM((D, F), BF16), pltpu.VMEM((F, D), BF16)],
        compiler_params=pltpu.CompilerParams(
            dimension_semantics=("parallel", "arbitrary"), vmem_limit_bytes=VMEM_LIMIT),
        name="expert_ffn",
    )(xg, g, w1, w3, w2)


def _recur_tables(C, rev):
    pos = (C - 1 - np.arange(C)) if rev else np.arange(C)
    row_of = np.argsort(pos)
    levels = int(np.log2(C))
    diff = np.zeros((levels + 1, C, C), np.float32)
    mask = np.zeros((levels + 1, C, C), np.float32)
    for t in range(C):
        p = pos[t]
        diff[0, t, row_of[:p + 1]] = 1.0
        mask[0, t, t] = 1.0
        for l in range(1, levels + 1):
            n, m = 2 ** l, 2 ** (l - 1)
            off = p % n
            mid = p - off + m - 1
            if off >= m:
                diff[l, t, row_of[mid + 1:p + 1]] = 1.0
                mask[l, t, row_of[p - off:mid + 1]] = 1.0
            else:
                diff[l, t, row_of[p + 1:mid + 1]] = 1.0
    return (jnp.asarray(np.tile(diff.reshape((levels + 1) * C, C), (1, 3)), BF16),
            jnp.asarray(np.tile(mask, (1, 1, N_HEADS))))


def _recur_kernel(*refs, rev, C, nsub, dk, dv, finish):
    if finish:
        (q_ref, k_ref, v_ref, g_ref, s0_ref, d_ref, m_ref, prev_ref, gate_ref, gn_ref,
         o_ref, sfin_ref, s_scr, e_scr, a_scr) = refs
    else:
        q_ref, k_ref, v_ref, g_ref, s0_ref, d_ref, m_ref, o_ref, sfin_ref, s_scr, e_scr, a_scr = refs
    HK = N_HEADS * dk
    HV = N_HEADS * dv
    i = pl.program_id(1)
    if finish:
        head_mean = jnp.where(lax.broadcasted_iota(jnp.int32, (HV, HV), 0) // dv
                              == lax.broadcasted_iota(jnp.int32, (HV, HV), 1) // dv, 1.0 / dv, 0.0).astype(BF16)

    @pl.when(i == 0)
    def _():
        s_scr[...] = s0_ref[0]

    head_k = [(lax.broadcasted_iota(jnp.int32, (1, HK), 1) // dk == h).astype(BF16) for h in range(N_HEADS)]
    head_v = [(lax.broadcasted_iota(jnp.int32, (1, HV), 1) // dv == h).astype(BF16) for h in range(N_HEADS)]
    bd = (lax.broadcasted_iota(jnp.int32, (HV, HK), 0) // dv
          == lax.broadcasted_iota(jnp.int32, (HV, HK), 1) // dk)
    levels = m_ref.shape[0] - 1

    def stack_heads(x, head):
        xb = x.astype(BF16)
        return jnp.concatenate([xb * head[h] for h in range(N_HEADS)], axis=0)

    def nt(a, b):
        return lax.dot_general(a, b, (((1,), (1,)), ((), ())), preferred_element_type=F32)

    order = [(nsub - 1 - j) if rev else j for j in range(nsub)]
    for c in order:
        g = g_ref[0, c * C:(c + 1) * C, :]
        g_hi = g.astype(BF16)
        r1 = g - g_hi.astype(F32)
        g_mid = r1.astype(BF16)
        g_lo = (r1 - g_mid.astype(F32)).astype(BF16)
        e_scr[c] = jnp.dot(d_ref[...], jnp.concatenate([g_hi, g_mid, g_lo], axis=0), preferred_element_type=F32)
    for c in order:
        q = q_ref[0, c * C:(c + 1) * C, :]
        k = k_ref[0, c * C:(c + 1) * C, :]
        a = nt(q.astype(BF16), stack_heads(k, head_k)) * m_ref[0]
        for lvl in range(1, levels + 1):
            x = jnp.exp(e_scr[c, lvl * C:(lvl + 1) * C, :])
            a = a + nt((q * x).astype(BF16), stack_heads(k * x, head_k)) * m_ref[lvl]
        a_scr[c] = a.astype(BF16)
    for c in order:
        sl = slice(c * C, (c + 1) * C)
        q = q_ref[0, sl, :]
        k = k_ref[0, sl, :]
        v = v_ref[0, sl, :]
        b = e_scr[c, 0:C, :]
        st = s_scr[...]
        o = jnp.dot(a_scr[c], stack_heads(v, head_v), preferred_element_type=F32)
        o = o + nt((q * jnp.exp(b)).astype(BF16), st.astype(BF16))
        if finish:
            o = o + prev_ref[0, sl, :]
            sq_hi, sq_lo = _split_bf16(o * o)
            ms = (jnp.dot(sq_hi, head_mean, preferred_element_type=F32)
                  + jnp.dot(sq_lo, head_mean, preferred_element_type=F32))
            gate = gate_ref[0, sl, :]
            o = o * lax.rsqrt(ms + EPS) * gn_ref[...] * (gate * jax.nn.sigmoid(gate))
        o_ref[0, sl, :] = o
        b_end = b[0:1, :] if rev else b[C - 1:C, :]
        kend = (k * jnp.exp(b_end - b)).astype(BF16)
        upd = lax.dot_general(v.astype(BF16), kend, (((0,), (0,)), ((), ())), preferred_element_type=F32)
        s_scr[...] = st * jnp.exp(b_end) + jnp.where(bd, upd, 0.0)

    @pl.when(i == pl.num_programs(1) - 1)
    def _():
        sfin_ref[0] = s_scr[...]


def _recurrence(pack, cols, s0, prev, gnorm, *, rev, dk, dv, tb):
    B, L, _ = pack.shape
    HK, HV = N_HEADS * dk, N_HEADS * dv
    C = RC
    nblk = L // tb
    diff, masks = _recur_tables(C, rev)
    cq, ck, cg, cv, cgate = cols

    def tok(col):
        return (lambda b, i: (b, nblk - 1 - i, col)) if rev else (lambda b, i: (b, i, col))

    in_specs = [
        pl.BlockSpec((1, tb, HK), tok(cq)), pl.BlockSpec((1, tb, HK), tok(ck)),
        pl.BlockSpec((1, tb, HV), tok(cv)), pl.BlockSpec((1, tb, HK), tok(cg)),
        pl.BlockSpec((1, HV, HK), lambda b, i: (b, 0, 0)),
        pl.BlockSpec(diff.shape, lambda b, i: (0, 0)),
        pl.BlockSpec(masks.shape, lambda b, i: (0, 0, 0)),
    ]
    args = [pack, pack, pack, pack, s0, diff, masks]
    if prev is not None:
        in_specs += [pl.BlockSpec((1, tb, HV), tok(0)), pl.BlockSpec((1, tb, HV), tok(cgate)),
                     pl.BlockSpec((1, HV), lambda b, i: (0, 0))]
        args += [prev, pack, gnorm]
    kern = functools.partial(_recur_kernel, rev=rev, C=C, nsub=tb // C, dk=dk, dv=dv, finish=prev is not None)
    return pl.pallas_call(
        kern,
        grid=(B, nblk),
        in_specs=in_specs,
        out_specs=[pl.BlockSpec((1, tb, HV), tok(0)), pl.BlockSpec((1, HV, HK), lambda b, i: (b, 0, 0))],
        out_shape=[jax.ShapeDtypeStruct((B, L, HV), F32), jax.ShapeDtypeStruct((B, HV, HK), F32)],
        scratch_shapes=[pltpu.VMEM((HV, HK), F32), pltpu.VMEM((tb // C, diff.shape[0], HK), F32),
                        pltpu.VMEM((tb // C, C, N_HEADS * C), BF16)],
        compiler_params=pltpu.CompilerParams(
            dimension_semantics=("parallel", "arbitrary"), vmem_limit_bytes=VMEM_LIMIT),
        name="recur_rev" if rev else "recur_fwd",
    )(*args)


def _bidir(pack_c, pack_l, cols_f, cols_b, gnorm, *, dk, dv):
    B, Lc, _ = pack_c.shape
    s0 = jnp.zeros((B, N_HEADS * dv, N_HEADS * dk), F32)
    gn = jnp.tile(gnorm.astype(F32), N_HEADS).reshape(1, N_HEADS * dv)
    kw = dict(dk=dk, dv=dv)
    oc_f, sc_f = _recurrence(pack_c, cols_f, s0, None, None, rev=False, tb=Lc, **kw)
    ol_f, _ = _recurrence(pack_l, cols_f, sc_f, None, None, rev=False, tb=512, **kw)
    oc, sc_b = _recurrence(pack_c, cols_b, s0, oc_f, gn, rev=True, tb=Lc, **kw)
    ol, _ = _recurrence(pack_l, cols_b, sc_b, ol_f, gn, rev=True, tb=512, **kw)
    return oc, ol


def _attn_kernel(lam_ref, qt_ref, k_ref, vt_ref, g_ref, o_ref, qm_scr, s_scr, p_scr, m_scr, a_scr, acc_scr, *,
                 first, tk):
    W = qt_ref.shape[1]
    tq = qt_ref.shape[2]
    Lk = k_ref.shape[1]
    qt = qt_ref[0]
    row_pair = lax.broadcasted_iota(jnp.int32, (W, tq), 0) // DA_DK
    for j in range(N_QK):
        qm_scr[j] = jnp.where(row_pair == j, qt, 0.0).astype(BF16)
    m_scr[...] = jnp.full(m_scr.shape, -jnp.inf, F32)
    acc_scr[...] = jnp.zeros(acc_scr.shape, F32)

    def block(start, size):
        ks = pl.ds(start, size)
        kblk = k_ref[0, ks, :]
        for j in range(N_QK):
            s_scr[j, :size] = jnp.dot(kblk, qm_scr[j], preferred_element_type=F32)
        for j in range(N_QK):
            s = s_scr[j, :size]
            m_old = m_scr[j]
            m_new = jnp.maximum(m_old, jnp.max(s, axis=0, keepdims=True))
            a_scr[j] = jnp.exp2(m_old - m_new)
            p_scr[j, :size] = jnp.exp2(s - m_new).astype(BF16)
            m_scr[j] = m_new
        for j in range(N_QK):
            pv = jnp.dot(vt_ref[0, j // 2, :, ks], p_scr[j, :size], preferred_element_type=F32)
            acc_scr[j] = a_scr[j] * acc_scr[j] + pv

    block(0, first)

    def body(i, carry):
        block(pl.multiple_of(first + i * tk, math.gcd(first, tk)), tk)
        return carry

    lax.fori_loop(0, (Lk - first) // tk, body, 0)
    lam = lam_ref[0]
    outs = []
    for h in range(DA_HEADS):
        a1, a2 = acc_scr[2 * h], acc_scr[2 * h + 1]
        o = a1[:DA_DV] / a1[DA_DV:DA_DV + 1] - lam * (a2[:DA_DV] / a2[DA_DV:DA_DV + 1])
        ms = jnp.mean(o * o, axis=0, keepdims=True)
        outs.append(o * lax.rsqrt(ms + EPS) * g_ref[...])
    o_ref[0] = jnp.concatenate(outs, axis=0).T


def _diff_attention(qt, k, vt, lam, gcol, *, tq, first, tk):
    B, W, Lq = qt.shape
    Lk = k.shape[1]
    assert (Lk - first) % tk == 0
    rows = max(first, tk)
    HV = DA_HEADS * DA_DV
    return pl.pallas_call(
        functools.partial(_attn_kernel, first=first, tk=tk),
        grid=(B, Lq // tq),
        in_specs=[
            pl.BlockSpec(memory_space=pltpu.SMEM),
            pl.BlockSpec((1, W, tq), lambda b, i: (b, 0, i)),
            pl.BlockSpec((1, Lk, W), lambda b, i: (b, 0, 0)),
            pl.BlockSpec((1, DA_HEADS, VROWS, Lk), lambda b, i: (b, 0, 0, 0)),
            pl.BlockSpec((DA_DV, 1), lambda b, i: (0, 0)),
        ],
        out_specs=pl.BlockSpec((1, tq, HV), lambda b, i: (b, i, 0)),
        out_shape=jax.ShapeDtypeStruct((B, Lq, HV), F32),
        scratch_shapes=[
            pltpu.VMEM((N_QK, W, tq), BF16),
            pltpu.VMEM((N_QK, rows, tq), F32),
            pltpu.VMEM((N_QK, rows, tq), BF16),
            pltpu.VMEM((N_QK, 1, tq), F32),
            pltpu.VMEM((N_QK, 1, tq), F32),
            pltpu.VMEM((N_QK, VROWS, tq), F32),
        ],
        compiler_params=pltpu.CompilerParams(
            dimension_semantics=("parallel", "parallel"), vmem_limit_bytes=VMEM_LIMIT),
        name="diff_attn",
    )(lam, qt, k, vt, gcol)


def _hyena_kernel(w_ref, bias_ref, uv_ref, u1_ref, u2_ref, hp_ref, o_ref, acc_scr, *, nb, nbatch):
    c = pl.program_id(0)
    half = nb // 2
    rows = nb * nbatch
    lane = lax.broadcasted_iota(jnp.int32, (rows, HB), 1)
    row = lax.broadcasted_iota(jnp.int32, (rows, HB), 0)

    def short_conv(u_ref, ch):
        u = u_ref[:, 0].reshape(rows, HB)
        prev = pltpu.roll(u, 1, axis=1)
        prev = jnp.where(lane == 0, jnp.where(row >= nbatch, pltpu.roll(prev, nbatch % rows, axis=0), 0.0), prev)
        nxt = pltpu.roll(u, HB - 1, axis=1)
        nxt = jnp.where(lane == HB - 1,
                        jnp.where(row < rows - nbatch, pltpu.roll(nxt, (rows - nbatch) % rows, axis=0), 0.0), nxt)
        return w_ref[0, ch] * prev + w_ref[1, ch] * u + w_ref[2, ch] * nxt

    def long_conv(z, o):
        zb = z.astype(BF16)
        hrow = hp_ref[o, 0]
        acc_scr[...] = jnp.zeros(acc_scr.shape, F32)
        for off in range(-half, half + 1):
            lo = (off + half) * HB
            r = jnp.broadcast_to(hrow[:, lo:lo + 2 * HB], (HB, 2 * HB))
            w = pltpu.roll(r, 1, 1, stride=1, stride_axis=0)[:, HB:].astype(BF16)
            s0, s1 = max(0, -off), nb - max(0, off)
            if s1 <= s0:
                continue
            src = slice(s0 * nbatch, s1 * nbatch)
            dst = slice((s0 + off) * nbatch, (s1 + off) * nbatch)
            acc_scr[dst, :] += jnp.dot(zb[src], w, preferred_element_type=F32)
        return acc_scr[...]

    z = short_conv(uv_ref, c)
    for o, u_ref in enumerate((u1_ref, u2_ref)):
        gate = short_conv(u_ref, (o + 1) * HY_WIDTH + c)
        z = gate * (long_conv(z, o) + bias_ref[o, c] * z)
    o_ref[0] = z.reshape(nb, nbatch, HB)


def _hyena_fused(ut, conv_w, hp, bias):
    nb, _, B, _ = ut.shape
    C = HY_WIDTH
    blk = lambda k: pl.BlockSpec((nb, 1, B, HB), lambda c: (0, c + k * C, 0, 0))
    return pl.pallas_call(
        functools.partial(_hyena_kernel, nb=nb, nbatch=B),
        grid=(C,),
        in_specs=[
            pl.BlockSpec(memory_space=pltpu.SMEM),
            pl.BlockSpec(memory_space=pltpu.SMEM),
            blk(0), blk(1), blk(2),
            pl.BlockSpec((HY_ORDER, 1, 1, hp.shape[-1]), lambda c: (0, c, 0, 0)),
        ],
        out_specs=pl.BlockSpec((1, nb, B, HB), lambda c: (c, 0, 0, 0)),
        out_shape=jax.ShapeDtypeStruct((C, nb, B, HB), F32),
        scratch_shapes=[pltpu.VMEM((nb * B, HB), F32)],
        compiler_params=pltpu.CompilerParams(
            dimension_semantics=("parallel",), vmem_limit_bytes=VMEM_LIMIT),
        name="hyena_branch",
    )(conv_w, bias, ut, ut, ut, hp)


def _rms_norm(x, g):
    xf = x.astype(F32)
    y = xf * lax.rsqrt(jnp.mean(xf * xf, axis=-1, keepdims=True) + EPS)
    return (y * g.astype(F32)).astype(x.dtype)


def _gate_weights(wa2, ba):
    qk = GLA_HEADS * GLA_DK
    w = jnp.zeros((LANES, 2 * qk), F32)
    w = w.at[:GLA_RANK, :qk].set(wa2[0].astype(F32)).at[GLA_RANK:2 * GLA_RANK, qk:].set(wa2[1].astype(F32))
    hi = w.astype(BF16)
    lo = (w - hi.astype(F32)).astype(BF16)
    return jnp.stack([hi, lo]), ba.astype(F32).reshape(1, 2 * qk)


def _hyena_filters(L, p):
    j = jnp.arange(L, dtype=F32)
    t = j / max(L - 1, 1)
    w = 2 * math.pi * j / L
    f = jnp.linspace(1e-4, HY_BANDS - 1, HY_BANDS, dtype=F32)
    feats = jnp.concatenate([t[:, None], jnp.cos(w[:, None] * f), -jnp.sin(w[:, None] * f)], axis=-1)
    h = jnp.sin(p['hy_freq'][0] * (feats @ p['hy_w1'] + p['hy_b1']))
    h = jnp.sin(p['hy_freq'][1] * (h @ p['hy_w2'] + p['hy_b2']))
    h = (h @ p['hy_w3']).astype(F32)
    dist = jnp.abs(j - L // 2) / (L // 2)
    h = h * (jnp.exp(-dist[:, None] * jnp.abs(p['hy_decay'].astype(F32))) + HY_SHIFT)
    h = h / jnp.sum(jnp.abs(h), axis=0, keepdims=True)
    return h.reshape(L, HY_ORDER, HY_WIDTH)


def _from_blocks(a):
    C, nb, B, _ = a.shape
    return a.transpose(2, 1, 3, 0).reshape(B, nb * HB, C)


def _pad_filter(h):
    L = h.shape[0]
    nb = L // HB
    left = (HB - 1) - (L // 2 - (nb // 2) * HB)
    return jnp.pad(h.T, ((0, 0), (left, (nb + 2) * HB - L - left)))[:, None, :]


def _hyena_branch(ut3, p):
    nb, C3, BH = ut3.shape
    h = _hyena_filters(nb * HB, p)
    hp = jnp.stack([_pad_filter(h[:, o]) for o in range(HY_ORDER)])
    ut = ut3.reshape(nb, C3, BH // HB, HB)
    return _from_blocks(_hyena_fused(ut, p['hy_conv_w'], hp, p['hy_bias']))


def _rope_tables(L):
    quarter = DA_DK // 4
    freqs = ROPE_BASE ** (-jnp.arange(quarter, dtype=F32) / quarter)
    row = (jnp.arange(L) // GRID_W).astype(F32)[:, None] * freqs
    col = (jnp.arange(L) % GRID_W).astype(F32)[:, None] * freqs
    return jnp.cos(row), jnp.sin(row), jnp.cos(col), jnp.sin(col)


def _da_prep(t, g, tables):
    B, L, W = t.shape
    t = _rms_norm(t.reshape(B, L, N_QK, DA_DK), g)
    if tables is not None:
        cr, sr, cc, sc = [a[None, :, None, :] for a in tables]
        e = DA_DK // 4
        a1, a2, b1, b2 = t[..., :e], t[..., e:2 * e], t[..., 2 * e:3 * e], t[..., 3 * e:]
        t = jnp.concatenate([a1 * cr - a2 * sr, a1 * sr + a2 * cr, b1 * cc - b2 * sc, b1 * sc + b2 * cc], axis=-1)
    return t.reshape(B, L, W)


def _value_rows(v):
    B, Lk, _ = v.shape
    vt = v.reshape(B, Lk, DA_HEADS, DA_DV).transpose(0, 2, 3, 1)
    extra = jnp.concatenate([jnp.ones((B, DA_HEADS, 1, Lk), v.dtype),
                             jnp.zeros((B, DA_HEADS, VROWS - DA_DV - 1, Lk), v.dtype)], axis=2)
    return jnp.concatenate([vt, extra], axis=2).astype(BF16)


def _da_parts(da):
    w = DA_HEADS * 2 * DA_DK
    return da[..., :w], da[..., w:2 * w], da[..., 2 * w:]


def _group_weights(w_in):
    cols = []
    for names, width in PROJ_GROUPS:
        used = 0
        for nm in names:
            i = IN_NAMES.index(nm)
            cols.append(w_in[:, IN_OFFSETS[i]:IN_OFFSETS[i] + IN_WIDTHS[i]])
            used += IN_WIDTHS[i]
        if width > used:
            cols.append(jnp.zeros((w_in.shape[0], width - used), w_in.dtype))
    return jnp.concatenate(cols, axis=1).astype(BF16)


GLA_COLS_F, GLA_COLS_B = (0, 1, 2, 2, 3), (0, 1, 3, 2, 3)
HG_COLS_F, HG_COLS_B = (0, 1, 3, 5, 6), (0, 2, 4, 5, 6)


def _token_mixer(gc, gx, p, lam_init, last):
    gla_c, gla_x = _bidir(gc[0], gx[0], GLA_COLS_F, GLA_COLS_B, p['gla_norm_g'], dk=GLA_DK, dv=GLA_DV)
    hg_c, hg_x = _bidir(gc[2], gx[2], HG_COLS_F, HG_COLS_B, p['hg_norm_g'], dk=HG_DK, dv=HG_DV)
    (dqc, dkc, dvc), (dqx, dkx, dvx) = _da_parts(gc[3]), _da_parts(gx[3])
    tables = _rope_tables(dqx.shape[1])
    scale = DA_DK ** -0.5 * LOG2E
    qx = (_da_prep(dqx, p['da_qnorm_g'], tables) * scale).transpose(0, 2, 1)
    kx = _da_prep(dkx, p['da_knorm_g'], tables)
    kc = _da_prep(dkc, p['da_knorm_g'], None)
    k_all = jnp.concatenate([kc, kx], axis=1).astype(BF16)
    vt_all = _value_rows(jnp.concatenate([dvc, dvx], axis=1))
    lp = p['da_lam'].astype(F32)
    lam = (jnp.exp(jnp.sum(lp[0] * lp[1])) - jnp.exp(jnp.sum(lp[2] * lp[3])) + lam_init).reshape(1)
    gcol = (p['da_norm_g'].astype(F32) * (1 - lam_init)).reshape(DA_DV, 1)
    Lc = kc.shape[1]
    da_x = _diff_attention(qx, k_all, vt_all, lam, gcol, tq=256, first=Lc, tk=512)
    outs_x = (gla_x, _hyena_branch(gx[1], p), hg_x, da_x)
    if last:
        return None, outs_x
    qc = (_da_prep(dqc, p['da_qnorm_g'], None) * scale).transpose(0, 2, 1)
    da_c = _diff_attention(qc, k_all[:, :Lc], vt_all[..., :Lc], lam, gcol, tq=Lc, first=Lc, tk=Lc)
    return (gla_c, _hyena_branch(gc[1], p), hg_c, da_c), outs_x


def _expert_choice_moe(x, m, h, hb, router, w1, w3, w2, layer):
    B, L, D = h.shape
    cap = CAPACITY_FACTOR * L // N_EXPERTS
    aff = jax.nn.softmax((h @ router).astype(F32), axis=-1)
    g, idx = lax.top_k(aff.transpose(0, 2, 1), cap)
    windowed = cap % CW == 0
    if windowed:
        idx, g = lax.sort((idx, g), dimension=-1, num_keys=1)
    xg = jax.vmap(lambda rows, ib: rows[ib])(hb, idx)
    if cap < FFN_MIN_ROWS:
        regroup = lambda t: t.transpose(1, 0, 2, 3).reshape(1, N_EXPERTS, B * cap, t.shape[-1])
        y = _expert_ffn(regroup(xg), regroup(g[..., None]), w1, w3, w2, layer)
        y = y.reshape(N_EXPERTS, B, cap, D).transpose(1, 0, 2, 3)
    else:
        y = _expert_ffn(xg, g[..., None], w1, w3, w2, layer)
    y = y.reshape(B, N_EXPERTS * cap, D)
    if windowed:
        return _combine_sorted(idx, y, x, m, tt=512)
    return _combine(idx.reshape(B, 1, N_EXPERTS * cap), y, x, m, tt=min(L, 512))


def kernel(x, c, ctx, c_ctx, ada_w, ada_b, norm1_g, norm2_g, w_in, gla_wa2, gla_ba, gla_norm_g,
           hy_conv_w, hy_w1, hy_b1, hy_w2, hy_b2, hy_w3, hy_freq, hy_decay, hy_bias,
           hg_lower, hg_norm_g, da_qnorm_g, da_knorm_g, da_lam, da_norm_g, w_branch, w_out,
           moe_router, moe_w1, moe_w3, moe_w2):
    B, L, D = x.shape
    P = jax.nn.softmax(hg_lower.astype(F32), axis=0)
    lower = jnp.cumsum(P, axis=0) - P[0]
    sc = jax.nn.silu(c)
    scc = jax.nn.silu(c_ctx)
    xc, xx = ctx, x
    for l in range(DEPTH):
        last = l == DEPTH - 1
        lam_init = 0.8 - 0.6 * math.exp(-0.3 * l)
        p = {'gla_wa2': gla_wa2[l], 'gla_ba': gla_ba[l], 'gla_norm_g': gla_norm_g[l],
             'hy_conv_w': hy_conv_w[l], 'hy_w1': hy_w1[l], 'hy_b1': hy_b1[l], 'hy_w2': hy_w2[l], 'hy_b2': hy_b2[l],
             'hy_w3': hy_w3[l], 'hy_freq': hy_freq[l], 'hy_decay': hy_decay[l], 'hy_bias': hy_bias[l],
             'hg_norm_g': hg_norm_g[l], 'da_qnorm_g': da_qnorm_g[l], 'da_knorm_g': da_knorm_g[l],
             'da_lam': da_lam[l], 'da_norm_g': da_norm_g[l]}
        mod_x = jnp.split((sc @ ada_w[l] + ada_b[l])[:, None, :], ADA_CHUNKS, axis=-1)
        mod_c1 = jnp.split((scc @ ada_w[l] + ada_b[l])[None, None, :], ADA_CHUNKS, axis=-1)
        mod_c = [jnp.broadcast_to(m, (B, 1, D)) for m in mod_c1]
        wg = _group_weights(w_in[l])
        wa, ba = _gate_weights(gla_wa2[l], gla_ba[l])
        lb = lower[l].reshape(1, 2 * HG_HEADS * HG_DK)
        wb = w_branch[l].astype(BF16)
        wo = w_out[l].astype(BF16)
        gx = _in_proj(xx, norm1_g[l], mod_x[0], mod_x[1], wg, wa, ba, lb, tm=256)
        gc = _in_proj(xc, norm1_g[l], mod_c[0], mod_c[1], wg, wa, ba, lb, tm=256)
        outs_c, outs_x = _token_mixer(gc, gx, p, lam_init, last)
        xx, hx, hxb = _merge_out(outs_x, gx[4], wb, wo, xx, mod_x[2], norm2_g[l], mod_x[3], mod_x[4], tm=512)
        xx = _expert_choice_moe(xx, mod_x[5], hx, hxb, moe_router[l], moe_w1, moe_w3, moe_w2, l)
        if not last:
            xc, hc, hcb = _merge_out(outs_c, gc[4], wb, wo, xc, mod_c[2], norm2_g[l], mod_c[3], mod_c[4], tm=256)
            xc = _expert_choice_moe(xc, mod_c[5], hc, hcb, moe_router[l], moe_w1, moe_w3, moe_w2, l)
    return xx
```

```python
import functools
import math

import jax
import jax.numpy as jnp
import numpy as np
from jax import lax
from jax.experimental import pallas as pl
from jax.experimental.pallas import tpu as pltpu

D_MODEL = 1024
DEPTH = 2
GRID_W = 64
N_BRANCH = 4
MIX_W = 256
GLA_HEADS = 4
GLA_DK = 32
GLA_DV = 64
GLA_RANK = 16
GLA_TAU = 16.0
HY_WIDTH = 256
HY_ORDER = 2
HY_BANDS = 16
HY_SHIFT = 0.05
HG_HEADS = 4
HG_DK = 64
HG_DV = 64
DA_HEADS = 4
DA_DK = 32
DA_DV = 64
ROPE_BASE = 10000.0
N_EXPERTS = 16
EXPERT_FF = 1024
CAPACITY_FACTOR = 2
ADA_CHUNKS = 6
EPS = 1e-6
F_TINY = 1e-20

IN_NAMES = ('gla_q', 'gla_k', 'gla_v', 'gla_af', 'gla_ab', 'gla_g', 'hy',
            'hg_q', 'hg_ff', 'hg_fb', 'hg_i', 'hg_g', 'da_q', 'da_k', 'da_v', 'merge')
IN_WIDTHS = (GLA_HEADS * GLA_DK, GLA_HEADS * GLA_DK, GLA_HEADS * GLA_DV, GLA_RANK, GLA_RANK, GLA_HEADS * GLA_DV,
             (1 + HY_ORDER) * HY_WIDTH,
             HG_HEADS * HG_DK, HG_HEADS * HG_DK, HG_HEADS * HG_DK, HG_HEADS * HG_DV, HG_HEADS * HG_DV,
             DA_HEADS * 2 * DA_DK, DA_HEADS * 2 * DA_DK, DA_HEADS * DA_DV,
             N_BRANCH * D_MODEL)
IN_OFFSETS = tuple(int(v) for v in np.cumsum((0,) + IN_WIDTHS)[:-1])

PROJ_GROUPS = (
    (('gla_q', 'gla_k', 'gla_v', 'gla_g', 'gla_af', 'gla_ab'), 896),
    (('hy',), 768),
    (('hg_q', 'hg_ff', 'hg_fb', 'hg_i', 'hg_g'), 1280),
    (('da_q', 'da_k', 'da_v'), 768),
    (('merge',), 4096),
)
PROJ_WIDTHS = tuple(w for _, w in PROJ_GROUPS)
LANES = 128
OUT_WIDTHS = (4 * GLA_HEADS * GLA_DK + 2 * GLA_HEADS * GLA_DV, PROJ_WIDTHS[1],
              5 * HG_HEADS * HG_DK + 2 * HG_HEADS * HG_DV, PROJ_WIDTHS[3], PROJ_WIDTHS[4])
OUT_DTYPES = (jnp.float32,) * 4 + (jnp.bfloat16,)
VMEM_LIMIT = 56 * 1024 * 1024

BF16 = jnp.bfloat16
F32 = jnp.float32
N_HEADS = 4
RC = 64
N_QK = 2 * DA_HEADS
VROWS = DA_DV + 8
LOG2E = 1.4426950408889634
CW = 256
FFN_MIN_ROWS = 256
HB = 256


def _split_bf16(a):
    hi = a.astype(BF16)
    return hi, (a - hi.astype(F32)).astype(BF16)


def _in_proj_kernel(x_ref, g_ref, sh_ref, sc_ref, w_ref, wa_ref, ba_ref, lb_ref,
                    gla_ref, hy_ref, hg_ref, da_ref, mg_ref):
    x = x_ref[0]
    ms = jnp.mean(x * x, axis=-1, keepdims=True)
    h = x * lax.rsqrt(ms + EPS) * g_ref[...]
    h = (h * (1.0 + sc_ref[0]) + sh_ref[0]).astype(BF16)
    offs = np.cumsum((0,) + PROJ_WIDTHS)

    def proj(k):
        return jnp.dot(h, w_ref[:, int(offs[k]):int(offs[k + 1])], preferred_element_type=F32)

    r = proj(0)
    qk = GLA_HEADS * GLA_DK
    vw = GLA_HEADS * GLA_DV
    gla_ref[0, :, 0:qk] = r[:, 0:qk] * (GLA_DK ** -0.5)
    gla_ref[0, :, qk:2 * qk] = r[:, qk:2 * qk]
    a_hi, a_lo = _split_bf16(r[:, 2 * qk + 2 * vw:])
    z = (jnp.dot(a_hi, wa_ref[0], preferred_element_type=F32) + jnp.dot(a_lo, wa_ref[0], preferred_element_type=F32)
         + jnp.dot(a_hi, wa_ref[1], preferred_element_type=F32) + ba_ref[...])
    gla_ref[0, :, 2 * qk:4 * qk] = jax.nn.log_sigmoid(z) * (1.0 / GLA_TAU)
    gla_ref[0, :, 4 * qk:4 * qk + 2 * vw] = r[:, 2 * qk:2 * qk + 2 * vw]

    hy_ref[0] = proj(1).T

    r = proj(2)
    hw = HG_HEADS * HG_DK
    q = r[:, 0:hw]
    hg_ref[0, :, 0:hw] = q * jax.nn.sigmoid(q)
    zf = r[:, hw:3 * hw]
    lb = lb_ref[...]
    hg_ref[0, :, hw:3 * hw] = (1.0 - lb) * jax.nn.sigmoid(-zf)
    hg_ref[0, :, 3 * hw:5 * hw] = jnp.log(jnp.maximum(lb + (1.0 - lb) * jax.nn.sigmoid(zf), F_TINY))
    hg_ref[0, :, 5 * hw:7 * hw] = r[:, 3 * hw:5 * hw]

    da_ref[0] = proj(3)
    mg_ref[0] = (0.5 * proj(4)).astype(mg_ref.dtype)


def _in_proj(x, g, shift, scale, w_groups, wa, ba, lb, tm):
    B, L, D = x.shape
    nw = w_groups.shape[1]
    const2 = lambda b, i: (0, 0)
    out_specs = [pl.BlockSpec((1, tm, w), lambda b, i: (b, i, 0)) for w in OUT_WIDTHS]
    out_shape = [jax.ShapeDtypeStruct((B, L, w), dt) for w, dt in zip(OUT_WIDTHS, OUT_DTYPES)]
    assert tm == HB
    out_specs[1] = pl.BlockSpec((1, OUT_WIDTHS[1], tm), lambda b, i: (i, 0, b))
    out_shape[1] = jax.ShapeDtypeStruct((L // tm, OUT_WIDTHS[1], B * tm), F32)
    return pl.pallas_call(
        _in_proj_kernel,
        grid=(B, L // tm),
        in_specs=[
            pl.BlockSpec((1, tm, D), lambda b, i: (b, i, 0)),
            pl.BlockSpec((1, D), const2),
            pl.BlockSpec((1, 1, D), lambda b, i: (b, 0, 0)),
            pl.BlockSpec((1, 1, D), lambda b, i: (b, 0, 0)),
            pl.BlockSpec((D, nw), const2, pipeline_mode=pl.Buffered(1)),
            pl.BlockSpec(wa.shape, lambda b, i: (0, 0, 0)),
            pl.BlockSpec(ba.shape, const2),
            pl.BlockSpec(lb.shape, const2),
        ],
        out_specs=out_specs,
        out_shape=out_shape,
        compiler_params=pltpu.CompilerParams(
            dimension_semantics=("parallel", "parallel"), vmem_limit_bytes=VMEM_LIMIT),
        name="in_proj",
    )(x, g.reshape(1, D), shift, scale, w_groups, wa, ba, lb)


def _merge_kernel(o0_ref, o1_ref, o2_ref, o3_ref, gate_ref, wb_ref, wo_ref, x_ref, m_ref, g_ref, sh_ref, sc_ref,
                  out_ref, h_ref, hb_ref):
    D = D_MODEL
    acc = None
    for i, o_ref in enumerate((o0_ref, o1_ref, o2_ref, o3_ref)):
        t = jnp.dot(o_ref[0].astype(BF16), wb_ref[i], preferred_element_type=F32)
        t = (jnp.tanh(gate_ref[0, :, i * D:(i + 1) * D].astype(F32)) + 1.0) * t
        acc = t if acc is None else acc + t
    mx = jnp.dot((0.5 * acc).astype(BF16), wo_ref[...], preferred_element_type=F32)
    out = x_ref[0] + m_ref[0] * mx
    out_ref[0] = out
    ms = jnp.mean(out * out, axis=-1, keepdims=True)
    h = out * lax.rsqrt(ms + EPS) * g_ref[...] * (1.0 + sc_ref[0]) + sh_ref[0]
    h_ref[0] = h
    hb_ref[0] = h.astype(BF16)


def _merge_out(outs, gate_cols, w_branch, w_out, x, m, g, shift, scale, tm):
    B, L, D = x.shape
    tok = pl.BlockSpec((1, tm, D), lambda b, i: (b, i, 0))
    per_sample = pl.BlockSpec((1, 1, D), lambda b, i: (b, 0, 0))
    return pl.pallas_call(
        _merge_kernel,
        grid=(B, L // tm),
        in_specs=[pl.BlockSpec((1, tm, MIX_W), lambda b, i: (b, i, 0)) for _ in range(N_BRANCH)] + [
            pl.BlockSpec((1, tm, N_BRANCH * D), lambda b, i: (b, i, 0)),
            pl.BlockSpec((N_BRANCH, MIX_W, D), lambda b, i: (0, 0, 0)),
            pl.BlockSpec((D, D), lambda b, i: (0, 0)),
            tok, per_sample,
            pl.BlockSpec((1, D), lambda b, i: (0, 0)), per_sample, per_sample,
        ],
        out_specs=[tok, tok, tok],
        out_shape=[jax.ShapeDtypeStruct((B, L, D), F32), jax.ShapeDtypeStruct((B, L, D), F32),
                   jax.ShapeDtypeStruct((B, L, D), BF16)],
        compiler_params=pltpu.CompilerParams(
            dimension_semantics=("parallel", "parallel"), vmem_limit_bytes=VMEM_LIMIT),
        name="merge_out",
    )(*outs, gate_cols, w_branch, w_out, x, m, g.reshape(1, D), shift, scale)


def _expert_ffn_kernel(x_ref, g_ref, w1_ref, w3_ref, w2_ref, out_ref, w1_scr, w3_scr, w2_scr):
    @pl.when(pl.program_id(1) == 0)
    def _():
        w1_scr[...] = w1_ref[0].astype(BF16)
        w3_scr[...] = w3_ref[0].astype(BF16)
        w2_scr[...] = w2_ref[0].astype(BF16)

    x = x_ref[0, 0].astype(BF16)
    a = jnp.dot(x, w1_scr[...], preferred_element_type=F32)
    b = jnp.dot(x, w3_scr[...], preferred_element_type=F32)
    h = (a * jax.nn.sigmoid(a) * b).astype(BF16)
    y = jnp.dot(h, w2_scr[...], preferred_element_type=F32)
    out_ref[0, 0] = (y * g_ref[0, 0]).astype(out_ref.dtype)


def _combine_kernel(idx_ref, y_ref, x_ref, m_ref, out_ref, *, ks):
    tt = out_ref.shape[1]
    S = y_ref.shape[1]
    base = pl.program_id(1) * tt
    acc = None
    for s0 in range(0, S, ks):
        tok = lax.broadcasted_iota(jnp.int32, (tt, ks), 0) + base
        onehot = jnp.where(tok == idx_ref[0, :, s0:s0 + ks], 1.0, 0.0).astype(BF16)
        part = jnp.dot(onehot, y_ref[0, s0:s0 + ks, :], preferred_element_type=F32)
        acc = part if acc is None else acc + part
    out_ref[0] = x_ref[0] + m_ref[0] * acc


def _combine_sorted_kernel(w0_ref, nw_ref, idx_ref, y_ref, x_ref, m_ref, out_ref, acc_scr, iw_scr, yw_scr, *, cap):
    tt = out_ref.shape[1]
    n_exp = y_ref.shape[1] // cap
    b, i = pl.program_id(0), pl.program_id(1)
    base = i * tt

    def entry(e):
        return (b * pl.num_programs(1) + i) * n_exp + e

    for e in range(n_exp):
        s0 = pl.multiple_of(e * cap + w0_ref[entry(e)] * LANES, LANES)
        iw_scr[:, e * CW:(e + 1) * CW] = idx_ref[0, :, pl.ds(s0, CW)]
        yw_scr[e * CW:(e + 1) * CW, :] = y_ref[0, pl.ds(s0, CW), :]
    acc = None
    ks = 8 * CW
    for k0 in range(0, n_exp * CW, ks):
        tok = lax.broadcasted_iota(jnp.int32, (tt, ks), 0) + base
        onehot = jnp.where(tok == iw_scr[:, k0:k0 + ks], 1.0, 0.0).astype(BF16)
        part = jnp.dot(onehot, yw_scr[k0:k0 + ks, :], preferred_element_type=F32)
        acc = part if acc is None else acc + part
    acc_scr[...] = acc
    tok = lax.broadcasted_iota(jnp.int32, (tt, LANES), 0) + base
    for e in range(n_exp):
        def block(w, carry):
            s0 = pl.multiple_of(e * cap + (w0_ref[entry(e)] + CW // LANES + w) * LANES, LANES)
            onehot = jnp.where(tok == idx_ref[0, :, pl.ds(s0, LANES)], 1.0, 0.0).astype(BF16)
            acc_scr[...] += jnp.dot(onehot, y_ref[0, pl.ds(s0, LANES), :], preferred_element_type=F32)
            return carry

        lax.fori_loop(0, nw_ref[entry(e)], block, 0)
    out_ref[0] = x_ref[0] + m_ref[0] * acc_scr[...]


def _combine_sorted(idx, y, x, m, tt):
    B, L, D = x.shape
    E, cap = idx.shape[1:]
    nt = L // tt
    below = jnp.sum(idx[:, :, None, :] < (jnp.arange(nt + 1, dtype=jnp.int32) * tt)[None, None, :, None],
                    axis=-1, dtype=jnp.int32)
    lo, hi = below[..., :-1], below[..., 1:]
    w0 = jnp.minimum(lo // LANES, (cap - CW) // LANES)
    nw = jnp.where(hi > lo, jnp.maximum((hi - 1) // LANES - (w0 + CW // LANES - 1), 0), 0)
    flat = lambda a: a.transpose(0, 2, 1).reshape(-1).astype(jnp.int32)
    grid_spec = pltpu.PrefetchScalarGridSpec(
        num_scalar_prefetch=2,
        grid=(B, nt),
        in_specs=[
            pl.BlockSpec((1, 1, E * cap), lambda b, i, *_: (b, 0, 0)),
            pl.BlockSpec((1, E * cap, D), lambda b, i, *_: (b, 0, 0), pipeline_mode=pl.Buffered(1)),
            pl.BlockSpec((1, tt, D), lambda b, i, *_: (b, i, 0)),
            pl.BlockSpec((1, 1, D), lambda b, i, *_: (b, 0, 0)),
        ],
        out_specs=pl.BlockSpec((1, tt, D), lambda b, i, *_: (b, i, 0)),
        scratch_shapes=[pltpu.VMEM((tt, D), F32), pltpu.VMEM((1, E * CW), jnp.int32),
                        pltpu.VMEM((E * CW, D), BF16)],
    )
    return pl.pallas_call(
        functools.partial(_combine_sorted_kernel, cap=cap),
        grid_spec=grid_spec,
        out_shape=jax.ShapeDtypeStruct((B, L, D), F32),
        compiler_params=pltpu.CompilerParams(
            dimension_semantics=("parallel", "parallel"), vmem_limit_bytes=VMEM_LIMIT),
        name="moe_combine",
    )(flat(w0), flat(nw), idx.reshape(B, 1, E * cap), y, x, m)


def _combine(idx, y, x, m, tt):
    B, L, D = x.shape
    S = y.shape[1]
    return pl.pallas_call(
        functools.partial(_combine_kernel, ks=min(S, 2048)),
        grid=(B, L // tt),
        in_specs=[
            pl.BlockSpec((1, 1, S), lambda b, i: (b, 0, 0)),
            pl.BlockSpec((1, S, D), lambda b, i: (b, 0, 0)),
            pl.BlockSpec((1, tt, D), lambda b, i: (b, i, 0)),
            pl.BlockSpec((1, 1, D), lambda b, i: (b, 0, 0)),
        ],
        out_specs=pl.BlockSpec((1, tt, D), lambda b, i: (b, i, 0)),
        out_shape=jax.ShapeDtypeStruct((B, L, D), F32),
        compiler_params=pltpu.CompilerParams(
            dimension_semantics=("parallel", "parallel"), vmem_limit_bytes=VMEM_LIMIT),
        name="moe_combine",
    )(idx, y, x, m)


def _expert_ffn(xg, g, w1, w3, w2, layer):
    B, E, cap, D = xg.shape
    F = w1.shape[-1]
    return pl.pallas_call(
        _expert_ffn_kernel,
        grid=(E, B),
        in_specs=[
            pl.BlockSpec((1, 1, cap, D), lambda e, b: (b, e, 0, 0)),
            pl.BlockSpec((1, 1, cap, 1), lambda e, b: (b, e, 0, 0)),
            pl.BlockSpec((None, 1, D, F), lambda e, b: (layer, e, 0, 0)),
            pl.BlockSpec((None, 1, D, F), lambda e, b: (layer, e, 0, 0)),
            pl.BlockSpec((None, 1, F, D), lambda e, b: (layer, e, 0, 0)),
        ],
        out_specs=pl.BlockSpec((1, 1, cap, D), lambda e, b: (b, e, 0, 0)),
        out_shape=jax.ShapeDtypeStruct((B, E, cap, D), BF16),
        scratch_shapes=[pltpu.VMEM((D, F), BF16), pltpu.VMEM((D, F), BF16), pltpu.VMEM((F, D), BF16)],
        compiler_params=pltpu.CompilerParams(
            dimension_semantics=("parallel", "arbitrary"), vmem_limit_bytes=VMEM_LIMIT),
        name="expert_ffn",
    )(xg, g, w1, w3, w2)


def _recur_tables(C, rev):
    pos = (C - 1 - np.arange(C)) if rev else np.arange(C)
    row_of = np.argsort(pos)
    levels = int(np.log2(C))
    diff = np.zeros((levels + 1, C, C), np.float32)
    mask = np.zeros((levels + 1, C, C), np.float32)
    for t in range(C):
        p = pos[t]
        diff[0, t, row_of[:p + 1]] = 1.0
        mask[0, t, t] = 1.0
        for l in range(1, levels + 1):
            n, m = 2 ** l, 2 ** (l - 1)
            off = p % n
            mid = p - off + m - 1
            if off >= m:
                diff[l, t, row_of[mid + 1:p + 1]] = 1.0
                mask[l, t, row_of[p - off:mid + 1]] = 1.0
            else:
                diff[l, t, row_of[p + 1:mid + 1]] = 1.0
    return (jnp.asarray(np.tile(diff.reshape((levels + 1) * C, C), (1, 3)), BF16),
            jnp.asarray(np.tile(mask, (1, 1, N_HEADS))))


def _recur_kernel(*refs, rev, C, nsub, dk, dv, finish):
    if finish:
        (q_ref, k_ref, v_ref, g_ref, s0_ref, d_ref, m_ref, prev_ref, gate_ref, gn_ref,
         o_ref, sfin_ref, s_scr, e_scr, a_scr) = refs
    else:
        q_ref, k_ref, v_ref, g_ref, s0_ref, d_ref, m_ref, o_ref, sfin_ref, s_scr, e_scr, a_scr = refs
    HK = N_HEADS * dk
    HV = N_HEADS * dv
    i = pl.program_id(1)
    if finish:
        head_mean = jnp.where(lax.broadcasted_iota(jnp.int32, (HV, HV), 0) // dv
                              == lax.broadcasted_iota(jnp.int32, (HV, HV), 1) // dv, 1.0 / dv, 0.0).astype(BF16)

    @pl.when(i == 0)
    def _():
        s_scr[...] = s0_ref[0]

    head_k = [(lax.broadcasted_iota(jnp.int32, (1, HK), 1) // dk == h).astype(BF16) for h in range(N_HEADS)]
    head_v = [(lax.broadcasted_iota(jnp.int32, (1, HV), 1) // dv == h).astype(BF16) for h in range(N_HEADS)]
    bd = (lax.broadcasted_iota(jnp.int32, (HV, HK), 0) // dv
          == lax.broadcasted_iota(jnp.int32, (HV, HK), 1) // dk)
    levels = m_ref.shape[0] - 1

    def stack_heads(x, head):
        xb = x.astype(BF16)
        return jnp.concatenate([xb * head[h] for h in range(N_HEADS)], axis=0)

    def nt(a, b):
        return lax.dot_general(a, b, (((1,), (1,)), ((), ())), preferred_element_type=F32)

    order = [(nsub - 1 - j) if rev else j for j in range(nsub)]
    for c in order:
        g = g_ref[0, c * C:(c + 1) * C, :]
        g_hi = g.astype(BF16)
        r1 = g - g_hi.astype(F32)
        g_mid = r1.astype(BF16)
        g_lo = (r1 - g_mid.astype(F32)).astype(BF16)
        e_scr[c] = jnp.dot(d_ref[...], jnp.concatenate([g_hi, g_mid, g_lo], axis=0), preferred_element_type=F32)
    for c in order:
        q = q_ref[0, c * C:(c + 1) * C, :]
        k = k_ref[0, c * C:(c + 1) * C, :]
        a = nt(q.astype(BF16), stack_heads(k, head_k)) * m_ref[0]
        for lvl in range(1, levels + 1):
            x = jnp.exp(e_scr[c, lvl * C:(lvl + 1) * C, :])
            a = a + nt((q * x).astype(BF16), stack_heads(k * x, head_k)) * m_ref[lvl]
        a_scr[c] = a.astype(BF16)
    for c in order:
        sl = slice(c * C, (c + 1) * C)
        q = q_ref[0, sl, :]
        k = k_ref[0, sl, :]
        v = v_ref[0, sl, :]
        b = e_scr[c, 0:C, :]
        st = s_scr[...]
        o = jnp.dot(a_scr[c], stack_heads(v, head_v), preferred_element_type=F32)
        o = o + nt((q * jnp.exp(b)).astype(BF16), st.astype(BF16))
        o_ref[0, sl, :] = o
        b_end = b[0:1, :] if rev else b[C - 1:C, :]
        kend = (k * jnp.exp(b_end - b)).astype(BF16)
        upd = lax.dot_general(v.astype(BF16), kend, (((0,), (0,)), ((), ())), preferred_element_type=F32)
        s_scr[...] = st * jnp.exp(b_end) + jnp.where(bd, upd, 0.0)
    if finish:
        o = o_ref[0] + prev_ref[0]
        sq_hi, sq_lo = _split_bf16(o * o)
        ms = (jnp.dot(sq_hi, head_mean, preferred_element_type=F32)
              + jnp.dot(sq_lo, head_mean, preferred_element_type=F32))
        gate = gate_ref[0]
        o_ref[0] = o * lax.rsqrt(ms + EPS) * gn_ref[...] * (gate * jax.nn.sigmoid(gate))

    @pl.when(i == pl.num_programs(1) - 1)
    def _():
        sfin_ref[0] = s_scr[...]


def _recurrence(pack, cols, s0, prev, gnorm, *, rev, dk, dv, tb):
    B, L, _ = pack.shape
    HK, HV = N_HEADS * dk, N_HEADS * dv
    C = RC
    nblk = L // tb
    diff, masks = _recur_tables(C, rev)
    cq, ck, cg, cv, cgate = cols

    def tok(col):
        return (lambda b, i: (b, nblk - 1 - i, col)) if rev else (lambda b, i: (b, i, col))

    in_specs = [
        pl.BlockSpec((1, tb, HK), tok(cq)), pl.BlockSpec((1, tb, HK), tok(ck)),
        pl.BlockSpec((1, tb, HV), tok(cv)), pl.BlockSpec((1, tb, HK), tok(cg)),
        pl.BlockSpec((1, HV, HK), lambda b, i: (b, 0, 0)),
        pl.BlockSpec(diff.shape, lambda b, i: (0, 0)),
        pl.BlockSpec(masks.shape, lambda b, i: (0, 0, 0)),
    ]
    args = [pack, pack, pack, pack, s0, diff, masks]
    if prev is not None:
        in_specs += [pl.BlockSpec((1, tb, HV), tok(0)), pl.BlockSpec((1, tb, HV), tok(cgate)),
                     pl.BlockSpec((1, HV), lambda b, i: (0, 0))]
        args += [prev, pack, gnorm]
    kern = functools.partial(_recur_kernel, rev=rev, C=C, nsub=tb // C, dk=dk, dv=dv, finish=prev is not None)
    return pl.pallas_call(
        kern,
        grid=(B, nblk),
        in_specs=in_specs,
        out_specs=[pl.BlockSpec((1, tb, HV), tok(0)), pl.BlockSpec((1, HV, HK), lambda b, i: (b, 0, 0))],
        out_shape=[jax.ShapeDtypeStruct((B, L, HV), F32), jax.ShapeDtypeStruct((B, HV, HK), F32)],
        scratch_shapes=[pltpu.VMEM((HV, HK), F32), pltpu.VMEM((tb // C, diff.shape[0], HK), F32),
                        pltpu.VMEM((tb // C, C, N_HEADS * C), BF16)],
        compiler_params=pltpu.CompilerParams(
            dimension_semantics=("parallel", "arbitrary"), vmem_limit_bytes=VMEM_LIMIT),
        name="recur_rev" if rev else "recur_fwd",
    )(*args)


def _bidir(pack_c, pack_l, cols_f, cols_b, gnorm, *, dk, dv):
    B, Lc, _ = pack_c.shape
    s0 = jnp.zeros((B, N_HEADS * dv, N_HEADS * dk), F32)
    gn = jnp.tile(gnorm.astype(F32), N_HEADS).reshape(1, N_HEADS * dv)
    kw = dict(dk=dk, dv=dv)
    oc_f, sc_f = _recurrence(pack_c, cols_f, s0, None, None, rev=False, tb=Lc, **kw)
    ol_f, _ = _recurrence(pack_l, cols_f, sc_f, None, None, rev=False, tb=512, **kw)
    oc, sc_b = _recurrence(pack_c, cols_b, s0, oc_f, gn, rev=True, tb=Lc, **kw)
    ol, _ = _recurrence(pack_l, cols_b, sc_b, ol_f, gn, rev=True, tb=512, **kw)
    return oc, ol


def _attn_kernel(lam_ref, qt_ref, k_ref, vt_ref, g_ref, o_ref, qm_scr, s_scr, p_scr, m_scr, a_scr, acc_scr, *,
                 first, tk):
    W = qt_ref.shape[1]
    tq = qt_ref.shape[2]
    Lk = k_ref.shape[1]
    qt = qt_ref[0]
    row_pair = lax.broadcasted_iota(jnp.int32, (W, tq), 0) // DA_DK
    for j in range(N_QK):
        qm_scr[j] = jnp.where(row_pair == j, qt, 0.0).astype(BF16)
    m_scr[...] = jnp.full(m_scr.shape, -jnp.inf, F32)
    acc_scr[...] = jnp.zeros(acc_scr.shape, F32)

    def block(start, size):
        ks = pl.ds(start, size)
        kblk = k_ref[0, ks, :]
        for j in range(N_QK):
            s_scr[j, :size] = jnp.dot(kblk, qm_scr[j], preferred_element_type=F32)
        for j in range(N_QK):
            s = s_scr[j, :size]
            m_old = m_scr[j]
            m_new = jnp.maximum(m_old, jnp.max(s, axis=0, keepdims=True))
            a_scr[j] = jnp.exp2(m_old - m_new)
            p_scr[j, :size] = jnp.exp2(s - m_new).astype(BF16)
            m_scr[j] = m_new
        for j in range(N_QK):
            pv = jnp.dot(vt_ref[0, j // 2, :, ks], p_scr[j, :size], preferred_element_type=F32)
            acc_scr[j] = a_scr[j] * acc_scr[j] + pv

    block(0, first)

    def body(i, carry):
        block(pl.multiple_of(first + i * tk, math.gcd(first, tk)), tk)
        return carry

    lax.fori_loop(0, (Lk - first) // tk, body, 0)
    lam = lam_ref[0]
    outs = []
    for h in range(DA_HEADS):
        a1, a2 = acc_scr[2 * h], acc_scr[2 * h + 1]
        o = a1[:DA_DV] / a1[DA_DV:DA_DV + 1] - lam * (a2[:DA_DV] / a2[DA_DV:DA_DV + 1])
        ms = jnp.mean(o * o, axis=0, keepdims=True)
        outs.append(o * lax.rsqrt(ms + EPS) * g_ref[...])
    o_ref[0] = jnp.concatenate(outs, axis=0).T


def _diff_attention(qt, k, vt, lam, gcol, *, tq, first, tk):
    B, W, Lq = qt.shape
    Lk = k.shape[1]
    assert (Lk - first) % tk == 0
    rows = max(first, tk)
    HV = DA_HEADS * DA_DV
    return pl.pallas_call(
        functools.partial(_attn_kernel, first=first, tk=tk),
        grid=(B, Lq // tq),
        in_specs=[
            pl.BlockSpec(memory_space=pltpu.SMEM),
            pl.BlockSpec((1, W, tq), lambda b, i: (b, 0, i)),
            pl.BlockSpec((1, Lk, W), lambda b, i: (b, 0, 0)),
            pl.BlockSpec((1, DA_HEADS, VROWS, Lk), lambda b, i: (b, 0, 0, 0)),
            pl.BlockSpec((DA_DV, 1), lambda b, i: (0, 0)),
        ],
        out_specs=pl.BlockSpec((1, tq, HV), lambda b, i: (b, i, 0)),
        out_shape=jax.ShapeDtypeStruct((B, Lq, HV), F32),
        scratch_shapes=[
            pltpu.VMEM((N_QK, W, tq), BF16),
            pltpu.VMEM((N_QK, rows, tq), F32),
            pltpu.VMEM((N_QK, rows, tq), BF16),
            pltpu.VMEM((N_QK, 1, tq), F32),
            pltpu.VMEM((N_QK, 1, tq), F32),
            pltpu.VMEM((N_QK, VROWS, tq), F32),
        ],
        compiler_params=pltpu.CompilerParams(
            dimension_semantics=("parallel", "parallel"), vmem_limit_bytes=VMEM_LIMIT),
        name="diff_attn",
    )(lam, qt, k, vt, gcol)


def _hyena_kernel(w_ref, bias_ref, uv_ref, u1_ref, u2_ref, hp_ref, o_ref, acc_scr, *, nb, nbatch):
    c = pl.program_id(0)
    half = nb // 2
    rows = nb * nbatch
    lane = lax.broadcasted_iota(jnp.int32, (rows, HB), 1)
    row = lax.broadcasted_iota(jnp.int32, (rows, HB), 0)

    def short_conv(u_ref, ch):
        u = u_ref[:, 0].reshape(rows, HB)
        prev = pltpu.roll(u, 1, axis=1)
        prev = jnp.where(lane == 0, jnp.where(row >= nbatch, pltpu.roll(prev, nbatch % rows, axis=0), 0.0), prev)
        nxt = pltpu.roll(u, HB - 1, axis=1)
        nxt = jnp.where(lane == HB - 1,
                        jnp.where(row < rows - nbatch, pltpu.roll(nxt, (rows - nbatch) % rows, axis=0), 0.0), nxt)
        return w_ref[0, ch] * prev + w_ref[1, ch] * u + w_ref[2, ch] * nxt

    def long_conv(z, o):
        zb = z.astype(BF16)
        hrow = hp_ref[o, 0]
        acc_scr[...] = jnp.zeros(acc_scr.shape, F32)
        for off in range(-half, half + 1):
            lo = (off + half) * HB
            r = jnp.broadcast_to(hrow[:, lo:lo + 2 * HB], (HB, 2 * HB))
            w = pltpu.roll(r, 1, 1, stride=1, stride_axis=0)[:, HB:].astype(BF16)
            s0, s1 = max(0, -off), nb - max(0, off)
            if s1 <= s0:
                continue
            src = slice(s0 * nbatch, s1 * nbatch)
            dst = slice((s0 + off) * nbatch, (s1 + off) * nbatch)
            acc_scr[dst, :] += jnp.dot(zb[src], w, preferred_element_type=F32)
        return acc_scr[...]

    z = short_conv(uv_ref, c)
    for o, u_ref in enumerate((u1_ref, u2_ref)):
        gate = short_conv(u_ref, (o + 1) * HY_WIDTH + c)
        z = gate * (long_conv(z, o) + bias_ref[o, c] * z)
    o_ref[0] = z.reshape(nb, nbatch, HB)


def _hyena_fused(ut, conv_w, hp, bias):
    nb, _, B, _ = ut.shape
    C = HY_WIDTH
    blk = lambda k: pl.BlockSpec((nb, 1, B, HB), lambda c: (0, c + k * C, 0, 0))
    return pl.pallas_call(
        functools.partial(_hyena_kernel, nb=nb, nbatch=B),
        grid=(C,),
        in_specs=[
            pl.BlockSpec(memory_space=pltpu.SMEM),
            pl.BlockSpec(memory_space=pltpu.SMEM),
            blk(0), blk(1), blk(2),
            pl.BlockSpec((HY_ORDER, 1, 1, hp.shape[-1]), lambda c: (0, c, 0, 0)),
        ],
        out_specs=pl.BlockSpec((1, nb, B, HB), lambda c: (c, 0, 0, 0)),
        out_shape=jax.ShapeDtypeStruct((C, nb, B, HB), F32),
        scratch_shapes=[pltpu.VMEM((nb * B, HB), F32)],
        compiler_params=pltpu.CompilerParams(
            dimension_semantics=("parallel",), vmem_limit_bytes=VMEM_LIMIT),
        name="hyena_branch",
    )(conv_w, bias, ut, ut, ut, hp)


def _rms_norm(x, g):
    xf = x.astype(F32)
    y = xf * lax.rsqrt(jnp.mean(xf * xf, axis=-1, keepdims=True) + EPS)
    return (y * g.astype(F32)).astype(x.dtype)


def _gate_weights(wa2, ba):
    qk = GLA_HEADS * GLA_DK
    w = jnp.zeros((LANES, 2 * qk), F32)
    w = w.at[:GLA_RANK, :qk].set(wa2[0].astype(F32)).at[GLA_RANK:2 * GLA_RANK, qk:].set(wa2[1].astype(F32))
    hi = w.astype(BF16)
    lo = (w - hi.astype(F32)).astype(BF16)
    return jnp.stack([hi, lo]), ba.astype(F32).reshape(1, 2 * qk)


def _hyena_filters(L, p):
    j = jnp.arange(L, dtype=F32)
    t = j / max(L - 1, 1)
    w = 2 * math.pi * j / L
    f = jnp.linspace(1e-4, HY_BANDS - 1, HY_BANDS, dtype=F32)
    feats = jnp.concatenate([t[:, None], jnp.cos(w[:, None] * f), -jnp.sin(w[:, None] * f)], axis=-1)
    h = jnp.sin(p['hy_freq'][0] * (feats @ p['hy_w1'] + p['hy_b1']))
    h = jnp.sin(p['hy_freq'][1] * (h @ p['hy_w2'] + p['hy_b2']))
    h = (h @ p['hy_w3']).astype(F32)
    dist = jnp.abs(j - L // 2) / (L // 2)
    h = h * (jnp.exp(-dist[:, None] * jnp.abs(p['hy_decay'].astype(F32))) + HY_SHIFT)
    h = h / jnp.sum(jnp.abs(h), axis=0, keepdims=True)
    return h.reshape(L, HY_ORDER, HY_WIDTH)


def _from_blocks(a):
    C, nb, B, _ = a.shape
    return a.transpose(2, 1, 3, 0).reshape(B, nb * HB, C)


def _pad_filter(h):
    L = h.shape[0]
    nb = L // HB
    left = (HB - 1) - (L // 2 - (nb // 2) * HB)
    return jnp.pad(h.T, ((0, 0), (left, (nb + 2) * HB - L - left)))[:, None, :]


def _hyena_branch(ut3, p):
    nb, C3, BH = ut3.shape
    h = _hyena_filters(nb * HB, p)
    hp = jnp.stack([_pad_filter(h[:, o]) for o in range(HY_ORDER)])
    ut = ut3.reshape(nb, C3, BH // HB, HB)
    return _from_blocks(_hyena_fused(ut, p['hy_conv_w'], hp, p['hy_bias']))


def _rope_tables(L):
    quarter = DA_DK // 4
    freqs = ROPE_BASE ** (-jnp.arange(quarter, dtype=F32) / quarter)
    row = (jnp.arange(L) // GRID_W).astype(F32)[:, None] * freqs
    col = (jnp.arange(L) % GRID_W).astype(F32)[:, None] * freqs
    return jnp.cos(row), jnp.sin(row), jnp.cos(col), jnp.sin(col)


def _da_prep(t, g, tables):
    B, L, W = t.shape
    t = _rms_norm(t.reshape(B, L, N_QK, DA_DK), g)
    if tables is not None:
        cr, sr, cc, sc = [a[None, :, None, :] for a in tables]
        e = DA_DK // 4
        a1, a2, b1, b2 = t[..., :e], t[..., e:2 * e], t[..., 2 * e:3 * e], t[..., 3 * e:]
        t = jnp.concatenate([a1 * cr - a2 * sr, a1 * sr + a2 * cr, b1 * cc - b2 * sc, b1 * sc + b2 * cc], axis=-1)
    return t.reshape(B, L, W)


def _value_rows(v):
    B, Lk, _ = v.shape
    vt = v.reshape(B, Lk, DA_HEADS, DA_DV).transpose(0, 2, 3, 1)
    extra = jnp.concatenate([jnp.ones((B, DA_HEADS, 1, Lk), v.dtype),
                             jnp.zeros((B, DA_HEADS, VROWS - DA_DV - 1, Lk), v.dtype)], axis=2)
    return jnp.concatenate([vt, extra], axis=2).astype(BF16)


def _da_parts(da):
    w = DA_HEADS * 2 * DA_DK
    return da[..., :w], da[..., w:2 * w], da[..., 2 * w:]


def _group_weights(w_in):
    cols = []
    for names, width in PROJ_GROUPS:
        used = 0
        for nm in names:
            i = IN_NAMES.index(nm)
            cols.append(w_in[:, IN_OFFSETS[i]:IN_OFFSETS[i] + IN_WIDTHS[i]])
            used += IN_WIDTHS[i]
        if width > used:
            cols.append(jnp.zeros((w_in.shape[0], width - used), w_in.dtype))
    return jnp.concatenate(cols, axis=1).astype(BF16)


GLA_COLS_F, GLA_COLS_B = (0, 1, 2, 2, 3), (0, 1, 3, 2, 3)
HG_COLS_F, HG_COLS_B = (0, 1, 3, 5, 6), (0, 2, 4, 5, 6)


def _token_mixer(gc, gx, p, lam_init, last):
    gla_c, gla_x = _bidir(gc[0], gx[0], GLA_COLS_F, GLA_COLS_B, p['gla_norm_g'], dk=GLA_DK, dv=GLA_DV)
    hg_c, hg_x = _bidir(gc[2], gx[2], HG_COLS_F, HG_COLS_B, p['hg_norm_g'], dk=HG_DK, dv=HG_DV)
    (dqc, dkc, dvc), (dqx, dkx, dvx) = _da_parts(gc[3]), _da_parts(gx[3])
    tables = _rope_tables(dqx.shape[1])
    scale = DA_DK ** -0.5 * LOG2E
    qx = (_da_prep(dqx, p['da_qnorm_g'], tables) * scale).transpose(0, 2, 1)
    kx = _da_prep(dkx, p['da_knorm_g'], tables)
    kc = _da_prep(dkc, p['da_knorm_g'], None)
    k_all = jnp.concatenate([kc, kx], axis=1).astype(BF16)
    vt_all = _value_rows(jnp.concatenate([dvc, dvx], axis=1))
    lp = p['da_lam'].astype(F32)
    lam = (jnp.exp(jnp.sum(lp[0] * lp[1])) - jnp.exp(jnp.sum(lp[2] * lp[3])) + lam_init).reshape(1)
    gcol = (p['da_norm_g'].astype(F32) * (1 - lam_init)).reshape(DA_DV, 1)
    Lc = kc.shape[1]
    da_x = _diff_attention(qx, k_all, vt_all, lam, gcol, tq=256, first=Lc, tk=512)
    outs_x = (gla_x, _hyena_branch(gx[1], p), hg_x, da_x)
    if last:
        return None, outs_x
    qc = (_da_prep(dqc, p['da_qnorm_g'], None) * scale).transpose(0, 2, 1)
    da_c = _diff_attention(qc, k_all[:, :Lc], vt_all[..., :Lc], lam, gcol, tq=Lc, first=Lc, tk=Lc)
    return (gla_c, _hyena_branch(gc[1], p), hg_c, da_c), outs_x


def _expert_choice_moe(x, m, h, hb, router, w1, w3, w2, layer):
    B, L, D = h.shape
    cap = CAPACITY_FACTOR * L // N_EXPERTS
    aff = jax.nn.softmax((h @ router).astype(F32), axis=-1)
    g, idx = lax.top_k(aff.transpose(0, 2, 1), cap)
    windowed = cap % CW == 0
    if windowed:
        idx, g = lax.sort((idx, g), dimension=-1, num_keys=1)
    xg = jax.vmap(lambda rows, ib: rows[ib])(hb, idx)
    if cap < FFN_MIN_ROWS:
        regroup = lambda t: t.transpose(1, 0, 2, 3).reshape(1, N_EXPERTS, B * cap, t.shape[-1])
        y = _expert_ffn(regroup(xg), regroup(g[..., None]), w1, w3, w2, layer)
        y = y.reshape(N_EXPERTS, B, cap, D).transpose(1, 0, 2, 3)
    else:
        y = _expert_ffn(xg, g[..., None], w1, w3, w2, layer)
    y = y.reshape(B, N_EXPERTS * cap, D)
    if windowed:
        return _combine_sorted(idx, y, x, m, tt=512)
    return _combine(idx.reshape(B, 1, N_EXPERTS * cap), y, x, m, tt=min(L, 512))


def kernel(x, c, ctx, c_ctx, ada_w, ada_b, norm1_g, norm2_g, w_in, gla_wa2, gla_ba, gla_norm_g,
           hy_conv_w, hy_w1, hy_b1, hy_w2, hy_b2, hy_w3, hy_freq, hy_decay, hy_bias,
           hg_lower, hg_norm_g, da_qnorm_g, da_knorm_g, da_lam, da_norm_g, w_branch, w_out,
           moe_router, moe_w1, moe_w3, moe_w2):
    B, L, D = x.shape
    P = jax.nn.softmax(hg_lower.astype(F32), axis=0)
    lower = jnp.cumsum(P, axis=0) - P[0]
    sc = jax.nn.silu(c)
    scc = jax.nn.silu(c_ctx)
    xc, xx = ctx, x
    for l in range(DEPTH):
        last = l == DEPTH - 1
        lam_init = 0.8 - 0.6 * math.exp(-0.3 * l)
        p = {'gla_wa2': gla_wa2[l], 'gla_ba': gla_ba[l], 'gla_norm_g': gla_norm_g[l],
             'hy_conv_w': hy_conv_w[l], 'hy_w1': hy_w1[l], 'hy_b1': hy_b1[l], 'hy_w2': hy_w2[l], 'hy_b2': hy_b2[l],
             'hy_w3': hy_w3[l], 'hy_freq': hy_freq[l], 'hy_decay': hy_decay[l], 'hy_bias': hy_bias[l],
             'hg_norm_g': hg_norm_g[l], 'da_qnorm_g': da_qnorm_g[l], 'da_knorm_g': da_knorm_g[l],
             'da_lam': da_lam[l], 'da_norm_g': da_norm_g[l]}
        mod_x = jnp.split((sc @ ada_w[l] + ada_b[l])[:, None, :], ADA_CHUNKS, axis=-1)
        mod_c1 = jnp.split((scc @ ada_w[l] + ada_b[l])[None, None, :], ADA_CHUNKS, axis=-1)
        mod_c = [jnp.broadcast_to(m, (B, 1, D)) for m in mod_c1]
        wg = _group_weights(w_in[l])
        wa, ba = _gate_weights(gla_wa2[l], gla_ba[l])
        lb = lower[l].reshape(1, 2 * HG_HEADS * HG_DK)
        wb = w_branch[l].astype(BF16)
        wo = w_out[l].astype(BF16)
        gx = _in_proj(xx, norm1_g[l], mod_x[0], mod_x[1], wg, wa, ba, lb, tm=256)
        gc = _in_proj(xc, norm1_g[l], mod_c[0], mod_c[1], wg, wa, ba, lb, tm=256)
        outs_c, outs_x = _token_mixer(gc, gx, p, lam_init, last)
        xx, hx, hxb = _merge_out(outs_x, gx[4], wb, wo, xx, mod_x[2], norm2_g[l], mod_x[3], mod_x[4], tm=512)
        xx = _expert_choice_moe(xx, mod_x[5], hx, hxb, moe_router[l], moe_w1, moe_w3, moe_w2, l)
        if not last:
            xc, hc, hcb = _merge_out(outs_c, gc[4], wb, wo, xc, mod_c[2], norm2_g[l], mod_c[3], mod_c[4], tm=256)
            xc = _expert_choice_moe(xc, mod_c[5], hc, hcb, moe_router[l], moe_w1, moe_w3, moe_w2, l)
    return xx
```
